```python
import math
import jax, jax.numpy as jnp
from jax import lax
import numpy as np

D_MODEL = 2048
BATCH = 1
SEQ = 8192
DEPTH = 1
DEC_BATCH = 32
DEC_SEQ = 1
PAST_LEN = 8192
PAGE_SIZE = 128

HEAD_DIM = 128
MIX_WIDTH = D_MODEL
B_HEADS = D_MODEL // (2 * HEAD_DIM)
A_HEADS = D_MODEL // (2 * HEAD_DIM)
B_WIDTH = B_HEADS * HEAD_DIM
A_WIDTH = A_HEADS * HEAD_DIM
PROJ_COLS = 3 * B_WIDTH + 2 * A_WIDTH
CHUNK = 128
MOBA_BLOCK = 256
MOBA_TOPK = 3
Q_BLOCK = 32
D_FF = 5632
LN_EPS = 1e-5
DN_ALPHA = (2 * DEPTH) ** 0.25
DN_BETA = (8 * DEPTH) ** -0.25

kernel_name = 'hymba_moba_gmlp_macaron_deepnorm_step'


def layer_norm(x, g, b):
    xf = x.astype(jnp.float32)
    mu = jnp.mean(xf, -1, keepdims=True)
    var = jnp.mean(jnp.square(xf - mu), -1, keepdims=True)
    return ((xf - mu) * lax.rsqrt(var + LN_EPS) * g + b).astype(x.dtype)


def post_norm(x, f, g, b):
    return layer_norm(DN_ALPHA * x + f, g, b)


def swiglu(x, w_gate, w_up, w_down):
    return (jax.nn.silu(x @ w_gate) * (x @ w_up)) @ w_down


def alibi_slopes(n):
    return jnp.exp2(-8.0 * jnp.arange(1, n + 1, dtype=jnp.float32) / n)


def split_proj(p):
    lead = p.shape[:-1]
    q = p[..., :B_WIDTH].reshape(lead + (B_HEADS, HEAD_DIM))
    k = p[..., B_WIDTH:2 * B_WIDTH].reshape(lead + (B_HEADS, HEAD_DIM))
    v = p[..., 2 * B_WIDTH:3 * B_WIDTH].reshape(lead + (B_HEADS, HEAD_DIM))
    u = p[..., 3 * B_WIDTH:3 * B_WIDTH + A_WIDTH].reshape(lead + (A_HEADS, HEAD_DIM))
    va = p[..., 3 * B_WIDTH + A_WIDTH:].reshape(lead + (A_HEADS, HEAD_DIM))
    return q, k, v, u, va


def moba_attend(q, q_pos, k_own, v_own, own_pos, slopes, k_sel=None, v_sel=None, sel_pos=None, sel_ok=None):
    scale = HEAD_DIM ** -0.5
    m = slopes[:, None, None]
    s_own = jnp.einsum('bhqd,bhkd->bhqk', q, k_own, preferred_element_type=jnp.float32) * scale
    d_own = (q_pos[:, None] - own_pos[None, :]).astype(jnp.float32)
    s_own = jnp.where(d_own >= 0, s_own - m * d_own, -jnp.inf)
    if k_sel is None:
        p = jax.nn.softmax(s_own, axis=-1)
        return jnp.einsum('bhqk,bhkd->bhqd', p.astype(v_own.dtype), v_own).astype(q.dtype)
    s_sel = jnp.einsum('bhqd,bhqkd->bhqk', q, k_sel, preferred_element_type=jnp.float32) * scale
    d_sel = (q_pos[:, None] - sel_pos).astype(jnp.float32)
    s_sel = s_sel - m * d_sel
    if sel_ok is not None:
        s_sel = jnp.where(sel_ok, s_sel, -jnp.inf)
    n_sel = s_sel.shape[-1]
    p = jax.nn.softmax(jnp.concatenate([s_sel, s_own], -1), axis=-1)
    out = (jnp.einsum('bhqk,bhqkd->bhqd', p[..., :n_sel].astype(v_sel.dtype), v_sel)
           + jnp.einsum('bhqk,bhkd->bhqd', p[..., n_sel:].astype(v_own.dtype), v_own))
    return out.astype(q.dtype)


def moba_prompt(q, k, v, slopes):
    nb, t_len = q.shape[0], q.shape[1]
    n_blk = -(-t_len // MOBA_BLOCK)
    pad = n_blk * MOBA_BLOCK - t_len
    qh = q.transpose(0, 2, 1, 3)
    def to_blocks(a):
        a = jnp.pad(a, ((0, 0), (0, pad), (0, 0), (0, 0))).transpose(0, 2, 1, 3)
        return a.reshape(nb, B_HEADS, n_blk, MOBA_BLOCK, HEAD_DIM)
    kb, vb = to_blocks(k), to_blocks(v)
    n_sel = min(MOBA_TOPK, n_blk - 1)
    n_qb = t_len // Q_BLOCK
    qb = qh.reshape(nb, B_HEADS, n_qb, Q_BLOCK, HEAD_DIM).transpose(2, 0, 1, 3, 4)
    blk_off = jnp.arange(MOBA_BLOCK)
    bidx = jnp.arange(nb)[:, None, None, None]
    hidx = jnp.arange(B_HEADS)[None, :, None, None]
    xs = (jnp.arange(n_qb), qb)
    if n_sel > 0:
        pos = jnp.arange(t_len)
        cur = pos // MOBA_BLOCK
        kmean = jnp.mean(kb.astype(jnp.float32), axis=3)
        gate = jnp.einsum('bhtd,bhnd->bhtn', qh.astype(jnp.float32), kmean)
        past = jnp.arange(n_blk)[None, :] < cur[:, None]
        gate = jnp.where(past, gate, -jnp.inf)
        _, sel = lax.top_k(gate, n_sel)
        valid = sel < cur[:, None]
        to_qb = lambda a: a.reshape(nb, B_HEADS, n_qb, Q_BLOCK, n_sel).transpose(2, 0, 1, 3, 4)
        xs = xs + (to_qb(sel), to_qb(valid))

    def body(args):
        i, q_i = args[0], args[1]
        t0 = i * Q_BLOCK
        q_pos = t0 + jnp.arange(Q_BLOCK)
        ob = t0 // MOBA_BLOCK
        k_own = lax.dynamic_index_in_dim(kb, ob, axis=2, keepdims=False)
        v_own = lax.dynamic_index_in_dim(vb, ob, axis=2, keepdims=False)
        own_pos = ob * MOBA_BLOCK + blk_off
        if n_sel == 0:
            return moba_attend(q_i, q_pos, k_own, v_own, own_pos, slopes)
        sel_i, val_i = args[2], args[3]
        flat = (nb, B_HEADS, Q_BLOCK, n_sel * MOBA_BLOCK)
        k_sel = kb[bidx, hidx, sel_i].reshape(flat + (HEAD_DIM,))
        v_sel = vb[bidx, hidx, sel_i].reshape(flat + (HEAD_DIM,))
        sel_pos = (sel_i[..., None] * MOBA_BLOCK + blk_off).reshape(flat)
        sel_ok = jnp.broadcast_to(val_i[..., None], sel_i.shape + (MOBA_BLOCK,)).reshape(flat)
        return moba_attend(q_i, q_pos, k_own, v_own, own_pos, slopes, k_sel, v_sel, sel_pos, sel_ok)

    out = lax.map(body, xs)
    return out.transpose(1, 0, 3, 2, 4).reshape(nb, t_len, B_HEADS, HEAD_DIM)


def moba_sample(q, k, v, cache_k, cache_v, page_table, layer, slopes):
    n_seq, s_len = q.shape[0], q.shape[1]
    n_pages = page_table.shape[1]
    past_len = n_pages * PAGE_SIZE
    ppb = MOBA_BLOCK // PAGE_SIZE
    n_full = past_len // MOBA_BLOCK
    qh = q.transpose(0, 2, 1, 3)
    q_pos = past_len + jnp.arange(s_len)
    blk_off = jnp.arange(MOBA_BLOCK)
    tail = page_table[:, n_full * ppb:]
    n_tail = tail.shape[1] * PAGE_SIZE
    k_tail = cache_k[tail, :, layer].reshape(n_seq, n_tail, B_HEADS, HEAD_DIM)
    v_tail = cache_v[tail, :, layer].reshape(n_seq, n_tail, B_HEADS, HEAD_DIM)
    k_own = jnp.concatenate([k_tail.astype(k.dtype), k], 1).transpose(0, 2, 1, 3)
    v_own = jnp.concatenate([v_tail.astype(v.dtype), v], 1).transpose(0, 2, 1, 3)
    own_pos = n_full * MOBA_BLOCK + jnp.arange(n_tail + s_len)
    n_sel = min(MOBA_TOPK, n_full)
    if n_sel == 0:
        out = moba_attend(qh, q_pos, k_own, v_own, own_pos, slopes)
        return out.transpose(0, 2, 1, 3)
    k_full = cache_k[page_table[:, :n_full * ppb], :, layer]
    kmean = jnp.mean(k_full.astype(jnp.float32).reshape(n_seq, n_full, MOBA_BLOCK, B_HEADS, HEAD_DIM), axis=2)
    gate = jnp.einsum('bhqd,bnhd->bhqn', qh.astype(jnp.float32), kmean)
    _, sel = lax.top_k(gate, n_sel)
    logical = sel[..., None] * ppb + jnp.arange(ppb)
    phys = page_table[jnp.arange(n_seq)[:, None, None, None, None], logical]
    hidx = jnp.arange(B_HEADS)[None, :, None, None, None]
    flat = (n_seq, B_HEADS, s_len, n_sel * MOBA_BLOCK)
    k_sel = cache_k[phys, :, layer, hidx].reshape(flat + (HEAD_DIM,)).astype(q.dtype)
    v_sel = cache_v[phys, :, layer, hidx].reshape(flat + (HEAD_DIM,)).astype(v.dtype)
    sel_pos = (sel[..., None] * MOBA_BLOCK + blk_off).reshape(flat)
    out = moba_attend(qh, q_pos, k_own, v_own, own_pos, slopes, k_sel, v_sel, sel_pos)
    return out.transpose(0, 2, 1, 3)


def causal_spatial(ws):
    mask = jnp.tril(jnp.ones((CHUNK, CHUNK), dtype=bool))
    return jnp.where(mask, ws, jnp.zeros_like(ws))


def chunk_gmlp_prompt(u, va, g, bn, ws, bs):
    nb, t_len = u.shape[0], u.shape[1]
    van = layer_norm(va, g, bn)
    vc = van.reshape(nb, t_len // CHUNK, CHUNK, A_HEADS, HEAD_DIM)
    mixed = jnp.einsum('hts,bnshd->bnthd', causal_spatial(ws).astype(vc.dtype), vc) + bs.T[None, None, :, :, None]
    return u * mixed.reshape(u.shape)


def chunk_gmlp_sample(u, va, g, bn, ws, bs):
    s_len = u.shape[1]
    van = layer_norm(va, g, bn)
    wm = causal_spatial(ws)[:, :s_len, :s_len].astype(van.dtype)
    mixed = jnp.einsum('hts,bshd->bthd', wm, van) + bs[:, :s_len].T[None, :, :, None]
    return u * mixed, van


def setup_inputs(seed: int = 0) -> dict:
    key = jax.random.key(seed)
    ks = jax.random.split(key, 24)
    n_pages = PAST_LEN // PAGE_SIZE
    n_used = DEC_BATCH * n_pages
    n_pool = n_used + (n_used + 3) // 4
    f32 = jnp.float32
    def nrm(k, shape, s=1.0):
        return jax.random.normal(k, shape, f32) * s
    def gain(k, shape):
        return 1.0 + 0.02 * jax.random.normal(k, shape, f32)
    def bias(k, shape):
        return 0.02 * jax.random.normal(k, shape, f32)
    col_scale = jnp.concatenate([jnp.ones((2 * B_WIDTH,), f32), jnp.full((B_WIDTH + 2 * A_WIDTH,), DN_BETA, f32)])
    page_table = jax.random.permutation(ks[4], n_pool)[:n_used].astype(jnp.int32).reshape(DEC_BATCH, n_pages)
    return {
        'x_prompt': nrm(ks[0], (BATCH, SEQ, D_MODEL)),
        'x_sample': nrm(ks[1], (DEC_BATCH, DEC_SEQ, D_MODEL)),
        'cache_k': nrm(ks[2], (n_pool, PAGE_SIZE, DEPTH, B_HEADS, HEAD_DIM)),
        'cache_v': nrm(ks[3], (n_pool, PAGE_SIZE, DEPTH, B_HEADS, HEAD_DIM)),
        'page_table': page_table,
        'ffn1_w_gate': nrm(ks[5], (DEPTH, D_MODEL, D_FF), D_MODEL ** -0.5),
        'ffn1_w_up': nrm(ks[6], (DEPTH, D_MODEL, D_FF), DN_BETA * D_MODEL ** -0.5),
        'ffn1_w_down': nrm(ks[7], (DEPTH, D_FF, D_MODEL), DN_BETA * D_FF ** -0.5),
        'ln1_g': gain(ks[8], (DEPTH, D_MODEL)),
        'ln1_b': bias(ks[9], (DEPTH, D_MODEL)),
        'w_in': nrm(ks[10], (DEPTH, D_MODEL, PROJ_COLS), D_MODEL ** -0.5) * col_scale,
        'a_norm_g': gain(ks[11], (DEPTH, A_HEADS, HEAD_DIM)),
        'a_norm_b': bias(ks[12], (DEPTH, A_HEADS, HEAD_DIM)),
        'a_spatial_w': nrm(ks[13], (DEPTH, A_HEADS, CHUNK, CHUNK), CHUNK ** -0.5),
        'a_spatial_b': gain(ks[14], (DEPTH, A_HEADS, CHUNK)),
        'w_out': nrm(ks[15], (DEPTH, MIX_WIDTH, D_MODEL), DN_BETA * MIX_WIDTH ** -0.5),
        'ln2_g': gain(ks[16], (DEPTH, D_MODEL)),
        'ln2_b': bias(ks[17], (DEPTH, D_MODEL)),
        'ffn2_w_gate': nrm(ks[18], (DEPTH, D_MODEL, D_FF), D_MODEL ** -0.5),
        'ffn2_w_up': nrm(ks[19], (DEPTH, D_MODEL, D_FF), DN_BETA * D_MODEL ** -0.5),
        'ffn2_w_down': nrm(ks[20], (DEPTH, D_FF, D_MODEL), DN_BETA * D_FF ** -0.5),
        'ln3_g': gain(ks[21], (DEPTH, D_MODEL)),
        'ln3_b': bias(ks[22], (DEPTH, D_MODEL)),
    }


def reference(x_prompt, x_sample, cache_k, cache_v, page_table,
              ffn1_w_gate, ffn1_w_up, ffn1_w_down, ln1_g, ln1_b,
              w_in, a_norm_g, a_norm_b, a_spatial_w, a_spatial_b, w_out, ln2_g, ln2_b,
              ffn2_w_gate, ffn2_w_up, ffn2_w_down, ln3_g, ln3_b):
    slopes = alibi_slopes(B_HEADS)
    hp, hs = x_prompt, x_sample
    kp_rows, vp_rows, ks_rows, vs_rows, va_rows = [], [], [], [], []
    for l in range(DEPTH):
        hp = post_norm(hp, 0.5 * swiglu(hp, ffn1_w_gate[l], ffn1_w_up[l], ffn1_w_down[l]), ln1_g[l], ln1_b[l])
        hs = post_norm(hs, 0.5 * swiglu(hs, ffn1_w_gate[l], ffn1_w_up[l], ffn1_w_down[l]), ln1_g[l], ln1_b[l])
        qp, kp, vp, up, vap = split_proj(hp @ w_in[l])
        attn_p = moba_prompt(qp, kp, vp, slopes)
        gm_p = chunk_gmlp_prompt(up, vap, a_norm_g[l], a_norm_b[l], a_spatial_w[l], a_spatial_b[l])
        mix_p = jnp.concatenate([attn_p.reshape(hp.shape[:2] + (B_WIDTH,)), gm_p.reshape(hp.shape[:2] + (A_WIDTH,))], -1) @ w_out[l]
        hp = post_norm(hp, mix_p, ln2_g[l], ln2_b[l])
        qs, ks_, vs_, us, vas = split_proj(hs @ w_in[l])
        attn_s = moba_sample(qs, ks_, vs_, cache_k, cache_v, page_table, l, slopes)
        gm_s, van_s = chunk_gmlp_sample(us, vas, a_norm_g[l], a_norm_b[l], a_spatial_w[l], a_spatial_b[l])
        mix_s = jnp.concatenate([attn_s.reshape(hs.shape[:2] + (B_WIDTH,)), gm_s.reshape(hs.shape[:2] + (A_WIDTH,))], -1) @ w_out[l]
        hs = post_norm(hs, mix_s, ln2_g[l], ln2_b[l])
        hp = post_norm(hp, 0.5 * swiglu(hp, ffn2_w_gate[l], ffn2_w_up[l], ffn2_w_down[l]), ln3_g[l], ln3_b[l])
        hs = post_norm(hs, 0.5 * swiglu(hs, ffn2_w_gate[l], ffn2_w_up[l], ffn2_w_down[l]), ln3_g[l], ln3_b[l])
        kp_rows.append(kp)
        vp_rows.append(vp)
        ks_rows.append(ks_)
        vs_rows.append(vs_)
        va_rows.append(van_s)
    k_prompt = jnp.stack(kp_rows, axis=2)
    v_prompt = jnp.stack(vp_rows, axis=2)
    k_sample = jnp.stack(ks_rows, axis=2)
    v_sample = jnp.stack(vs_rows, axis=2)
    a_v_sample = jnp.stack(va_rows, axis=2)
    return (hp, hs, k_prompt, v_prompt, k_sample, v_sample, a_v_sample)
```

```python
import functools

import jax
import jax.numpy as jnp
from jax import lax
from jax.experimental import pallas as pl
from jax.experimental.pallas import tpu as pltpu

D_MODEL = 2048
HEAD_DIM = 128
N_HEADS = 8
WIDTH = N_HEADS * HEAD_DIM
CHUNK = 128
MOBA_BLOCK = 256
MOBA_TOPK = 3
PAGE_SIZE = 128
PAGES_PER_BLOCK = MOBA_BLOCK // PAGE_SIZE
D_FF = 5632
LN_EPS = 1e-5
DEPTH = 1
DN_ALPHA = (2 * DEPTH) ** 0.25
ATTN_SCALE = HEAD_DIM ** -0.5

MASKED = -1e30
MASKED_BF16 = -(2.0 ** 100)

LANES = 128
VMEM_LIMIT = 56 * 1024 * 1024

BF16 = jnp.bfloat16
F32 = jnp.float32


def _layer_norm(x, g, b):
    mu = jnp.mean(x, axis=-1, keepdims=True)
    xc = x - mu
    var = jnp.mean(xc * xc, axis=-1, keepdims=True)
    return xc * lax.rsqrt(var + LN_EPS) * g + b


def _dot(a, b):
    return jnp.dot(a, b, preferred_element_type=F32)


def _dot_nt(a, b):
    return lax.dot_general(a, b, (((1,), (1,)), ((), ())), preferred_element_type=F32)


def _ffn_ln_body(x_ref, wg_ref, wu_ref, wd_ref, g_ref, b_ref, o_ref, xb_ref, acc_ref):
    j = pl.program_id(1)

    @pl.when(j == 0)
    def _():
        xb_ref[...] = x_ref[...].astype(BF16)
        acc_ref[...] = jnp.zeros_like(acc_ref)

    xb = xb_ref[...]
    gate = _dot(xb, wg_ref[...])
    up = _dot(xb, wu_ref[...])
    act = gate * jax.nn.sigmoid(gate) * up
    acc_ref[...] += _dot(act.astype(BF16), wd_ref[...])

    @pl.when(j == pl.num_programs(1) - 1)
    def _():
        y = DN_ALPHA * x_ref[...] + 0.5 * acc_ref[...]
        o_ref[...] = _layer_norm(y, g_ref[...], b_ref[...])


def _ffn_ln(x, wg, wu, wd, g, b, *, tm, tf, name):
    rows = x.shape[0]
    return pl.pallas_call(
        _ffn_ln_body,
        grid=(rows // tm, D_FF // tf),
        in_specs=[
            pl.BlockSpec((tm, D_MODEL), lambda i, j: (i, 0)),
            pl.BlockSpec((D_MODEL, tf), lambda i, j: (0, j)),
            pl.BlockSpec((D_MODEL, tf), lambda i, j: (0, j)),
            pl.BlockSpec((tf, D_MODEL), lambda i, j: (j, 0)),
            pl.BlockSpec((1, D_MODEL), lambda i, j: (0, 0)),
            pl.BlockSpec((1, D_MODEL), lambda i, j: (0, 0)),
        ],
        out_specs=pl.BlockSpec((tm, D_MODEL), lambda i, j: (i, 0)),
        out_shape=jax.ShapeDtypeStruct((rows, D_MODEL), F32),
        scratch_shapes=[pltpu.VMEM((tm, D_MODEL), BF16), pltpu.VMEM((tm, D_MODEL), F32)],
        compiler_params=pltpu.CompilerParams(
            dimension_semantics=("parallel", "arbitrary"), vmem_limit_bytes=VMEM_LIMIT),
        name=name,
    )(x, wg, wu, wd, g, b)


def _head_layer_norm(va, g, b):
    parts = []
    for h in range(N_HEADS):
        sl = slice(h * HEAD_DIM, (h + 1) * HEAD_DIM)
        parts.append(_layer_norm(va[:, sl], g[:, sl], b[:, sl]))
    return jnp.concatenate(parts, axis=-1)


def _proj_prompt_body(x_ref, w_ref, ag_ref, ab_ref, ws_ref, bs_ref,
                      q_ref, k_ref, v_ref, gm_ref, xb_ref, u_ref):
    j = pl.program_id(1)

    @pl.when(j == 0)
    def _():
        xb_ref[...] = x_ref[...].astype(BF16)

    p = _dot(xb_ref[...], w_ref[...])

    @pl.when(j == 0)
    def _():
        q_ref[...] = p.astype(BF16)

    @pl.when(j == 1)
    def _():
        k_ref[...] = p

    @pl.when(j == 2)
    def _():
        v_ref[...] = p

    @pl.when(j == 3)
    def _():
        u_ref[...] = p

    @pl.when(j == 4)
    def _():
        van = _head_layer_norm(p, ag_ref[...], ab_ref[...]).astype(BF16)
        row = lax.broadcasted_iota(jnp.int32, (CHUNK, CHUNK), 0)
        col = lax.broadcasted_iota(jnp.int32, (CHUNK, CHUNK), 1)
        causal = row >= col
        n_chunks = p.shape[0] // CHUNK
        for h in range(N_HEADS):
            sl = slice(h * HEAD_DIM, (h + 1) * HEAD_DIM)
            w_h = jnp.where(causal, ws_ref[h], 0.0).astype(BF16)
            for c in range(n_chunks):
                rs = slice(c * CHUNK, (c + 1) * CHUNK)
                mixed = _dot(w_h, van[rs, sl]) + bs_ref[:, sl]
                gm_ref[rs, sl] = (u_ref[rs, sl] * mixed).astype(BF16)


def _proj_prompt(x, w_in, ag, ab, ws, bs_exp, *, tm):
    rows = x.shape[0]
    row_blk = lambda i, j: (i, 0)
    const2 = lambda i, j: (0, 0)
    return pl.pallas_call(
        _proj_prompt_body,
        grid=(rows // tm, 5),
        in_specs=[
            pl.BlockSpec((tm, D_MODEL), row_blk),
            pl.BlockSpec((D_MODEL, WIDTH), lambda i, j: (0, j)),
            pl.BlockSpec((1, WIDTH), const2),
            pl.BlockSpec((1, WIDTH), const2),
            pl.BlockSpec((N_HEADS, CHUNK, CHUNK), lambda i, j: (0, 0, 0)),
            pl.BlockSpec((CHUNK, WIDTH), const2),
        ],
        out_specs=[
            pl.BlockSpec((tm, WIDTH), row_blk),
            pl.BlockSpec((tm, WIDTH), row_blk),
            pl.BlockSpec((tm, WIDTH), row_blk),
            pl.BlockSpec((tm, WIDTH), row_blk),
        ],
        out_shape=[
            jax.ShapeDtypeStruct((rows, WIDTH), BF16),
            jax.ShapeDtypeStruct((rows, WIDTH), F32),
            jax.ShapeDtypeStruct((rows, WIDTH), F32),
            jax.ShapeDtypeStruct((rows, WIDTH), BF16),
        ],
        scratch_shapes=[pltpu.VMEM((tm, D_MODEL), BF16), pltpu.VMEM((tm, WIDTH), F32)],
        compiler_params=pltpu.CompilerParams(
            dimension_semantics=("parallel", "arbitrary"), vmem_limit_bytes=VMEM_LIMIT),
        name="proj_prompt",
    )(x, w_in, ag, ab, ws, bs_exp)


def _proj_sample_body(x_ref, w_ref, ag_ref, ab_ref, w00_ref, b0_ref,
                      q_ref, k_ref, v_ref, gm_ref, van_ref, xb_ref, u_ref):
    j = pl.program_id(0)

    @pl.when(j == 0)
    def _():
        xb_ref[...] = x_ref[...].astype(BF16)

    p = _dot(xb_ref[...], w_ref[...])

    @pl.when(j == 0)
    def _():
        q_ref[...] = p

    @pl.when(j == 1)
    def _():
        k_ref[...] = p

    @pl.when(j == 2)
    def _():
        v_ref[...] = p

    @pl.when(j == 3)
    def _():
        u_ref[...] = p

    @pl.when(j == 4)
    def _():
        van = _head_layer_norm(p, ag_ref[...], ab_ref[...])
        van_ref[...] = van
        gm_ref[...] = (u_ref[...] * (w00_ref[...] * van + b0_ref[...])).astype(BF16)


def _proj_sample(x, w_in, ag, ab, w00, b0):
    rows = x.shape[0]
    whole = lambda j: (0, 0)
    return pl.pallas_call(
        _proj_sample_body,
        grid=(5,),
        in_specs=[
            pl.BlockSpec((rows, D_MODEL), whole),
            pl.BlockSpec((D_MODEL, WIDTH), lambda j: (0, j)),
            pl.BlockSpec((1, WIDTH), whole),
            pl.BlockSpec((1, WIDTH), whole),
            pl.BlockSpec((1, WIDTH), whole),
            pl.BlockSpec((1, WIDTH), whole),
        ],
        out_specs=[pl.BlockSpec((rows, WIDTH), whole)] * 5,
        out_shape=[
            jax.ShapeDtypeStruct((rows, WIDTH), F32),
            jax.ShapeDtypeStruct((rows, WIDTH), F32),
            jax.ShapeDtypeStruct((rows, WIDTH), F32),
            jax.ShapeDtypeStruct((rows, WIDTH), BF16),
            jax.ShapeDtypeStruct((rows, WIDTH), F32),
        ],
        scratch_shapes=[pltpu.VMEM((rows, D_MODEL), BF16), pltpu.VMEM((rows, WIDTH), F32)],
        compiler_params=pltpu.CompilerParams(
            dimension_semantics=("arbitrary",), vmem_limit_bytes=VMEM_LIMIT),
        name="proj_sample",
    )(x, w_in, ag, ab, w00, b0)


def _out_ln_body(a_ref, m_ref, h_ref, wa_ref, wm_ref, g_ref, b_ref, o_ref):
    mix = _dot(a_ref[...].astype(BF16), wa_ref[...]) + _dot(m_ref[...], wm_ref[...])
    o_ref[...] = _layer_norm(DN_ALPHA * h_ref[...] + mix, g_ref[...], b_ref[...])


def _out_ln(attn, gm, h, w_attn, w_gm, g, b, *, tm, name):
    rows = h.shape[0]
    row_blk = lambda i: (i, 0)
    const = lambda i: (0, 0)
    return pl.pallas_call(
        _out_ln_body,
        grid=(rows // tm,),
        in_specs=[
            pl.BlockSpec((tm, WIDTH), row_blk),
            pl.BlockSpec((tm, WIDTH), row_blk),
            pl.BlockSpec((tm, D_MODEL), row_blk),
            pl.BlockSpec((WIDTH, D_MODEL), const),
            pl.BlockSpec((WIDTH, D_MODEL), const),
            pl.BlockSpec((1, D_MODEL), const),
            pl.BlockSpec((1, D_MODEL), const),
        ],
        out_specs=pl.BlockSpec((tm, D_MODEL), row_blk),
        out_shape=jax.ShapeDtypeStruct((rows, D_MODEL), F32),
        compiler_params=pltpu.CompilerParams(
            dimension_semantics=("parallel",), vmem_limit_bytes=VMEM_LIMIT),
        name=name,
    )(attn, gm, h, w_attn, w_gm, g, b)


def _top_blocks(gate, n_valid):
    lane = lax.broadcasted_iota(jnp.int32, gate.shape, 1).astype(F32)
    valid = lane < n_valid.astype(F32)
    g = jnp.where(valid, gate, -jnp.inf)
    sel = jnp.zeros(gate.shape, dtype=jnp.bool_)
    for _ in range(MOBA_TOPK):
        mx = jnp.max(g, axis=-1, keepdims=True)
        first = jnp.min(jnp.where(g == mx, lane, float(LANES)), axis=-1, keepdims=True)
        hit = lane == first
        sel = sel | (hit & valid)
        g = jnp.where(hit, -jnp.inf, g)
    return sel


def _attn_prompt_body(slope_ref, q_ref, k_ref, v_ref, o_ref,
                      ka_ref, vb_ref, km_ref, toep_ref, m_ref, l_ref, acc_ref):
    h = pl.program_id(0)
    qb = pl.program_id(1)
    n_blocks = k_ref.shape[0] // MOBA_BLOCK
    slope = slope_ref[h]

    @pl.when(qb == 0)
    def _():
        km_ref[...] = jnp.zeros_like(km_ref)
        lane = lax.broadcasted_iota(jnp.int32, (MOBA_BLOCK, LANES), 1)

        def prep(n, carry):
            rows = pl.ds(pl.multiple_of(n * MOBA_BLOCK, MOBA_BLOCK), MOBA_BLOCK)
            kblk = k_ref[rows, :]
            ka_ref[rows, :HEAD_DIM] = kblk.astype(BF16)
            ka_ref[rows, HEAD_DIM:] = jnp.where(lane == n, 1.0, 0.0).astype(BF16)
            vb_ref[rows, :] = v_ref[rows, :].astype(BF16)
            km_ref[pl.ds(n, 1), :] = jnp.mean(kblk, axis=0, keepdims=True)
            return carry

        lax.fori_loop(0, n_blocks, prep, 0)
        row = lax.broadcasted_iota(jnp.int32, (MOBA_BLOCK, MOBA_BLOCK), 0)
        col = lax.broadcasted_iota(jnp.int32, (MOBA_BLOCK, MOBA_BLOCK), 1)
        toep_ref[...] = slope * (row - col).astype(F32)

    q = q_ref[...]
    gate = _dot_nt(q, km_ref[...].astype(BF16))
    sel = _top_blocks(gate, qb)
    lane = lax.broadcasted_iota(jnp.int32, gate.shape, 1)
    sel_bias = jnp.where(sel | (lane == qb), 0.0, MASKED_BF16).astype(BF16)
    qa = jnp.concatenate([q, sel_bias], axis=-1)

    def scores(kb):
        rows = pl.ds(pl.multiple_of(kb * MOBA_BLOCK, MOBA_BLOCK), MOBA_BLOCK)
        s = _dot_nt(qa, ka_ref[rows, :]) * ATTN_SCALE
        dist0 = ((qb - kb) * MOBA_BLOCK).astype(F32)
        return s - toep_ref[...] - slope * dist0, vb_ref[rows, :]

    s, vblk = scores(qb)
    row = lax.broadcasted_iota(jnp.int32, s.shape, 0)
    col = lax.broadcasted_iota(jnp.int32, s.shape, 1)
    s = jnp.where(row >= col, s, MASKED)
    m0 = jnp.max(s, axis=-1, keepdims=True)
    p = jnp.exp(s - m0)
    m_ref[...] = m0
    l_ref[...] = jnp.sum(p, axis=-1, keepdims=True)
    acc_ref[...] = _dot(p.astype(BF16), vblk)

    def past(kb, carry):
        s, vblk = scores(kb)
        m_old = m_ref[...]
        m_new = jnp.maximum(m_old, jnp.max(s, axis=-1, keepdims=True))
        alpha = jnp.exp(m_old - m_new)
        p = jnp.exp(s - m_new)
        l_ref[...] = alpha * l_ref[...] + jnp.sum(p, axis=-1, keepdims=True)
        acc_ref[...] = alpha * acc_ref[...] + _dot(p.astype(BF16), vblk)
        m_ref[...] = m_new
        return carry

    lax.fori_loop(0, qb, past, 0)
    o_ref[...] = (acc_ref[...] / l_ref[...]).astype(o_ref.dtype)


def _attn_prompt(slopes, q, k, v):
    t_len = q.shape[0]
    n_qb = t_len // MOBA_BLOCK
    grid_spec = pltpu.PrefetchScalarGridSpec(
        num_scalar_prefetch=1,
        grid=(N_HEADS, n_qb),
        in_specs=[
            pl.BlockSpec((MOBA_BLOCK, HEAD_DIM), lambda h, i, s: (i, h)),
            pl.BlockSpec((t_len, HEAD_DIM), lambda h, i, s: (0, h)),
            pl.BlockSpec((t_len, HEAD_DIM), lambda h, i, s: (0, h)),
        ],
        out_specs=pl.BlockSpec((MOBA_BLOCK, HEAD_DIM), lambda h, i, s: (i, h)),
        scratch_shapes=[
            pltpu.VMEM((t_len, 2 * HEAD_DIM), BF16),
            pltpu.VMEM((t_len, HEAD_DIM), BF16),
            pltpu.VMEM((LANES, HEAD_DIM), F32),
            pltpu.VMEM((MOBA_BLOCK, MOBA_BLOCK), F32),
            pltpu.VMEM((MOBA_BLOCK, 1), F32),
            pltpu.VMEM((MOBA_BLOCK, 1), F32),
            pltpu.VMEM((MOBA_BLOCK, HEAD_DIM), F32),
        ],
    )
    return pl.pallas_call(
        _attn_prompt_body,
        grid_spec=grid_spec,
        out_shape=jax.ShapeDtypeStruct((t_len, WIDTH), BF16),
        compiler_params=pltpu.CompilerParams(
            dimension_semantics=("parallel", "arbitrary"), vmem_limit_bytes=VMEM_LIMIT),
        name="attn_prompt",
    )(slopes, q, k, v)


GROUP_PAGES = 8
GROUP_BLOCKS = GROUP_PAGES // PAGES_PER_BLOCK


def _select_sample_body(pt_ref, q_ref, ck_ref, sel_ref, buf_ref, sem_ref, ksum_ref):
    n_seq, n_pages = pt_ref.shape
    groups_per_seq = n_pages // GROUP_PAGES
    n_steps = n_seq * groups_per_seq
    n_full = n_pages // PAGES_PER_BLOCK

    def page_copy(page, slot, p):
        return pltpu.make_async_copy(
            ck_ref.at[page, :, 0], buf_ref.at[slot, pl.ds(p * PAGE_SIZE, PAGE_SIZE)], sem_ref.at[slot])

    def start(t, slot):
        b = t // groups_per_seq
        g = t % groups_per_seq
        for p in range(GROUP_PAGES):
            page_copy(pt_ref[b, g * GROUP_PAGES + p], slot, p).start()

    def wait(slot):
        for p in range(GROUP_PAGES):
            page_copy(0, slot, p).wait()

    start(0, 0)

    def step(t, carry):
        slot = t % 2
        b = t // groups_per_seq
        g = t % groups_per_seq

        @pl.when(t + 1 < n_steps)
        def _():
            start(t + 1, 1 - slot)

        wait(slot)
        for n in range(GROUP_BLOCKS):
            blk = buf_ref[slot, pl.ds(n * MOBA_BLOCK, MOBA_BLOCK)]
            ksum_ref[g * GROUP_BLOCKS + n] = jnp.sum(blk, axis=0)

        @pl.when(g == groups_per_seq - 1)
        def _():
            kmean = ksum_ref[...] / MOBA_BLOCK
            gate = jnp.sum(kmean * q_ref[b], axis=-1, keepdims=True)
            blk_id = lax.broadcasted_iota(jnp.int32, gate.shape, 0).astype(F32)
            lane = lax.broadcasted_iota(jnp.int32, (N_HEADS, LANES), 1)
            picks = jnp.zeros((N_HEADS, LANES), F32)
            for j in range(MOBA_TOPK):
                mx = jnp.max(gate, axis=0, keepdims=True)
                first = jnp.min(jnp.where(gate == mx, blk_id, float(n_full)), axis=0, keepdims=True)
                picks = jnp.where(lane == j, first[0], picks)
                gate = jnp.where(blk_id == first, -jnp.inf, gate)
            sel_ref[b] = picks.astype(jnp.int32)
        return carry

    lax.fori_loop(0, n_steps, step, 0)


def _select_sample(page_table, q, cache_k):
    n_seq, n_pages = page_table.shape
    assert n_pages % GROUP_PAGES == 0
    n_full = n_pages // PAGES_PER_BLOCK
    grid_spec = pltpu.PrefetchScalarGridSpec(
        num_scalar_prefetch=1,
        grid=(1,),
        in_specs=[
            pl.BlockSpec((n_seq, N_HEADS, HEAD_DIM), lambda i, pt: (0, 0, 0)),
            pl.BlockSpec(memory_space=pl.ANY),
        ],
        out_specs=pl.BlockSpec((n_seq, N_HEADS, LANES), lambda i, pt: (0, 0, 0)),
        scratch_shapes=[
            pltpu.VMEM((2, GROUP_PAGES * PAGE_SIZE, N_HEADS, HEAD_DIM), F32),
            pltpu.SemaphoreType.DMA((2,)),
            pltpu.VMEM((n_full, N_HEADS, HEAD_DIM), F32),
        ],
    )
    return pl.pallas_call(
        _select_sample_body,
        grid_spec=grid_spec,
        out_shape=jax.ShapeDtypeStruct((n_seq, N_HEADS, LANES), jnp.int32),
        compiler_params=pltpu.CompilerParams(
            dimension_semantics=("arbitrary",), vmem_limit_bytes=VMEM_LIMIT),
        name="select_sample",
    )(page_table, q.reshape(n_seq, N_HEADS, HEAD_DIM), cache_k)


def _attn_sample_body(pt_ref, sel_ref, slope_ref, q_ref, kn_ref, vn_ref, ck_ref, cv_ref, o_ref,
                      kbuf_ref, vbuf_ref, sem_ref, *, past_len):
    b = pl.program_id(0)
    n_sel = MOBA_TOPK

    def copies(h, j, pp):
        page = pt_ref[b, sel_ref[b, h * MOBA_TOPK + j] * PAGES_PER_BLOCK + pp]
        rows = pl.ds((j * PAGES_PER_BLOCK + pp) * PAGE_SIZE, PAGE_SIZE)
        return (pltpu.make_async_copy(ck_ref.at[page, :, 0, h, :], kbuf_ref.at[h, rows, :], sem_ref.at[0]),
                pltpu.make_async_copy(cv_ref.at[page, :, 0, h, :], vbuf_ref.at[h, rows, :], sem_ref.at[1]))

    gathers = [copies(h, j, pp) for h in range(N_HEADS) for j in range(n_sel)
               for pp in range(PAGES_PER_BLOCK)]
    for ck, cv in gathers:
        ck.start()
        cv.start()
    for ck, cv in gathers:
        ck.wait()
        cv.wait()

    n_keys = n_sel * MOBA_BLOCK
    lane = lax.broadcasted_iota(jnp.int32, (1, n_keys), 1)
    slot = lane // MOBA_BLOCK
    offs = lane % MOBA_BLOCK
    outs = []
    for h in range(N_HEADS):
        sl = slice(h * HEAD_DIM, (h + 1) * HEAD_DIM)
        slope = slope_ref[h]
        qh = q_ref[:, sl]
        kn = kn_ref[:, sl]
        vn = vn_ref[:, sl]
        q8 = jnp.broadcast_to(qh, (8, HEAD_DIM)).astype(BF16)
        s = _dot_nt(q8, kbuf_ref[h].astype(BF16))[:1] * ATTN_SCALE
        blk = jnp.zeros((1, n_keys), jnp.int32)
        for j in range(n_sel):
            blk = jnp.where(slot == j, sel_ref[b, h * MOBA_TOPK + j], blk)
        dist = (past_len - (blk * MOBA_BLOCK + offs)).astype(F32)
        s = s - slope * dist
        s_own = jnp.sum(qh * kn, axis=-1, keepdims=True) * ATTN_SCALE
        mx = jnp.maximum(jnp.max(s, axis=-1, keepdims=True), s_own)
        p = jnp.exp(s - mx)
        p_own = jnp.exp(s_own - mx)
        denom = jnp.sum(p, axis=-1, keepdims=True) + p_own
        p8 = jnp.broadcast_to(p, (8, n_keys)).astype(BF16)
        pv = _dot(p8, vbuf_ref[h].astype(BF16))[:1]
        outs.append((pv + p_own * vn) / denom)
    o_ref[...] = jnp.concatenate(outs, axis=-1)


def _attn_sample(page_table, sel, slopes, q, k_new, v_new, cache_k, cache_v):
    n_seq, n_pages = page_table.shape
    per_seq = pl.BlockSpec((None, 1, WIDTH), lambda b, *_: (b, 0, 0))
    grid_spec = pltpu.PrefetchScalarGridSpec(
        num_scalar_prefetch=3,
        grid=(n_seq,),
        in_specs=[
            per_seq, per_seq, per_seq,
            pl.BlockSpec(memory_space=pl.ANY),
            pl.BlockSpec(memory_space=pl.ANY),
        ],
        out_specs=per_seq,
        scratch_shapes=[
            pltpu.VMEM((N_HEADS, MOBA_TOPK * MOBA_BLOCK, HEAD_DIM), F32),
            pltpu.VMEM((N_HEADS, MOBA_TOPK * MOBA_BLOCK, HEAD_DIM), F32),
            pltpu.SemaphoreType.DMA((2,)),
        ],
    )
    return pl.pallas_call(
        functools.partial(_attn_sample_body, past_len=n_pages * PAGE_SIZE),
        grid_spec=grid_spec,
        out_shape=jax.ShapeDtypeStruct((n_seq, 1, WIDTH), F32),
        compiler_params=pltpu.CompilerParams(
            dimension_semantics=("arbitrary",), vmem_limit_bytes=VMEM_LIMIT),
        name="attn_sample",
    )(page_table, sel, slopes, q.reshape(n_seq, 1, WIDTH), k_new.reshape(n_seq, 1, WIDTH),
      v_new.reshape(n_seq, 1, WIDTH), cache_k, cache_v).reshape(n_seq, WIDTH)


def kernel(x_prompt, x_sample, cache_k, cache_v, page_table, ffn1_w_gate, ffn1_w_up, ffn1_w_down, ln1_g, ln1_b, w_in, a_norm_g, a_norm_b, a_spatial_w, a_spatial_b, w_out, ln2_g, ln2_b, ffn2_w_gate, ffn2_w_up, ffn2_w_down, ln3_g, ln3_b):
    assert x_prompt.shape[0] == 1 and x_sample.shape[1] == 1 and w_in.shape[0] == DEPTH == 1
    t_len = x_prompt.shape[1]
    n_seq = x_sample.shape[0]
    n_pool = cache_k.shape[0]
    xp = x_prompt.reshape(t_len, D_MODEL)
    xs = x_sample.reshape(n_seq, D_MODEL)
    slopes = jnp.exp2(-8.0 * jnp.arange(1, N_HEADS + 1, dtype=F32) / N_HEADS)

    bf = lambda w: w[0].astype(BF16)
    row = lambda v: v.reshape(1, -1)
    w1g, w1u, w1d = bf(ffn1_w_gate), bf(ffn1_w_up), bf(ffn1_w_down)
    w2g, w2u, w2d = bf(ffn2_w_gate), bf(ffn2_w_up), bf(ffn2_w_down)
    w_in_b, w_out_b = bf(w_in), bf(w_out)
    w_out_attn, w_out_gm = w_out_b[:WIDTH], w_out_b[WIDTH:]
    ag, ab = row(a_norm_g[0]), row(a_norm_b[0])
    ws, bs = a_spatial_w[0], a_spatial_b[0]
    bs_exp = jnp.repeat(bs.T, HEAD_DIM, axis=1)
    w00 = row(jnp.repeat(ws[:, 0, 0], HEAD_DIM))
    b0 = row(jnp.repeat(bs[:, 0], HEAD_DIM))

    hp = _ffn_ln(xp, w1g, w1u, w1d, row(ln1_g[0]), row(ln1_b[0]), tm=512, tf=512, name="ffn1_prompt")
    qp, kp, vp, gm_p = _proj_prompt(hp, w_in_b, ag, ab, ws, bs_exp, tm=512)
    attn_p = _attn_prompt(slopes, qp, kp, vp)
    hp = _out_ln(attn_p, gm_p, hp, w_out_attn, w_out_gm, row(ln2_g[0]), row(ln2_b[0]), tm=512, name="out_prompt")
    yp = _ffn_ln(hp, w2g, w2u, w2d, row(ln3_g[0]), row(ln3_b[0]), tm=512, tf=512, name="ffn2_prompt")

    hs = _ffn_ln(xs, w1g, w1u, w1d, row(ln1_g[0]), row(ln1_b[0]), tm=n_seq, tf=512, name="ffn1_sample")
    qs, ks, vs, gm_s, van_s = _proj_sample(hs, w_in_b, ag, ab, w00, b0)
    sel = _select_sample(page_table, qs, cache_k)
    sel = sel[:, :, :MOBA_TOPK].reshape(n_seq, N_HEADS * MOBA_TOPK)
    attn_s = _attn_sample(page_table, sel, slopes, qs, ks, vs, cache_k, cache_v)
    hs = _out_ln(attn_s, gm_s, hs, w_out_attn, w_out_gm, row(ln2_g[0]), row(ln2_b[0]), tm=n_seq, name="out_sample")
    ys = _ffn_ln(hs, w2g, w2u, w2d, row(ln3_g[0]), row(ln3_b[0]), tm=n_seq, tf=512, name="ffn2_sample")

    heads = (1, N_HEADS, HEAD_DIM)
    return (yp.reshape(1, t_len, D_MODEL), ys.reshape(n_seq, 1, D_MODEL),
            kp.reshape((1, t_len) + heads), vp.reshape((1, t_len) + heads),
            ks.reshape((n_seq, 1) + heads), vs.reshape((n_seq, 1) + heads),
            van_s.reshape((n_seq, 1) + heads))
```

```python
import functools

import jax
import jax.numpy as jnp
from jax import lax
from jax.experimental import pallas as pl
from jax.experimental.pallas import tpu as pltpu

D_MODEL = 2048
HEAD_DIM = 128
N_HEADS = 8
WIDTH = N_HEADS * HEAD_DIM
CHUNK = 128
MOBA_BLOCK = 256
MOBA_TOPK = 3
PAGE_SIZE = 128
PAGES_PER_BLOCK = MOBA_BLOCK // PAGE_SIZE
D_FF = 5632
LN_EPS = 1e-5
DEPTH = 1
DN_ALPHA = (2 * DEPTH) ** 0.25
ATTN_SCALE = HEAD_DIM ** -0.5

MASKED = -1e30
MASKED_BF16 = -(2.0 ** 100)

LANES = 128
VMEM_LIMIT = 56 * 1024 * 1024

BF16 = jnp.bfloat16
F32 = jnp.float32


def _layer_norm(x, g, b):
    mu = jnp.mean(x, axis=-1, keepdims=True)
    xc = x - mu
    var = jnp.mean(xc * xc, axis=-1, keepdims=True)
    return xc * lax.rsqrt(var + LN_EPS) * g + b


def _dot(a, b):
    return jnp.dot(a, b, preferred_element_type=F32)


def _dot_nt(a, b):
    return lax.dot_general(a, b, (((1,), (1,)), ((), ())), preferred_element_type=F32)


def _ffn_ln_body(x_ref, wg_ref, wu_ref, wd_ref, g_ref, b_ref, o_ref, xb_ref, acc_ref):
    j = pl.program_id(1)

    @pl.when(j == 0)
    def _():
        xb_ref[...] = x_ref[...].astype(BF16)
        acc_ref[...] = jnp.zeros_like(acc_ref)

    xb = xb_ref[...]
    gate = _dot(xb, wg_ref[...])
    up = _dot(xb, wu_ref[...])
    act = gate * jax.nn.sigmoid(gate) * up
    acc_ref[...] += _dot(act.astype(BF16), wd_ref[...])

    @pl.when(j == pl.num_programs(1) - 1)
    def _():
        y = DN_ALPHA * x_ref[...] + 0.5 * acc_ref[...]
        o_ref[...] = _layer_norm(y, g_ref[...], b_ref[...])


def _ffn_ln(x, wg, wu, wd, g, b, *, tm, tf, name):
    rows = x.shape[0]
    return pl.pallas_call(
        _ffn_ln_body,
        grid=(rows // tm, D_FF // tf),
        in_specs=[
            pl.BlockSpec((tm, D_MODEL), lambda i, j: (i, 0)),
            pl.BlockSpec((D_MODEL, tf), lambda i, j: (0, j)),
            pl.BlockSpec((D_MODEL, tf), lambda i, j: (0, j)),
            pl.BlockSpec((tf, D_MODEL), lambda i, j: (j, 0)),
            pl.BlockSpec((1, D_MODEL), lambda i, j: (0, 0)),
            pl.BlockSpec((1, D_MODEL), lambda i, j: (0, 0)),
        ],
        out_specs=pl.BlockSpec((tm, D_MODEL), lambda i, j: (i, 0)),
        out_shape=jax.ShapeDtypeStruct((rows, D_MODEL), F32),
        scratch_shapes=[pltpu.VMEM((tm, D_MODEL), BF16), pltpu.VMEM((tm, D_MODEL), F32)],
        compiler_params=pltpu.CompilerParams(
            dimension_semantics=("parallel", "arbitrary"), vmem_limit_bytes=VMEM_LIMIT),
        name=name,
    )(x, wg, wu, wd, g, b)


def _head_layer_norm(va, g, b):
    parts = []
    for h in range(N_HEADS):
        sl = slice(h * HEAD_DIM, (h + 1) * HEAD_DIM)
        parts.append(_layer_norm(va[:, sl], g[:, sl], b[:, sl]))
    return jnp.concatenate(parts, axis=-1)


def _proj_prompt_body(x_ref, w_ref, ag_ref, ab_ref, ws_ref, bs_ref,
                      q_ref, k_ref, v_ref, gm_ref, xb_ref, u_ref):
    j = pl.program_id(1)

    @pl.when(j == 0)
    def _():
        xb_ref[...] = x_ref[...].astype(BF16)

    p = _dot(xb_ref[...], w_ref[...])

    @pl.when(j == 0)
    def _():
        q_ref[...] = p.astype(BF16)

    @pl.when(j == 1)
    def _():
        k_ref[...] = p

    @pl.when(j == 2)
    def _():
        v_ref[...] = p

    @pl.when(j == 3)
    def _():
        u_ref[...] = p

    @pl.when(j == 4)
    def _():
        van = _head_layer_norm(p, ag_ref[...], ab_ref[...]).astype(BF16)
        row = lax.broadcasted_iota(jnp.int32, (CHUNK, CHUNK), 0)
        col = lax.broadcasted_iota(jnp.int32, (CHUNK, CHUNK), 1)
        causal = row >= col
        n_chunks = p.shape[0] // CHUNK
        for h in range(N_HEADS):
            sl = slice(h * HEAD_DIM, (h + 1) * HEAD_DIM)
            w_h = jnp.where(causal, ws_ref[h], 0.0).astype(BF16)
            for c in range(n_chunks):
                rs = slice(c * CHUNK, (c + 1) * CHUNK)
                mixed = _dot(w_h, van[rs, sl]) + bs_ref[:, sl]
                gm_ref[rs, sl] = (u_ref[rs, sl] * mixed).astype(BF16)


def _proj_prompt(x, w_in, ag, ab, ws, bs_exp, *, tm):
    rows = x.shape[0]
    row_blk = lambda i, j: (i, 0)
    const2 = lambda i, j: (0, 0)
    return pl.pallas_call(
        _proj_prompt_body,
        grid=(rows // tm, 5),
        in_specs=[
            pl.BlockSpec((tm, D_MODEL), row_blk),
            pl.BlockSpec((D_MODEL, WIDTH), lambda i, j: (0, j)),
            pl.BlockSpec((1, WIDTH), const2),
            pl.BlockSpec((1, WIDTH), const2),
            pl.BlockSpec((N_HEADS, CHUNK, CHUNK), lambda i, j: (0, 0, 0)),
            pl.BlockSpec((CHUNK, WIDTH), const2),
        ],
        out_specs=[
            pl.BlockSpec((tm, WIDTH), row_blk),
            pl.BlockSpec((tm, WIDTH), row_blk),
            pl.BlockSpec((tm, WIDTH), row_blk),
            pl.BlockSpec((tm, WIDTH), row_blk),
        ],
        out_shape=[
            jax.ShapeDtypeStruct((rows, WIDTH), BF16),
            jax.ShapeDtypeStruct((rows, WIDTH), F32),
            jax.ShapeDtypeStruct((rows, WIDTH), F32),
            jax.ShapeDtypeStruct((rows, WIDTH), BF16),
        ],
        scratch_shapes=[pltpu.VMEM((tm, D_MODEL), BF16), pltpu.VMEM((tm, WIDTH), F32)],
        compiler_params=pltpu.CompilerParams(
            dimension_semantics=("parallel", "arbitrary"), vmem_limit_bytes=VMEM_LIMIT),
        name="proj_prompt",
    )(x, w_in, ag, ab, ws, bs_exp)


def _proj_sample_body(x_ref, w_ref, ag_ref, ab_ref, w00_ref, b0_ref,
                      q_ref, k_ref, v_ref, gm_ref, van_ref, xb_ref, u_ref):
    j = pl.program_id(0)

    @pl.when(j == 0)
    def _():
        xb_ref[...] = x_ref[...].astype(BF16)

    p = _dot(xb_ref[...], w_ref[...])

    @pl.when(j == 0)
    def _():
        q_ref[...] = p

    @pl.when(j == 1)
    def _():
        k_ref[...] = p

    @pl.when(j == 2)
    def _():
        v_ref[...] = p

    @pl.when(j == 3)
    def _():
        u_ref[...] = p

    @pl.when(j == 4)
    def _():
        van = _head_layer_norm(p, ag_ref[...], ab_ref[...])
        van_ref[...] = van
        gm_ref[...] = (u_ref[...] * (w00_ref[...] * van + b0_ref[...])).astype(BF16)


def _proj_sample(x, w_in, ag, ab, w00, b0):
    rows = x.shape[0]
    whole = lambda j: (0, 0)
    return pl.pallas_call(
        _proj_sample_body,
        grid=(5,),
        in_specs=[
            pl.BlockSpec((rows, D_MODEL), whole),
            pl.BlockSpec((D_MODEL, WIDTH), lambda j: (0, j)),
            pl.BlockSpec((1, WIDTH), whole),
            pl.BlockSpec((1, WIDTH), whole),
            pl.BlockSpec((1, WIDTH), whole),
            pl.BlockSpec((1, WIDTH), whole),
        ],
        out_specs=[pl.BlockSpec((rows, WIDTH), whole)] * 5,
        out_shape=[
            jax.ShapeDtypeStruct((rows, WIDTH), F32),
            jax.ShapeDtypeStruct((rows, WIDTH), F32),
            jax.ShapeDtypeStruct((rows, WIDTH), F32),
            jax.ShapeDtypeStruct((rows, WIDTH), BF16),
            jax.ShapeDtypeStruct((rows, WIDTH), F32),
        ],
        scratch_shapes=[pltpu.VMEM((rows, D_MODEL), BF16), pltpu.VMEM((rows, WIDTH), F32)],
        compiler_params=pltpu.CompilerParams(
            dimension_semantics=("arbitrary",), vmem_limit_bytes=VMEM_LIMIT),
        name="proj_sample",
    )(x, w_in, ag, ab, w00, b0)


def _out_ln_body(a_ref, m_ref, h_ref, wa_ref, wm_ref, g_ref, b_ref, o_ref):
    mix = _dot(a_ref[...].astype(BF16), wa_ref[...]) + _dot(m_ref[...], wm_ref[...])
    o_ref[...] = _layer_norm(DN_ALPHA * h_ref[...] + mix, g_ref[...], b_ref[...])


def _out_ln(attn, gm, h, w_attn, w_gm, g, b, *, tm, name):
    rows = h.shape[0]
    row_blk = lambda i: (i, 0)
    const = lambda i: (0, 0)
    return pl.pallas_call(
        _out_ln_body,
        grid=(rows // tm,),
        in_specs=[
            pl.BlockSpec((tm, WIDTH), row_blk),
            pl.BlockSpec((tm, WIDTH), row_blk),
            pl.BlockSpec((tm, D_MODEL), row_blk),
            pl.BlockSpec((WIDTH, D_MODEL), const),
            pl.BlockSpec((WIDTH, D_MODEL), const),
            pl.BlockSpec((1, D_MODEL), const),
            pl.BlockSpec((1, D_MODEL), const),
        ],
        out_specs=pl.BlockSpec((tm, D_MODEL), row_blk),
        out_shape=jax.ShapeDtypeStruct((rows, D_MODEL), F32),
        compiler_params=pltpu.CompilerParams(
            dimension_semantics=("parallel",), vmem_limit_bytes=VMEM_LIMIT),
        name=name,
    )(attn, gm, h, w_attn, w_gm, g, b)


LOG2E = 1.4426950408889634
CHUNK_BLOCKS = 4
N_SEL_ROWS = 32


def _top_blocks_t(gate, n_valid):
    blk = lax.broadcasted_iota(jnp.int32, gate.shape, 0).astype(F32)
    valid = blk < n_valid.astype(F32)
    g = jnp.where(valid, gate, -jnp.inf)
    sel = jnp.zeros(gate.shape, dtype=jnp.bool_)
    for _ in range(MOBA_TOPK):
        mx = jnp.max(g, axis=0, keepdims=True)
        first = jnp.min(jnp.where(g == mx, blk, float(gate.shape[0])), axis=0, keepdims=True)
        hit = blk == first
        sel = sel | (hit & valid)
        g = jnp.where(hit, -jnp.inf, g)
    return sel


def _attn_prompt_body(slope_ref, q_ref, k_ref, v_ref, o_ref,
                      ka_ref, vt_ref, km_ref, toep_ref, m_ref, l_ref, acc_ref):
    h = pl.program_id(0)
    qb = pl.program_id(1)
    n_blocks = k_ref.shape[0] // MOBA_BLOCK
    chunk = CHUNK_BLOCKS * MOBA_BLOCK
    slope2 = slope_ref[h] * LOG2E

    @pl.when(qb == 0)
    def _():
        km_ref[...] = jnp.zeros_like(km_ref)
        lane = lax.broadcasted_iota(jnp.int32, (MOBA_BLOCK, LANES), 1)

        def prep(n, carry):
            rows = pl.ds(pl.multiple_of(n * MOBA_BLOCK, MOBA_BLOCK), MOBA_BLOCK)
            kblk = k_ref[rows, :]
            ka_ref[rows, :HEAD_DIM] = kblk.astype(BF16)
            ka_ref[rows, HEAD_DIM:] = jnp.where(lane == n, 1.0, 0.0).astype(BF16)
            vt_ref[n] = v_ref[rows, :].T.astype(BF16)
            km_ref[pl.ds(n, 1), :] = jnp.mean(kblk, axis=0, keepdims=True)
            return carry

        lax.fori_loop(0, n_blocks, prep, 0)
        key = lax.broadcasted_iota(jnp.int32, (chunk, MOBA_BLOCK), 0)
        qry = lax.broadcasted_iota(jnp.int32, (chunk, MOBA_BLOCK), 1)
        toep_ref[...] = slope2 * (qry - key).astype(F32)

    qt = q_ref[...].astype(F32).T.astype(BF16)
    gate = _dot(km_ref[...].astype(BF16), qt)
    sel = _top_blocks_t(gate, qb)
    blk = lax.broadcasted_iota(jnp.int32, gate.shape, 0)
    pad = jnp.zeros((HEAD_DIM - N_SEL_ROWS, MOBA_BLOCK), BF16)
    qa_past = jnp.concatenate([qt, jnp.where(sel, 0.0, MASKED_BF16).astype(BF16), pad], axis=0)
    qa_own = jnp.concatenate([qt, jnp.where(blk == qb, 0.0, MASKED_BF16).astype(BF16), pad], axis=0)
    c1 = ATTN_SCALE * LOG2E

    rows = pl.ds(pl.multiple_of(qb * MOBA_BLOCK, MOBA_BLOCK), MOBA_BLOCK)
    u = _dot(ka_ref[rows, :], qa_own) * c1 - toep_ref[:MOBA_BLOCK, :]
    key = lax.broadcasted_iota(jnp.int32, u.shape, 0)
    qry = lax.broadcasted_iota(jnp.int32, u.shape, 1)
    u = jnp.where(qry >= key, u, MASKED)
    m0 = jnp.max(u, axis=0, keepdims=True)
    p = jnp.exp2(u - m0)
    m_ref[...] = m0
    l_ref[...] = jnp.sum(p, axis=0, keepdims=True)
    acc_ref[...] = _dot(vt_ref[qb], p.astype(BF16))

    def past(c, carry):
        rows = pl.ds(pl.multiple_of(c * chunk, chunk), chunk)
        u = _dot(ka_ref[rows, :], qa_past) * c1 - toep_ref[...]
        off = slope2 * ((qb - c * CHUNK_BLOCKS) * MOBA_BLOCK).astype(F32)
        m_old = m_ref[...]
        m_new = jnp.maximum(m_old, jnp.max(u, axis=0, keepdims=True) - off)
        alpha = jnp.exp2(m_old - m_new)
        p = jnp.exp2(u - (m_new + off))
        l_ref[...] = alpha * l_ref[...] + jnp.sum(p, axis=0, keepdims=True)
        pv = jnp.zeros(acc_ref.shape, F32)
        for n in range(CHUNK_BLOCKS):
            pn = p[n * MOBA_BLOCK:(n + 1) * MOBA_BLOCK].astype(BF16)
            pv = pv + _dot(vt_ref[c * CHUNK_BLOCKS + n], pn)
        acc_ref[...] = alpha * acc_ref[...] + pv
        m_ref[...] = m_new
        return carry

    lax.fori_loop(0, (qb + CHUNK_BLOCKS - 1) // CHUNK_BLOCKS, past, 0)
    o_ref[...] = (acc_ref[...] / l_ref[...]).T.astype(o_ref.dtype)


def _attn_prompt(slopes, q, k, v):
    t_len = q.shape[0]
    n_qb = t_len // MOBA_BLOCK
    grid_spec = pltpu.PrefetchScalarGridSpec(
        num_scalar_prefetch=1,
        grid=(N_HEADS, n_qb),
        in_specs=[
            pl.BlockSpec((MOBA_BLOCK, HEAD_DIM), lambda h, i, s: (i, h)),
            pl.BlockSpec((t_len, HEAD_DIM), lambda h, i, s: (0, h)),
            pl.BlockSpec((t_len, HEAD_DIM), lambda h, i, s: (0, h)),
        ],
        out_specs=pl.BlockSpec((MOBA_BLOCK, HEAD_DIM), lambda h, i, s: (i, h)),
        scratch_shapes=[
            pltpu.VMEM((t_len, 2 * HEAD_DIM), BF16),
            pltpu.VMEM((n_qb, HEAD_DIM, MOBA_BLOCK), BF16),
            pltpu.VMEM((N_SEL_ROWS, HEAD_DIM), F32),
            pltpu.VMEM((CHUNK_BLOCKS * MOBA_BLOCK, MOBA_BLOCK), F32),
            pltpu.VMEM((1, MOBA_BLOCK), F32),
            pltpu.VMEM((1, MOBA_BLOCK), F32),
            pltpu.VMEM((HEAD_DIM, MOBA_BLOCK), F32),
        ],
    )
    assert n_qb <= N_SEL_ROWS and n_qb % CHUNK_BLOCKS == 0
    return pl.pallas_call(
        _attn_prompt_body,
        grid_spec=grid_spec,
        out_shape=jax.ShapeDtypeStruct((t_len, WIDTH), BF16),
        compiler_params=pltpu.CompilerParams(
            dimension_semantics=("parallel", "arbitrary"), vmem_limit_bytes=VMEM_LIMIT),
        name="attn_prompt",
    )(slopes, q, k, v)


GROUP_PAGES = 8
GROUP_BLOCKS = GROUP_PAGES // PAGES_PER_BLOCK


def _select_sample_body(pt_ref, q_ref, ck_ref, sel_ref, buf_ref, sem_ref, ksum_ref):
    n_seq, n_pages = pt_ref.shape
    groups_per_seq = n_pages // GROUP_PAGES
    n_steps = n_seq * groups_per_seq
    n_full = n_pages // PAGES_PER_BLOCK

    def page_copy(page, slot, p):
        return pltpu.make_async_copy(
            ck_ref.at[page, :, 0], buf_ref.at[slot, pl.ds(p * PAGE_SIZE, PAGE_SIZE)], sem_ref.at[slot])

    def start(t, slot):
        b = t // groups_per_seq
        g = t % groups_per_seq
        for p in range(GROUP_PAGES):
            page_copy(pt_ref[b, g * GROUP_PAGES + p], slot, p).start()

    def wait(slot):
        for p in range(GROUP_PAGES):
            page_copy(0, slot, p).wait()

    start(0, 0)

    def step(t, carry):
        slot = t % 2
        b = t // groups_per_seq
        g = t % groups_per_seq

        @pl.when(t + 1 < n_steps)
        def _():
            start(t + 1, 1 - slot)

        wait(slot)
        for n in range(GROUP_BLOCKS):
            blk = buf_ref[slot, pl.ds(n * MOBA_BLOCK, MOBA_BLOCK)]
            ksum_ref[g * GROUP_BLOCKS + n] = jnp.sum(blk, axis=0)

        @pl.when(g == groups_per_seq - 1)
        def _():
            kmean = ksum_ref[...] / MOBA_BLOCK
            gate = jnp.sum(kmean * q_ref[b], axis=-1, keepdims=True)
            blk_id = lax.broadcasted_iota(jnp.int32, gate.shape, 0).astype(F32)
            lane = lax.broadcasted_iota(jnp.int32, (N_HEADS, LANES), 1)
            picks = jnp.zeros((N_HEADS, LANES), F32)
            for j in range(MOBA_TOPK):
                mx = jnp.max(gate, axis=0, keepdims=True)
                first = jnp.min(jnp.where(gate == mx, blk_id, float(n_full)), axis=0, keepdims=True)
                picks = jnp.where(lane == j, first[0], picks)
                gate = jnp.where(blk_id == first, -jnp.inf, gate)
            sel_ref[b] = picks.astype(jnp.int32)
        return carry

    lax.fori_loop(0, n_steps, step, 0)


def _select_sample(page_table, q, cache_k):
    n_seq, n_pages = page_table.shape
    assert n_pages % GROUP_PAGES == 0
    n_full = n_pages // PAGES_PER_BLOCK
    grid_spec = pltpu.PrefetchScalarGridSpec(
        num_scalar_prefetch=1,
        grid=(1,),
        in_specs=[
            pl.BlockSpec((n_seq, N_HEADS, HEAD_DIM), lambda i, pt: (0, 0, 0)),
            pl.BlockSpec(memory_space=pl.ANY),
        ],
        out_specs=pl.BlockSpec((n_seq, N_HEADS, LANES), lambda i, pt: (0, 0, 0)),
        scratch_shapes=[
            pltpu.VMEM((2, GROUP_PAGES * PAGE_SIZE, N_HEADS, HEAD_DIM), F32),
            pltpu.SemaphoreType.DMA((2,)),
            pltpu.VMEM((n_full, N_HEADS, HEAD_DIM), F32),
        ],
    )
    return pl.pallas_call(
        _select_sample_body,
        grid_spec=grid_spec,
        out_shape=jax.ShapeDtypeStruct((n_seq, N_HEADS, LANES), jnp.int32),
        compiler_params=pltpu.CompilerParams(
            dimension_semantics=("arbitrary",), vmem_limit_bytes=VMEM_LIMIT),
        name="select_sample",
    )(page_table, q.reshape(n_seq, N_HEADS, HEAD_DIM), cache_k)


def _attn_sample_body(pt_ref, sel_ref, slope_ref, q_ref, kn_ref, vn_ref, ck_ref, cv_ref, o_ref,
                      kbuf_ref, vbuf_ref, sem_ref, *, past_len):
    b = pl.program_id(0)
    n_sel = MOBA_TOPK

    def copies(h, j, pp):
        page = pt_ref[b, sel_ref[b, h * MOBA_TOPK + j] * PAGES_PER_BLOCK + pp]
        rows = pl.ds((j * PAGES_PER_BLOCK + pp) * PAGE_SIZE, PAGE_SIZE)
        return (pltpu.make_async_copy(ck_ref.at[page, :, 0, h, :], kbuf_ref.at[h, rows, :], sem_ref.at[0]),
                pltpu.make_async_copy(cv_ref.at[page, :, 0, h, :], vbuf_ref.at[h, rows, :], sem_ref.at[1]))

    gathers = [copies(h, j, pp) for h in range(N_HEADS) for j in range(n_sel)
               for pp in range(PAGES_PER_BLOCK)]
    for ck, cv in gathers:
        ck.start()
        cv.start()
    for ck, cv in gathers:
        ck.wait()
        cv.wait()

    n_keys = n_sel * MOBA_BLOCK
    lane = lax.broadcasted_iota(jnp.int32, (1, n_keys), 1)
    slot = lane // MOBA_BLOCK
    offs = lane % MOBA_BLOCK
    outs = []
    for h in range(N_HEADS):
        sl = slice(h * HEAD_DIM, (h + 1) * HEAD_DIM)
        slope = slope_ref[h]
        qh = q_ref[:, sl]
        kn = kn_ref[:, sl]
        vn = vn_ref[:, sl]
        q8 = jnp.broadcast_to(qh, (8, HEAD_DIM)).astype(BF16)
        s = _dot_nt(q8, kbuf_ref[h].astype(BF16))[:1] * ATTN_SCALE
        blk = jnp.zeros((1, n_keys), jnp.int32)
        for j in range(n_sel):
            blk = jnp.where(slot == j, sel_ref[b, h * MOBA_TOPK + j], blk)
        dist = (past_len - (blk * MOBA_BLOCK + offs)).astype(F32)
        s = s - slope * dist
        s_own = jnp.sum(qh * kn, axis=-1, keepdims=True) * ATTN_SCALE
        mx = jnp.maximum(jnp.max(s, axis=-1, keepdims=True), s_own)
        p = jnp.exp(s - mx)
        p_own = jnp.exp(s_own - mx)
        denom = jnp.sum(p, axis=-1, keepdims=True) + p_own
        p8 = jnp.broadcast_to(p, (8, n_keys)).astype(BF16)
        pv = _dot(p8, vbuf_ref[h].astype(BF16))[:1]
        outs.append((pv + p_own * vn) / denom)
    o_ref[...] = jnp.concatenate(outs, axis=-1)


def _attn_sample(page_table, sel, slopes, q, k_new, v_new, cache_k, cache_v):
    n_seq, n_pages = page_table.shape
    per_seq = pl.BlockSpec((None, 1, WIDTH), lambda b, *_: (b, 0, 0))
    grid_spec = pltpu.PrefetchScalarGridSpec(
        num_scalar_prefetch=3,
        grid=(n_seq,),
        in_specs=[
            per_seq, per_seq, per_seq,
            pl.BlockSpec(memory_space=pl.ANY),
            pl.BlockSpec(memory_space=pl.ANY),
        ],
        out_specs=per_seq,
        scratch_shapes=[
            pltpu.VMEM((N_HEADS, MOBA_TOPK * MOBA_BLOCK, HEAD_DIM), F32),
            pltpu.VMEM((N_HEADS, MOBA_TOPK * MOBA_BLOCK, HEAD_DIM), F32),
            pltpu.SemaphoreType.DMA((2,)),
        ],
    )
    return pl.pallas_call(
        functools.partial(_attn_sample_body, past_len=n_pages * PAGE_SIZE),
        grid_spec=grid_spec,
        out_shape=jax.ShapeDtypeStruct((n_seq, 1, WIDTH), F32),
        compiler_params=pltpu.CompilerParams(
            dimension_semantics=("arbitrary",), vmem_limit_bytes=VMEM_LIMIT),
        name="attn_sample",
    )(page_table, sel, slopes, q.reshape(n_seq, 1, WIDTH), k_new.reshape(n_seq, 1, WIDTH),
      v_new.reshape(n_seq, 1, WIDTH), cache_k, cache_v).reshape(n_seq, WIDTH)


def kernel(x_prompt, x_sample, cache_k, cache_v, page_table, ffn1_w_gate, ffn1_w_up, ffn1_w_down, ln1_g, ln1_b, w_in, a_norm_g, a_norm_b, a_spatial_w, a_spatial_b, w_out, ln2_g, ln2_b, ffn2_w_gate, ffn2_w_up, ffn2_w_down, ln3_g, ln3_b):
    assert x_prompt.shape[0] == 1 and x_sample.shape[1] == 1 and w_in.shape[0] == DEPTH == 1
    t_len = x_prompt.shape[1]
    n_seq = x_sample.shape[0]
    n_pool = cache_k.shape[0]
    xp = x_prompt.reshape(t_len, D_MODEL)
    xs = x_sample.reshape(n_seq, D_MODEL)
    slopes = jnp.exp2(-8.0 * jnp.arange(1, N_HEADS + 1, dtype=F32) / N_HEADS)

    bf = lambda w: w[0].astype(BF16)
    row = lambda v: v.reshape(1, -1)
    w1g, w1u, w1d = bf(ffn1_w_gate), bf(ffn1_w_up), bf(ffn1_w_down)
    w2g, w2u, w2d = bf(ffn2_w_gate), bf(ffn2_w_up), bf(ffn2_w_down)
    w_in_b, w_out_b = bf(w_in), bf(w_out)
    w_out_attn, w_out_gm = w_out_b[:WIDTH], w_out_b[WIDTH:]
    ag, ab = row(a_norm_g[0]), row(a_norm_b[0])
    ws, bs = a_spatial_w[0], a_spatial_b[0]
    bs_exp = jnp.repeat(bs.T, HEAD_DIM, axis=1)
    w00 = row(jnp.repeat(ws[:, 0, 0], HEAD_DIM))
    b0 = row(jnp.repeat(bs[:, 0], HEAD_DIM))

    hp = _ffn_ln(xp, w1g, w1u, w1d, row(ln1_g[0]), row(ln1_b[0]), tm=512, tf=512, name="ffn1_prompt")
    qp, kp, vp, gm_p = _proj_prompt(hp, w_in_b, ag, ab, ws, bs_exp, tm=512)
    attn_p = _attn_prompt(slopes, qp, kp, vp)
    hp = _out_ln(attn_p, gm_p, hp, w_out_attn, w_out_gm, row(ln2_g[0]), row(ln2_b[0]), tm=512, name="out_prompt")
    yp = _ffn_ln(hp, w2g, w2u, w2d, row(ln3_g[0]), row(ln3_b[0]), tm=512, tf=512, name="ffn2_prompt")

    hs = _ffn_ln(xs, w1g, w1u, w1d, row(ln1_g[0]), row(ln1_b[0]), tm=n_seq, tf=512, name="ffn1_sample")
    qs, ks, vs, gm_s, van_s = _proj_sample(hs, w_in_b, ag, ab, w00, b0)
    sel = _select_sample(page_table, qs, cache_k)
    sel = sel[:, :, :MOBA_TOPK].reshape(n_seq, N_HEADS * MOBA_TOPK)
    attn_s = _attn_sample(page_table, sel, slopes, qs, ks, vs, cache_k, cache_v)
    hs = _out_ln(attn_s, gm_s, hs, w_out_attn, w_out_gm, row(ln2_g[0]), row(ln2_b[0]), tm=n_seq, name="out_sample")
    ys = _ffn_ln(hs, w2g, w2u, w2d, row(ln3_g[0]), row(ln3_b[0]), tm=n_seq, tf=512, name="ffn2_sample")

    heads = (1, N_HEADS, HEAD_DIM)
    return (yp.reshape(1, t_len, D_MODEL), ys.reshape(n_seq, 1, D_MODEL),
            kp.reshape((1, t_len) + heads), vp.reshape((1, t_len) + heads),
            ks.reshape((n_seq, 1) + heads), vs.reshape((n_seq, 1) + heads),
            van_s.reshape((n_seq, 1) + heads))
```

```python
import functools

import jax
import jax.numpy as jnp
from jax import lax
from jax.experimental import pallas as pl
from jax.experimental.pallas import tpu as pltpu

D_MODEL = 2048
HEAD_DIM = 128
N_HEADS = 8
WIDTH = N_HEADS * HEAD_DIM
CHUNK = 128
MOBA_BLOCK = 256
MOBA_TOPK = 3
PAGE_SIZE = 128
PAGES_PER_BLOCK = MOBA_BLOCK // PAGE_SIZE
D_FF = 5632
LN_EPS = 1e-5
DEPTH = 1
DN_ALPHA = (2 * DEPTH) ** 0.25
ATTN_SCALE = HEAD_DIM ** -0.5
LOG2E = 1.4426950408889634

MASKED = -1e30
MASKED_BF16 = -(2.0 ** 100)

LANES = 128
VMEM_LIMIT = 56 * 1024 * 1024

BF16 = jnp.bfloat16
F32 = jnp.float32


def _layer_norm(x, g, b):
    mu = jnp.mean(x, axis=-1, keepdims=True)
    xc = x - mu
    var = jnp.mean(xc * xc, axis=-1, keepdims=True)
    return xc * lax.rsqrt(var + LN_EPS) * g + b


def _dot(a, b):
    return jnp.dot(a, b, preferred_element_type=F32)


def _dot_nt(a, b):
    return lax.dot_general(a, b, (((1,), (1,)), ((), ())), preferred_element_type=F32)


def _ffn_ln_body(x_ref, wg_ref, wu_ref, wd_ref, g_ref, b_ref, o_ref, xb_ref, acc_ref):
    j = pl.program_id(1)

    @pl.when(j == 0)
    def _():
        xb_ref[...] = x_ref[...].astype(BF16)
        acc_ref[...] = jnp.zeros_like(acc_ref)

    xb = xb_ref[...]
    gate = _dot(xb, wg_ref[...])
    up = _dot(xb, wu_ref[...])
    act = gate * jax.nn.sigmoid(gate) * up
    acc_ref[...] += _dot(act.astype(BF16), wd_ref[...])

    @pl.when(j == pl.num_programs(1) - 1)
    def _():
        y = DN_ALPHA * x_ref[...] + 0.5 * acc_ref[...]
        o_ref[...] = _layer_norm(y, g_ref[...], b_ref[...])


def _ffn_ln(x, wg, wu, wd, g, b, *, tm, tf, name):
    rows = x.shape[0]
    return pl.pallas_call(
        _ffn_ln_body,
        grid=(rows // tm, D_FF // tf),
        in_specs=[
            pl.BlockSpec((tm, D_MODEL), lambda i, j: (i, 0)),
            pl.BlockSpec((D_MODEL, tf), lambda i, j: (0, j)),
            pl.BlockSpec((D_MODEL, tf), lambda i, j: (0, j)),
            pl.BlockSpec((tf, D_MODEL), lambda i, j: (j, 0)),
            pl.BlockSpec((1, D_MODEL), lambda i, j: (0, 0)),
            pl.BlockSpec((1, D_MODEL), lambda i, j: (0, 0)),
        ],
        out_specs=pl.BlockSpec((tm, D_MODEL), lambda i, j: (i, 0)),
        out_shape=jax.ShapeDtypeStruct((rows, D_MODEL), F32),
        scratch_shapes=[pltpu.VMEM((tm, D_MODEL), BF16), pltpu.VMEM((tm, D_MODEL), F32)],
        compiler_params=pltpu.CompilerParams(
            dimension_semantics=("parallel", "arbitrary"), vmem_limit_bytes=VMEM_LIMIT),
        name=name,
    )(x, wg, wu, wd, g, b)


def _split(x):
    hi = x.astype(BF16)
    return hi, (x - hi.astype(F32)).astype(BF16)


def _dot_split(x, w):
    x_hi, x_lo = _split(x)
    w_hi, w_lo = _split(w)
    n = x.shape[0]
    both = _dot(jnp.concatenate([x_hi, x_lo], axis=0), w_hi)
    return both[:n] + (both[n:] + _dot(x_hi, w_lo)), w_hi


def _dot_cast(x, w):
    w_hi = w.astype(BF16)
    return _dot(x.astype(BF16), w_hi), w_hi


def _ffn_ln_sample_body(x_ref, wg_ref, wu_ref, wd_ref, g_ref, b_ref,
                        o_ref, wgb_ref, wub_ref, wdb_ref, acc_ref, *, accurate):
    j = pl.program_id(0)
    dot = _dot_split if accurate else _dot_cast

    @pl.when(j == 0)
    def _():
        acc_ref[...] = jnp.zeros_like(acc_ref)

    x = x_ref[...]
    gate, wgb_ref[...] = dot(x, wg_ref[...])
    up, wub_ref[...] = dot(x, wu_ref[...])
    down, wdb_ref[...] = dot(gate * jax.nn.sigmoid(gate) * up, wd_ref[...])
    acc_ref[...] += down

    @pl.when(j == pl.num_programs(0) - 1)
    def _():
        y = DN_ALPHA * x + 0.5 * acc_ref[...]
        o_ref[...] = _layer_norm(y, g_ref[...], b_ref[...])


def _ffn_ln_sample(x, wg, wu, wd, g, b, *, tf, accurate, name):
    rows = x.shape[0]
    whole = lambda j: (0, 0)
    cols = pl.BlockSpec((D_MODEL, tf), lambda j: (0, j))
    rws = pl.BlockSpec((tf, D_MODEL), lambda j: (j, 0))
    return pl.pallas_call(
        functools.partial(_ffn_ln_sample_body, accurate=accurate),
        grid=(D_FF // tf,),
        in_specs=[pl.BlockSpec((rows, D_MODEL), whole), cols, cols, rws,
                  pl.BlockSpec((1, D_MODEL), whole), pl.BlockSpec((1, D_MODEL), whole)],
        out_specs=[pl.BlockSpec((rows, D_MODEL), whole), cols, cols, rws],
        out_shape=[jax.ShapeDtypeStruct((rows, D_MODEL), F32),
                   jax.ShapeDtypeStruct(wg.shape, BF16),
                   jax.ShapeDtypeStruct(wu.shape, BF16),
                   jax.ShapeDtypeStruct(wd.shape, BF16)],
        scratch_shapes=[pltpu.VMEM((rows, D_MODEL), F32)],
        compiler_params=pltpu.CompilerParams(
            dimension_semantics=("arbitrary",), vmem_limit_bytes=VMEM_LIMIT),
        name=name,
    )(x, wg, wu, wd, g, b)


def _head_layer_norm(va, g, b):
    parts = []
    for h in range(N_HEADS):
        sl = slice(h * HEAD_DIM, (h + 1) * HEAD_DIM)
        parts.append(_layer_norm(va[:, sl], g[:, sl], b[:, sl]))
    return jnp.concatenate(parts, axis=-1)


def _proj_prompt_body(x_ref, w_ref, ag_ref, ab_ref, ws_ref, bs_ref,
                      q_ref, k_ref, v_ref, gm_ref, xb_ref, u_ref):
    j = pl.program_id(1)

    @pl.when(j == 0)
    def _():
        xb_ref[...] = x_ref[...].astype(BF16)

    p = _dot(xb_ref[...], w_ref[...])

    @pl.when(j == 0)
    def _():
        q_ref[...] = (p * (ATTN_SCALE * LOG2E)).astype(BF16)

    @pl.when(j == 1)
    def _():
        k_ref[...] = p

    @pl.when(j == 2)
    def _():
        v_ref[...] = p

    @pl.when(j == 3)
    def _():
        u_ref[...] = p

    @pl.when(j == 4)
    def _():
        van = _head_layer_norm(p, ag_ref[...], ab_ref[...]).astype(BF16)
        row = lax.broadcasted_iota(jnp.int32, (CHUNK, CHUNK), 0)
        col = lax.broadcasted_iota(jnp.int32, (CHUNK, CHUNK), 1)
        causal = row >= col
        n_chunks = p.shape[0] // CHUNK
        for h in range(N_HEADS):
            sl = slice(h * HEAD_DIM, (h + 1) * HEAD_DIM)
            w_h = jnp.where(causal, ws_ref[h], 0.0).astype(BF16)
            for c in range(n_chunks):
                rs = slice(c * CHUNK, (c + 1) * CHUNK)
                mixed = _dot(w_h, van[rs, sl]) + bs_ref[:, sl]
                gm_ref[rs, sl] = (u_ref[rs, sl] * mixed).astype(BF16)


def _proj_prompt(x, w_in, ag, ab, ws, bs_exp, *, tm):
    rows = x.shape[0]
    row_blk = lambda i, j: (i, 0)
    const2 = lambda i, j: (0, 0)
    return pl.pallas_call(
        _proj_prompt_body,
        grid=(rows // tm, 5),
        in_specs=[
            pl.BlockSpec((tm, D_MODEL), row_blk),
            pl.BlockSpec((D_MODEL, WIDTH), lambda i, j: (0, j)),
            pl.BlockSpec((1, WIDTH), const2),
            pl.BlockSpec((1, WIDTH), const2),
            pl.BlockSpec((N_HEADS, CHUNK, CHUNK), lambda i, j: (0, 0, 0)),
            pl.BlockSpec((CHUNK, WIDTH), const2),
        ],
        out_specs=[
            pl.BlockSpec((tm, WIDTH), row_blk),
            pl.BlockSpec((tm, WIDTH), row_blk),
            pl.BlockSpec((tm, WIDTH), row_blk),
            pl.BlockSpec((tm, WIDTH), row_blk),
        ],
        out_shape=[
            jax.ShapeDtypeStruct((rows, WIDTH), BF16),
            jax.ShapeDtypeStruct((rows, WIDTH), F32),
            jax.ShapeDtypeStruct((rows, WIDTH), F32),
            jax.ShapeDtypeStruct((rows, WIDTH), BF16),
        ],
        scratch_shapes=[pltpu.VMEM((tm, D_MODEL), BF16), pltpu.VMEM((tm, WIDTH), F32)],
        compiler_params=pltpu.CompilerParams(
            dimension_semantics=("parallel", "arbitrary"), vmem_limit_bytes=VMEM_LIMIT),
        name="proj_prompt",
    )(x, w_in, ag, ab, ws, bs_exp)


def _proj_sample_body(x_ref, w_ref, ag_ref, ab_ref, w00_ref, b0_ref,
                      q_ref, k_ref, v_ref, gm_ref, van_ref, wb_ref, u_ref):
    j = pl.program_id(0)
    p, w_hi = _dot_split(x_ref[...], w_ref[...])
    wb_ref[...] = w_hi

    @pl.when(j == 0)
    def _():
        q_ref[...] = p

    @pl.when(j == 1)
    def _():
        k_ref[...] = p

    @pl.when(j == 2)
    def _():
        v_ref[...] = p

    @pl.when(j == 3)
    def _():
        u_ref[...] = p

    @pl.when(j == 4)
    def _():
        van = _head_layer_norm(p, ag_ref[...], ab_ref[...])
        van_ref[...] = van
        gm_ref[...] = u_ref[...] * (w00_ref[...] * van + b0_ref[...])


def _proj_sample(x, w_in, ag, ab, w00, b0):
    rows = x.shape[0]
    whole = lambda j: (0, 0)
    w_cols = lambda j: (0, j)
    return pl.pallas_call(
        _proj_sample_body,
        grid=(5,),
        in_specs=[
            pl.BlockSpec((rows, D_MODEL), whole),
            pl.BlockSpec((D_MODEL, WIDTH), w_cols),
            pl.BlockSpec((1, WIDTH), whole),
            pl.BlockSpec((1, WIDTH), whole),
            pl.BlockSpec((1, WIDTH), whole),
            pl.BlockSpec((1, WIDTH), whole),
        ],
        out_specs=[pl.BlockSpec((rows, WIDTH), whole)] * 5 + [pl.BlockSpec((D_MODEL, WIDTH), w_cols)],
        out_shape=[
            jax.ShapeDtypeStruct((rows, WIDTH), F32),
            jax.ShapeDtypeStruct((rows, WIDTH), F32),
            jax.ShapeDtypeStruct((rows, WIDTH), F32),
            jax.ShapeDtypeStruct((rows, WIDTH), F32),
            jax.ShapeDtypeStruct((rows, WIDTH), F32),
            jax.ShapeDtypeStruct(w_in.shape, BF16),
        ],
        scratch_shapes=[pltpu.VMEM((rows, WIDTH), F32)],
        compiler_params=pltpu.CompilerParams(
            dimension_semantics=("arbitrary",), vmem_limit_bytes=VMEM_LIMIT),
        name="proj_sample",
    )(x, w_in, ag, ab, w00, b0)


def _out_ln_body(a_ref, m_ref, h_ref, wa_ref, wm_ref, g_ref, b_ref, o_ref):
    mix = _dot(a_ref[...], wa_ref[...]) + _dot(m_ref[...], wm_ref[...])
    o_ref[...] = _layer_norm(DN_ALPHA * h_ref[...] + mix, g_ref[...], b_ref[...])


def _out_ln(attn, gm, h, w_out, g, b, *, tm):
    rows = h.shape[0]
    row_blk = lambda i: (i, 0)
    const = lambda i: (0, 0)
    return pl.pallas_call(
        _out_ln_body,
        grid=(rows // tm,),
        in_specs=[
            pl.BlockSpec((tm, WIDTH), row_blk),
            pl.BlockSpec((tm, WIDTH), row_blk),
            pl.BlockSpec((tm, D_MODEL), row_blk),
            pl.BlockSpec((WIDTH, D_MODEL), const),
            pl.BlockSpec((WIDTH, D_MODEL), lambda i: (1, 0)),
            pl.BlockSpec((1, D_MODEL), const),
            pl.BlockSpec((1, D_MODEL), const),
        ],
        out_specs=pl.BlockSpec((tm, D_MODEL), row_blk),
        out_shape=jax.ShapeDtypeStruct((rows, D_MODEL), F32),
        compiler_params=pltpu.CompilerParams(
            dimension_semantics=("parallel",), vmem_limit_bytes=VMEM_LIMIT),
        name="out_prompt",
    )(attn, gm, h, w_out, w_out, g, b)


def _out_ln_sample_body(x_ref, h_ref, w_ref, g_ref, b_ref, o_ref, wb_ref, acc_ref):
    j = pl.program_id(0)

    @pl.when(j == 0)
    def _():
        acc_ref[...] = jnp.zeros_like(acc_ref)

    part, wb_ref[...] = _dot_cast(x_ref[...], w_ref[...])
    acc_ref[...] += part

    @pl.when(j == pl.num_programs(0) - 1)
    def _():
        o_ref[...] = _layer_norm(DN_ALPHA * h_ref[...] + acc_ref[...], g_ref[...], b_ref[...])


def _out_ln_sample(mix, h, w_out, g, b, *, tk):
    rows = h.shape[0]
    whole = lambda j: (0, 0)
    slab = pl.BlockSpec((tk, D_MODEL), lambda j: (j, 0))
    return pl.pallas_call(
        _out_ln_sample_body,
        grid=(w_out.shape[0] // tk,),
        in_specs=[pl.BlockSpec((rows, tk), lambda j: (0, j)), pl.BlockSpec((rows, D_MODEL), whole), slab,
                  pl.BlockSpec((1, D_MODEL), whole), pl.BlockSpec((1, D_MODEL), whole)],
        out_specs=[pl.BlockSpec((rows, D_MODEL), whole), slab],
        out_shape=[jax.ShapeDtypeStruct((rows, D_MODEL), F32), jax.ShapeDtypeStruct(w_out.shape, BF16)],
        scratch_shapes=[pltpu.VMEM((rows, D_MODEL), F32)],
        compiler_params=pltpu.CompilerParams(
            dimension_semantics=("arbitrary",), vmem_limit_bytes=VMEM_LIMIT),
        name="out_sample",
    )(mix, h, w_out, g, b)


CHUNK_BLOCKS = 4
KEY_CHUNK = CHUNK_BLOCKS * MOBA_BLOCK
N_SEL_ROWS = 32
N_POS_ROWS = 16
SPLIT = 3


def _top_blocks_t(gate, n_valid):
    blk = lax.broadcasted_iota(jnp.int32, gate.shape, 0).astype(F32)
    valid = blk < n_valid
    g = jnp.where(valid, gate, -jnp.inf)
    sel = jnp.zeros(gate.shape, dtype=jnp.bool_)
    for _ in range(MOBA_TOPK):
        mx = jnp.max(g, axis=0, keepdims=True)
        first = jnp.min(jnp.where(g == mx, blk, float(gate.shape[0])), axis=0, keepdims=True)
        hit = blk == first
        sel = sel | (hit & valid)
        g = jnp.where(hit, -jnp.inf, g)
    return sel


def _attn_prompt_body(slope_ref, q_ref, k_ref, v_ref, o_ref,
                      ka_ref, vt_ref, km_ref, pos_ref, qa_ref, m_ref, l_ref, acc_ref,
                      u_ref, p_ref, mx_ref, alpha_ref):
    h = pl.program_id(0)
    i = pl.program_id(1)
    n_blocks = k_ref.shape[0] // MOBA_BLOCK
    slope2 = slope_ref[h] * LOG2E

    @pl.when(i == 0)
    def _():
        km_ref[...] = jnp.zeros_like(km_ref)
        lane = lax.broadcasted_iota(jnp.int32, (MOBA_BLOCK, LANES), 1)
        offs = lax.broadcasted_iota(jnp.int32, (MOBA_BLOCK, LANES), 0).astype(F32)

        def prep(n, carry):
            rows = pl.ds(pl.multiple_of(n * MOBA_BLOCK, MOBA_BLOCK), MOBA_BLOCK)
            kblk = k_ref[rows, :]
            in_chunk = lax.convert_element_type(n % CHUNK_BLOCKS, F32)
            aug = jnp.where(lane == n, 1.0, 0.0)
            aug = jnp.where((lane >= N_SEL_ROWS) & (lane < N_SEL_ROWS + SPLIT), in_chunk, aug)
            aug = jnp.where((lane >= N_SEL_ROWS + SPLIT) & (lane < N_SEL_ROWS + 2 * SPLIT), offs, aug)
            ka_ref[rows, :HEAD_DIM] = kblk.astype(BF16)
            ka_ref[rows, HEAD_DIM:] = aug.astype(BF16)
            vt_ref[n] = v_ref[rows, :].T.astype(BF16)
            km_ref[pl.ds(n, 1), :] = jnp.mean(kblk, axis=0, keepdims=True)
            return carry

        lax.fori_loop(0, n_blocks, prep, 0)
        r = lax.broadcasted_iota(jnp.int32, pos_ref.shape, 0)
        rest = jnp.full(pos_ref.shape, slope2, F32)
        rows = jnp.zeros(pos_ref.shape, F32)
        for t in range(SPLIT):
            term = rest.astype(BF16).astype(F32)
            rest = rest - term
            rows = jnp.where(r == t, term * MOBA_BLOCK, jnp.where(r == t + SPLIT, term, rows))
        pos_ref[...] = rows.astype(BF16)

    qt = q_ref[...].astype(F32).T.astype(BF16)
    gate = _dot(km_ref[...].astype(BF16), qt)
    own = i * CHUNK_BLOCKS + lax.broadcasted_iota(jnp.int32, (1, KEY_CHUNK), 1) // MOBA_BLOCK
    sel = _top_blocks_t(gate, own.astype(F32))
    blk = lax.broadcasted_iota(jnp.int32, gate.shape, 0)
    qa_ref[:HEAD_DIM, :] = qt
    qa_ref[HEAD_DIM:HEAD_DIM + N_SEL_ROWS, :] = jnp.where(sel | (blk == own), 0.0, MASKED_BF16).astype(BF16)
    qa_ref[HEAD_DIM + N_SEL_ROWS:HEAD_DIM + N_SEL_ROWS + N_POS_ROWS, :] = jnp.concatenate(
        [pos_ref[...]] * CHUNK_BLOCKS, axis=1)
    qa_ref[HEAD_DIM + N_SEL_ROWS + N_POS_ROWS:, :] = jnp.zeros(
        (HEAD_DIM - N_SEL_ROWS - N_POS_ROWS, KEY_CHUNK), BF16)

    half = KEY_CHUNK // 2
    key = lax.broadcasted_iota(jnp.int32, (half, MOBA_BLOCK), 0)
    qry = lax.broadcasted_iota(jnp.int32, (half, MOBA_BLOCK), 1)

    def stage_logits(c, j, diagonal):
        cols = slice(j * MOBA_BLOCK, (j + 1) * MOBA_BLOCK)
        mx = None
        for s in range(2):
            rows = pl.ds(pl.multiple_of(c * KEY_CHUNK + s * half, half), half)
            u = _dot(ka_ref[rows, :], qa_ref[:, cols])
            if diagonal:
                u = jnp.where(key + s * half <= qry + j * MOBA_BLOCK, u, MASKED)
            u_ref[j, s * half:(s + 1) * half, :] = u
            hmax = jnp.max(u, axis=0, keepdims=True)
            mx = hmax if mx is None else jnp.maximum(mx, hmax)
        mx_ref[:, cols] = mx

    def stage_softmax(c, j, diagonal):
        cols = slice(j * MOBA_BLOCK, (j + 1) * MOBA_BLOCK)
        blocks_ahead = (c - i) * CHUNK_BLOCKS - j
        off = slope2 * lax.convert_element_type(blocks_ahead * MOBA_BLOCK, F32)
        m_new = mx_ref[:, cols] + off
        if not diagonal:
            m_old = m_ref[:, cols]
            m_new = jnp.maximum(m_old, m_new)
            alpha = jnp.exp2(m_old - m_new)
            alpha_ref[:, cols] = alpha
        p = jnp.exp2(u_ref[j] - (m_new - off))
        p_ref[j] = p.astype(BF16)
        psum = jnp.sum(p, axis=0, keepdims=True)
        l_ref[:, cols] = psum if diagonal else alpha * l_ref[:, cols] + psum
        m_ref[:, cols] = m_new

    def stage_values(c, j, diagonal):
        cols = slice(j * MOBA_BLOCK, (j + 1) * MOBA_BLOCK)
        parts = []
        for s in range(2):
            pv = None
            for n in range(s * CHUNK_BLOCKS // 2, (s + 1) * CHUNK_BLOCKS // 2):
                d = _dot(vt_ref[c * CHUNK_BLOCKS + n], p_ref[j, n * MOBA_BLOCK:(n + 1) * MOBA_BLOCK, :])
                pv = d if pv is None else pv + d
            parts.append(pv)
        pv = parts[0] + parts[1]
        acc_ref[:, cols] = pv if diagonal else alpha_ref[:, cols] * acc_ref[:, cols] + pv

    def sweep(c, diagonal):
        stage_logits(c, 0, diagonal)
        for j in range(CHUNK_BLOCKS):
            if j + 1 < CHUNK_BLOCKS:
                stage_logits(c, j + 1, diagonal)
            stage_softmax(c, j, diagonal)
            stage_values(c, j, diagonal)

    sweep(i, True)

    def past(c, carry):
        sweep(c, False)
        return carry

    lax.fori_loop(0, i, past, 0)
    o_ref[...] = (acc_ref[...] / l_ref[...]).T.astype(o_ref.dtype)


def _attn_prompt(slopes, q, k, v):
    t_len = q.shape[0]
    n_blocks = t_len // MOBA_BLOCK
    assert n_blocks <= N_SEL_ROWS and t_len % KEY_CHUNK == 0
    grid_spec = pltpu.PrefetchScalarGridSpec(
        num_scalar_prefetch=1,
        grid=(N_HEADS, t_len // KEY_CHUNK),
        in_specs=[
            pl.BlockSpec((KEY_CHUNK, HEAD_DIM), lambda h, i, s: (i, h)),
            pl.BlockSpec((t_len, HEAD_DIM), lambda h, i, s: (0, h)),
            pl.BlockSpec((t_len, HEAD_DIM), lambda h, i, s: (0, h)),
        ],
        out_specs=pl.BlockSpec((KEY_CHUNK, HEAD_DIM), lambda h, i, s: (i, h)),
        scratch_shapes=[
            pltpu.VMEM((t_len, 2 * HEAD_DIM), BF16),
            pltpu.VMEM((n_blocks, HEAD_DIM, MOBA_BLOCK), BF16),
            pltpu.VMEM((N_SEL_ROWS, HEAD_DIM), F32),
            pltpu.VMEM((N_POS_ROWS, MOBA_BLOCK), BF16),
            pltpu.VMEM((2 * HEAD_DIM, KEY_CHUNK), BF16),
            pltpu.VMEM((1, KEY_CHUNK), F32),
            pltpu.VMEM((1, KEY_CHUNK), F32),
            pltpu.VMEM((HEAD_DIM, KEY_CHUNK), F32),
            pltpu.VMEM((CHUNK_BLOCKS, KEY_CHUNK, MOBA_BLOCK), F32),
            pltpu.VMEM((CHUNK_BLOCKS, KEY_CHUNK, MOBA_BLOCK), BF16),
            pltpu.VMEM((1, KEY_CHUNK), F32),
            pltpu.VMEM((1, KEY_CHUNK), F32),
        ],
    )
    return pl.pallas_call(
        _attn_prompt_body,
        grid_spec=grid_spec,
        out_shape=jax.ShapeDtypeStruct((t_len, WIDTH), BF16),
        compiler_params=pltpu.CompilerParams(
            dimension_semantics=("parallel", "arbitrary"), vmem_limit_bytes=VMEM_LIMIT),
        name="attn_prompt",
    )(slopes, q, k, v)


GROUP_PAGES = 8
GROUP_BLOCKS = GROUP_PAGES // PAGES_PER_BLOCK


def _select_sample_body(pt_ref, q_ref, ck_ref, sel_ref, buf_ref, sem_ref, ksum_ref):
    n_seq, n_pages = pt_ref.shape
    groups_per_seq = n_pages // GROUP_PAGES
    n_steps = n_seq * groups_per_seq
    n_full = n_pages // PAGES_PER_BLOCK

    def page_copy(page, slot, p):
        return pltpu.make_async_copy(
            ck_ref.at[page, :, 0], buf_ref.at[slot, pl.ds(p * PAGE_SIZE, PAGE_SIZE)], sem_ref.at[slot])

    def start(t, slot):
        b = t // groups_per_seq
        g = t % groups_per_seq
        for p in range(GROUP_PAGES):
            page_copy(pt_ref[b, g * GROUP_PAGES + p], slot, p).start()

    def wait(slot):
        for p in range(GROUP_PAGES):
            page_copy(0, slot, p).wait()

    start(0, 0)

    def step(t, carry):
        slot = t % 2
        b = t // groups_per_seq
        g = t % groups_per_seq

        @pl.when(t + 1 < n_steps)
        def _():
            start(t + 1, 1 - slot)

        wait(slot)
        for n in range(GROUP_BLOCKS):
            blk = buf_ref[slot, pl.ds(n * MOBA_BLOCK, MOBA_BLOCK)]
            ksum_ref[g * GROUP_BLOCKS + n] = jnp.sum(blk, axis=0)

        @pl.when(g == groups_per_seq - 1)
        def _():
            kmean = ksum_ref[...] / MOBA_BLOCK
            gate = jnp.sum(kmean * q_ref[b], axis=-1, keepdims=True)
            blk_id = lax.broadcasted_iota(jnp.int32, gate.shape, 0).astype(F32)
            lane = lax.broadcasted_iota(jnp.int32, (N_HEADS, LANES), 1)
            picks = jnp.zeros((N_HEADS, LANES), F32)
            for j in range(MOBA_TOPK):
                mx = jnp.max(gate, axis=0, keepdims=True)
                first = jnp.min(jnp.where(gate == mx, blk_id, float(n_full)), axis=0, keepdims=True)
                picks = jnp.where(lane == j, first[0], picks)
                gate = jnp.where(blk_id == first, -jnp.inf, gate)
            sel_ref[b] = picks.astype(jnp.int32)
        return carry

    lax.fori_loop(0, n_steps, step, 0)


def _select_sample(page_table, q, cache_k):
    n_seq, n_pages = page_table.shape
    assert n_pages % GROUP_PAGES == 0
    n_full = n_pages // PAGES_PER_BLOCK
    grid_spec = pltpu.PrefetchScalarGridSpec(
        num_scalar_prefetch=1,
        grid=(1,),
        in_specs=[
            pl.BlockSpec((n_seq, N_HEADS, HEAD_DIM), lambda i, pt: (0, 0, 0)),
            pl.BlockSpec(memory_space=pl.ANY),
        ],
        out_specs=pl.BlockSpec((n_seq, N_HEADS, LANES), lambda i, pt: (0, 0, 0)),
        scratch_shapes=[
            pltpu.VMEM((2, GROUP_PAGES * PAGE_SIZE, N_HEADS, HEAD_DIM), F32),
            pltpu.SemaphoreType.DMA((2,)),
            pltpu.VMEM((n_full, N_HEADS, HEAD_DIM), F32),
        ],
    )
    return pl.pallas_call(
        _select_sample_body,
        grid_spec=grid_spec,
        out_shape=jax.ShapeDtypeStruct((n_seq, N_HEADS, LANES), jnp.int32),
        compiler_params=pltpu.CompilerParams(
            dimension_semantics=("arbitrary",), vmem_limit_bytes=VMEM_LIMIT),
        name="select_sample",
    )(page_table, q.reshape(n_seq, N_HEADS, HEAD_DIM), cache_k)


def _attn_sample_body(pt_ref, sel_ref, slope_ref, q_ref, kn_ref, vn_ref, ck_ref, cv_ref, o_ref,
                      kbuf_ref, vbuf_ref, sem_ref, *, past_len):
    b = pl.program_id(0)
    n_sel = MOBA_TOPK

    def copies(h, j, pp):
        page = pt_ref[b, sel_ref[b, h * MOBA_TOPK + j] * PAGES_PER_BLOCK + pp]
        rows = pl.ds((j * PAGES_PER_BLOCK + pp) * PAGE_SIZE, PAGE_SIZE)
        return (pltpu.make_async_copy(ck_ref.at[page, :, 0, h, :], kbuf_ref.at[h, rows, :], sem_ref.at[0]),
                pltpu.make_async_copy(cv_ref.at[page, :, 0, h, :], vbuf_ref.at[h, rows, :], sem_ref.at[1]))

    gathers = [copies(h, j, pp) for h in range(N_HEADS) for j in range(n_sel)
               for pp in range(PAGES_PER_BLOCK)]
    for ck, cv in gathers:
        ck.start()
        cv.start()
    for ck, cv in gathers:
        ck.wait()
        cv.wait()

    n_keys = n_sel * MOBA_BLOCK
    lane = lax.broadcasted_iota(jnp.int32, (1, n_keys), 1)
    slot = lane // MOBA_BLOCK
    offs = lane % MOBA_BLOCK
    outs = []
    for h in range(N_HEADS):
        sl = slice(h * HEAD_DIM, (h + 1) * HEAD_DIM)
        slope = slope_ref[h]
        qh = q_ref[:, sl]
        kn = kn_ref[:, sl]
        vn = vn_ref[:, sl]
        q8 = jnp.broadcast_to(qh, (8, HEAD_DIM)).astype(BF16)
        s = _dot_nt(q8, kbuf_ref[h].astype(BF16))[:1] * ATTN_SCALE
        blk = jnp.zeros((1, n_keys), jnp.int32)
        for j in range(n_sel):
            blk = jnp.where(slot == j, sel_ref[b, h * MOBA_TOPK + j], blk)
        dist = (past_len - (blk * MOBA_BLOCK + offs)).astype(F32)
        s = s - slope * dist
        s_own = jnp.sum(qh * kn, axis=-1, keepdims=True) * ATTN_SCALE
        mx = jnp.maximum(jnp.max(s, axis=-1, keepdims=True), s_own)
        p = jnp.exp(s - mx)
        p_own = jnp.exp(s_own - mx)
        denom = jnp.sum(p, axis=-1, keepdims=True) + p_own
        p8 = jnp.broadcast_to(p, (8, n_keys)).astype(BF16)
        pv = _dot(p8, vbuf_ref[h].astype(BF16))[:1]
        outs.append((pv + p_own * vn) / denom)
    o_ref[...] = jnp.concatenate(outs, axis=-1)


def _attn_sample(page_table, sel, slopes, q, k_new, v_new, cache_k, cache_v):
    n_seq, n_pages = page_table.shape
    per_seq = pl.BlockSpec((None, 1, WIDTH), lambda b, *_: (b, 0, 0))
    grid_spec = pltpu.PrefetchScalarGridSpec(
        num_scalar_prefetch=3,
        grid=(n_seq,),
        in_specs=[
            per_seq, per_seq, per_seq,
            pl.BlockSpec(memory_space=pl.ANY),
            pl.BlockSpec(memory_space=pl.ANY),
        ],
        out_specs=per_seq,
        scratch_shapes=[
            pltpu.VMEM((N_HEADS, MOBA_TOPK * MOBA_BLOCK, HEAD_DIM), F32),
            pltpu.VMEM((N_HEADS, MOBA_TOPK * MOBA_BLOCK, HEAD_DIM), F32),
            pltpu.SemaphoreType.DMA((2,)),
        ],
    )
    return pl.pallas_call(
        functools.partial(_attn_sample_body, past_len=n_pages * PAGE_SIZE),
        grid_spec=grid_spec,
        out_shape=jax.ShapeDtypeStruct((n_seq, 1, WIDTH), F32),
        compiler_params=pltpu.CompilerParams(
            dimension_semantics=("arbitrary",), vmem_limit_bytes=VMEM_LIMIT),
        name="attn_sample",
    )(page_table, sel, slopes, q.reshape(n_seq, 1, WIDTH), k_new.reshape(n_seq, 1, WIDTH),
      v_new.reshape(n_seq, 1, WIDTH), cache_k, cache_v).reshape(n_seq, WIDTH)


def kernel(x_prompt, x_sample, cache_k, cache_v, page_table, ffn1_w_gate, ffn1_w_up, ffn1_w_down, ln1_g, ln1_b, w_in, a_norm_g, a_norm_b, a_spatial_w, a_spatial_b, w_out, ln2_g, ln2_b, ffn2_w_gate, ffn2_w_up, ffn2_w_down, ln3_g, ln3_b):
    assert x_prompt.shape[0] == 1 and x_sample.shape[1] == 1 and w_in.shape[0] == DEPTH == 1
    t_len = x_prompt.shape[1]
    n_seq = x_sample.shape[0]
    n_pool = cache_k.shape[0]
    xp = x_prompt.reshape(t_len, D_MODEL)
    xs = x_sample.reshape(n_seq, D_MODEL)
    slopes = jnp.exp2(-8.0 * jnp.arange(1, N_HEADS + 1, dtype=F32) / N_HEADS)

    row = lambda v: v.reshape(1, -1)
    ln1, ln2, ln3 = ((row(g[0]), row(b[0])) for g, b in ((ln1_g, ln1_b), (ln2_g, ln2_b), (ln3_g, ln3_b)))
    ag, ab = row(a_norm_g[0]), row(a_norm_b[0])
    ws, bs = a_spatial_w[0], a_spatial_b[0]
    bs_exp = jnp.repeat(bs.T, HEAD_DIM, axis=1)
    w00 = row(jnp.repeat(ws[:, 0, 0], HEAD_DIM))
    b0 = row(jnp.repeat(bs[:, 0], HEAD_DIM))

    hs, w1g, w1u, w1d = _ffn_ln_sample(xs, ffn1_w_gate[0], ffn1_w_up[0], ffn1_w_down[0], *ln1,
                                       tf=256, accurate=True, name="ffn1_sample")
    hp = _ffn_ln(xp, w1g, w1u, w1d, *ln1, tm=512, tf=512, name="ffn1_prompt")

    qs, ks, vs, gm_s, van_s, w_in_b = _proj_sample(hs, w_in[0], ag, ab, w00, b0)
    qp, kp, vp, gm_p = _proj_prompt(hp, w_in_b, ag, ab, ws, bs_exp, tm=512)

    sel = _select_sample(page_table, qs, cache_k)
    sel = sel[:, :, :MOBA_TOPK].reshape(n_seq, N_HEADS * MOBA_TOPK)
    attn_s = _attn_sample(page_table, sel, slopes, qs, ks, vs, cache_k, cache_v)
    attn_p = _attn_prompt(slopes, qp, kp, vp)

    hs, w_out_b = _out_ln_sample(jnp.concatenate([attn_s, gm_s], axis=-1), hs, w_out[0], *ln2, tk=512)
    hp = _out_ln(attn_p, gm_p, hp, w_out_b, *ln2, tm=512)

    ys, w2g, w2u, w2d = _ffn_ln_sample(hs, ffn2_w_gate[0], ffn2_w_up[0], ffn2_w_down[0], *ln3,
                                       tf=256, accurate=False, name="ffn2_sample")
    yp = _ffn_ln(hp, w2g, w2u, w2d, *ln3, tm=512, tf=512, name="ffn2_prompt")

    heads = (1, N_HEADS, HEAD_DIM)
    return (yp.reshape(1, t_len, D_MODEL), ys.reshape(n_seq, 1, D_MODEL),
            kp.reshape((1, t_len) + heads), vp.reshape((1, t_len) + heads),
            ks.reshape((n_seq, 1) + heads), vs.reshape((n_seq, 1) + heads),
            van_s.reshape((n_seq, 1) + heads))
```

```python
import functools

import jax
import jax.numpy as jnp
from jax import lax
from jax.experimental import pallas as pl
from jax.experimental.pallas import tpu as pltpu

D_MODEL = 2048
HEAD_DIM = 128
N_HEADS = 8
WIDTH = N_HEADS * HEAD_DIM
CHUNK = 128
MOBA_BLOCK = 256
MOBA_TOPK = 3
PAGE_SIZE = 128
PAGES_PER_BLOCK = MOBA_BLOCK // PAGE_SIZE
D_FF = 5632
LN_EPS = 1e-5
DEPTH = 1
DN_ALPHA = (2 * DEPTH) ** 0.25
ATTN_SCALE = HEAD_DIM ** -0.5
LOG2E = 1.4426950408889634

MASKED = -1e30
MASKED_BF16 = -(2.0 ** 100)

LANES = 128
VMEM_LIMIT = 56 * 1024 * 1024

BF16 = jnp.bfloat16
F32 = jnp.float32


def _layer_norm(x, g, b):
    mu = jnp.mean(x, axis=-1, keepdims=True)
    xc = x - mu
    var = jnp.mean(xc * xc, axis=-1, keepdims=True)
    return xc * lax.rsqrt(var + LN_EPS) * g + b


def _dot(a, b):
    return jnp.dot(a, b, preferred_element_type=F32)


def _dot_nt(a, b):
    return lax.dot_general(a, b, (((1,), (1,)), ((), ())), preferred_element_type=F32)


def _ffn_ln_body(x_ref, wg_ref, wu_ref, wd_ref, g_ref, b_ref, o_ref, xb_ref, acc_ref, side_work=None):
    j = pl.program_id(1)

    @pl.when(j == 0)
    def _():
        xb_ref[...] = x_ref[...].astype(BF16)
        acc_ref[...] = jnp.zeros_like(acc_ref)

    if side_work is not None:
        side_work()
    xb = xb_ref[...]
    gate = _dot(xb, wg_ref[...])
    up = _dot(xb, wu_ref[...])
    act = gate * jax.nn.sigmoid(gate) * up
    acc_ref[...] += _dot(act.astype(BF16), wd_ref[...])

    @pl.when(j == pl.num_programs(1) - 1)
    def _():
        y = DN_ALPHA * x_ref[...] + 0.5 * acc_ref[...]
        o_ref[...] = _layer_norm(y, g_ref[...], b_ref[...])


def _ffn_ln(x, wg, wu, wd, g, b, *, tm, tf, name):
    rows = x.shape[0]
    return pl.pallas_call(
        _ffn_ln_body,
        grid=(rows // tm, D_FF // tf),
        in_specs=[
            pl.BlockSpec((tm, D_MODEL), lambda i, j: (i, 0)),
            pl.BlockSpec((D_MODEL, tf), lambda i, j: (0, j)),
            pl.BlockSpec((D_MODEL, tf), lambda i, j: (0, j)),
            pl.BlockSpec((tf, D_MODEL), lambda i, j: (j, 0)),
            pl.BlockSpec((1, D_MODEL), lambda i, j: (0, 0)),
            pl.BlockSpec((1, D_MODEL), lambda i, j: (0, 0)),
        ],
        out_specs=pl.BlockSpec((tm, D_MODEL), lambda i, j: (i, 0)),
        out_shape=jax.ShapeDtypeStruct((rows, D_MODEL), F32),
        scratch_shapes=[pltpu.VMEM((tm, D_MODEL), BF16), pltpu.VMEM((tm, D_MODEL), F32)],
        compiler_params=pltpu.CompilerParams(
            dimension_semantics=("parallel", "arbitrary"), vmem_limit_bytes=VMEM_LIMIT),
        name=name,
    )(x, wg, wu, wd, g, b)


def _split(x):
    hi = x.astype(BF16)
    return hi, (x - hi.astype(F32)).astype(BF16)


def _dot_split(x, w):
    x_hi, x_lo = _split(x)
    w_hi, w_lo = _split(w)
    n = x.shape[0]
    both = _dot(jnp.concatenate([x_hi, x_lo], axis=0), w_hi)
    return both[:n] + (both[n:] + _dot(x_hi, w_lo)), w_hi


def _dot_cast(x, w):
    w_hi = w.astype(BF16)
    return _dot(x.astype(BF16), w_hi), w_hi


def _ffn_ln_sample_body(x_ref, wg_ref, wu_ref, wd_ref, g_ref, b_ref,
                        o_ref, wgb_ref, wub_ref, wdb_ref, acc_ref, *, accurate):
    j = pl.program_id(0)
    dot = _dot_split if accurate else _dot_cast

    @pl.when(j == 0)
    def _():
        acc_ref[...] = jnp.zeros_like(acc_ref)

    x = x_ref[...]
    gate, wgb_ref[...] = dot(x, wg_ref[...])
    up, wub_ref[...] = dot(x, wu_ref[...])
    down, wdb_ref[...] = dot(gate * jax.nn.sigmoid(gate) * up, wd_ref[...])
    acc_ref[...] += down

    @pl.when(j == pl.num_programs(0) - 1)
    def _():
        y = DN_ALPHA * x + 0.5 * acc_ref[...]
        o_ref[...] = _layer_norm(y, g_ref[...], b_ref[...])


def _ffn_ln_sample(x, wg, wu, wd, g, b, *, tf, accurate, name):
    rows = x.shape[0]
    whole = lambda j: (0, 0)
    cols = pl.BlockSpec((D_MODEL, tf), lambda j: (0, j))
    rws = pl.BlockSpec((tf, D_MODEL), lambda j: (j, 0))
    return pl.pallas_call(
        functools.partial(_ffn_ln_sample_body, accurate=accurate),
        grid=(D_FF // tf,),
        in_specs=[pl.BlockSpec((rows, D_MODEL), whole), cols, cols, rws,
                  pl.BlockSpec((1, D_MODEL), whole), pl.BlockSpec((1, D_MODEL), whole)],
        out_specs=[pl.BlockSpec((rows, D_MODEL), whole), cols, cols, rws],
        out_shape=[jax.ShapeDtypeStruct((rows, D_MODEL), F32),
                   jax.ShapeDtypeStruct(wg.shape, BF16),
                   jax.ShapeDtypeStruct(wu.shape, BF16),
                   jax.ShapeDtypeStruct(wd.shape, BF16)],
        scratch_shapes=[pltpu.VMEM((rows, D_MODEL), F32)],
        compiler_params=pltpu.CompilerParams(
            dimension_semantics=("arbitrary",), vmem_limit_bytes=VMEM_LIMIT),
        name=name,
    )(x, wg, wu, wd, g, b)


def _head_layer_norm(va, g, b):
    parts = []
    for h in range(N_HEADS):
        sl = slice(h * HEAD_DIM, (h + 1) * HEAD_DIM)
        parts.append(_layer_norm(va[:, sl], g[:, sl], b[:, sl]))
    return jnp.concatenate(parts, axis=-1)


def _proj_prompt_body(x_ref, w_ref, ag_ref, ab_ref, ws_ref, bs_ref,
                      q_ref, k_ref, v_ref, gm_ref, xb_ref, u_ref):
    j = pl.program_id(1)

    @pl.when(j == 0)
    def _():
        xb_ref[...] = x_ref[...].astype(BF16)

    p = _dot(xb_ref[...], w_ref[...])

    @pl.when(j == 0)
    def _():
        q_ref[...] = (p * (ATTN_SCALE * LOG2E)).astype(BF16)

    @pl.when(j == 1)
    def _():
        k_ref[...] = p

    @pl.when(j == 2)
    def _():
        v_ref[...] = p

    @pl.when(j == 3)
    def _():
        u_ref[...] = p

    @pl.when(j == 4)
    def _():
        van = _head_layer_norm(p, ag_ref[...], ab_ref[...]).astype(BF16)
        row = lax.broadcasted_iota(jnp.int32, (CHUNK, CHUNK), 0)
        col = lax.broadcasted_iota(jnp.int32, (CHUNK, CHUNK), 1)
        causal = row >= col
        n_chunks = p.shape[0] // CHUNK
        for h in range(N_HEADS):
            sl = slice(h * HEAD_DIM, (h + 1) * HEAD_DIM)
            w_h = jnp.where(causal, ws_ref[h], 0.0).astype(BF16)
            for c in range(n_chunks):
                rs = slice(c * CHUNK, (c + 1) * CHUNK)
                mixed = _dot(w_h, van[rs, sl]) + bs_ref[:, sl]
                gm_ref[rs, sl] = (u_ref[rs, sl] * mixed).astype(BF16)


def _proj_prompt(x, w_in, ag, ab, ws, bs_exp, *, tm):
    rows = x.shape[0]
    row_blk = lambda i, j: (i, 0)
    const2 = lambda i, j: (0, 0)
    return pl.pallas_call(
        _proj_prompt_body,
        grid=(rows // tm, 5),
        in_specs=[
            pl.BlockSpec((tm, D_MODEL), row_blk),
            pl.BlockSpec((D_MODEL, WIDTH), lambda i, j: (0, j)),
            pl.BlockSpec((1, WIDTH), const2),
            pl.BlockSpec((1, WIDTH), const2),
            pl.BlockSpec((N_HEADS, CHUNK, CHUNK), lambda i, j: (0, 0, 0)),
            pl.BlockSpec((CHUNK, WIDTH), const2),
        ],
        out_specs=[
            pl.BlockSpec((tm, WIDTH), row_blk),
            pl.BlockSpec((tm, WIDTH), row_blk),
            pl.BlockSpec((tm, WIDTH), row_blk),
            pl.BlockSpec((tm, WIDTH), row_blk),
        ],
        out_shape=[
            jax.ShapeDtypeStruct((rows, WIDTH), BF16),
            jax.ShapeDtypeStruct((rows, WIDTH), F32),
            jax.ShapeDtypeStruct((rows, WIDTH), F32),
            jax.ShapeDtypeStruct((rows, WIDTH), BF16),
        ],
        scratch_shapes=[pltpu.VMEM((tm, D_MODEL), BF16), pltpu.VMEM((tm, WIDTH), F32)],
        compiler_params=pltpu.CompilerParams(
            dimension_semantics=("parallel", "arbitrary"), vmem_limit_bytes=VMEM_LIMIT),
        name="proj_prompt",
    )(x, w_in, ag, ab, ws, bs_exp)


def _proj_sample_body(x_ref, w_ref, ag_ref, ab_ref, w00_ref, b0_ref,
                      q_ref, k_ref, v_ref, gm_ref, van_ref, wb_ref, u_ref):
    j = pl.program_id(0)
    p, w_hi = _dot_split(x_ref[...], w_ref[...])
    wb_ref[...] = w_hi

    @pl.when(j == 0)
    def _():
        q_ref[...] = p

    @pl.when(j == 1)
    def _():
        k_ref[...] = p

    @pl.when(j == 2)
    def _():
        v_ref[...] = p

    @pl.when(j == 3)
    def _():
        u_ref[...] = p

    @pl.when(j == 4)
    def _():
        van = _head_layer_norm(p, ag_ref[...], ab_ref[...])
        van_ref[...] = van
        gm_ref[...] = u_ref[...] * (w00_ref[...] * van + b0_ref[...])


def _proj_sample(x, w_in, ag, ab, w00, b0):
    rows = x.shape[0]
    whole = lambda j: (0, 0)
    w_cols = lambda j: (0, j)
    return pl.pallas_call(
        _proj_sample_body,
        grid=(5,),
        in_specs=[
            pl.BlockSpec((rows, D_MODEL), whole),
            pl.BlockSpec((D_MODEL, WIDTH), w_cols),
            pl.BlockSpec((1, WIDTH), whole),
            pl.BlockSpec((1, WIDTH), whole),
            pl.BlockSpec((1, WIDTH), whole),
            pl.BlockSpec((1, WIDTH), whole),
        ],
        out_specs=[pl.BlockSpec((rows, WIDTH), whole)] * 5 + [pl.BlockSpec((D_MODEL, WIDTH), w_cols)],
        out_shape=[
            jax.ShapeDtypeStruct((rows, WIDTH), F32),
            jax.ShapeDtypeStruct((rows, WIDTH), F32),
            jax.ShapeDtypeStruct((rows, WIDTH), F32),
            jax.ShapeDtypeStruct((rows, WIDTH), F32),
            jax.ShapeDtypeStruct((rows, WIDTH), F32),
            jax.ShapeDtypeStruct(w_in.shape, BF16),
        ],
        scratch_shapes=[pltpu.VMEM((rows, WIDTH), F32)],
        compiler_params=pltpu.CompilerParams(
            dimension_semantics=("arbitrary",), vmem_limit_bytes=VMEM_LIMIT),
        name="proj_sample",
    )(x, w_in, ag, ab, w00, b0)


def _out_ln_body(a_ref, m_ref, h_ref, wa_ref, wm_ref, g_ref, b_ref, o_ref):
    mix = _dot(a_ref[...], wa_ref[...]) + _dot(m_ref[...], wm_ref[...])
    o_ref[...] = _layer_norm(DN_ALPHA * h_ref[...] + mix, g_ref[...], b_ref[...])


def _out_ln(attn, gm, h, w_out, g, b, *, tm):
    rows = h.shape[0]
    row_blk = lambda i: (i, 0)
    const = lambda i: (0, 0)
    return pl.pallas_call(
        _out_ln_body,
        grid=(rows // tm,),
        in_specs=[
            pl.BlockSpec((tm, WIDTH), row_blk),
            pl.BlockSpec((tm, WIDTH), row_blk),
            pl.BlockSpec((tm, D_MODEL), row_blk),
            pl.BlockSpec((WIDTH, D_MODEL), const),
            pl.BlockSpec((WIDTH, D_MODEL), lambda i: (1, 0)),
            pl.BlockSpec((1, D_MODEL), const),
            pl.BlockSpec((1, D_MODEL), const),
        ],
        out_specs=pl.BlockSpec((tm, D_MODEL), row_blk),
        out_shape=jax.ShapeDtypeStruct((rows, D_MODEL), F32),
        compiler_params=pltpu.CompilerParams(
            dimension_semantics=("parallel",), vmem_limit_bytes=VMEM_LIMIT),
        name="out_prompt",
    )(attn, gm, h, w_out, w_out, g, b)


def _out_ln_sample_body(x_ref, h_ref, w_ref, g_ref, b_ref, o_ref, wb_ref, acc_ref):
    j = pl.program_id(0)

    @pl.when(j == 0)
    def _():
        acc_ref[...] = jnp.zeros_like(acc_ref)

    part, wb_ref[...] = _dot_cast(x_ref[...], w_ref[...])
    acc_ref[...] += part

    @pl.when(j == pl.num_programs(0) - 1)
    def _():
        o_ref[...] = _layer_norm(DN_ALPHA * h_ref[...] + acc_ref[...], g_ref[...], b_ref[...])


def _out_ln_sample(mix, h, w_out, g, b, *, tk):
    rows = h.shape[0]
    whole = lambda j: (0, 0)
    slab = pl.BlockSpec((tk, D_MODEL), lambda j: (j, 0))
    return pl.pallas_call(
        _out_ln_sample_body,
        grid=(w_out.shape[0] // tk,),
        in_specs=[pl.BlockSpec((rows, tk), lambda j: (0, j)), pl.BlockSpec((rows, D_MODEL), whole), slab,
                  pl.BlockSpec((1, D_MODEL), whole), pl.BlockSpec((1, D_MODEL), whole)],
        out_specs=[pl.BlockSpec((rows, D_MODEL), whole), slab],
        out_shape=[jax.ShapeDtypeStruct((rows, D_MODEL), F32), jax.ShapeDtypeStruct(w_out.shape, BF16)],
        scratch_shapes=[pltpu.VMEM((rows, D_MODEL), F32)],
        compiler_params=pltpu.CompilerParams(
            dimension_semantics=("arbitrary",), vmem_limit_bytes=VMEM_LIMIT),
        name="out_sample",
    )(mix, h, w_out, g, b)


CHUNK_BLOCKS = 4
KEY_CHUNK = CHUNK_BLOCKS * MOBA_BLOCK
N_SEL_ROWS = 32
N_POS_ROWS = 16
SPLIT = 3


def _top_blocks_t(gate, n_valid):
    blk = lax.broadcasted_iota(jnp.int32, gate.shape, 0).astype(F32)
    valid = blk < n_valid
    g = jnp.where(valid, gate, -jnp.inf)
    sel = jnp.zeros(gate.shape, dtype=jnp.bool_)
    for _ in range(MOBA_TOPK):
        mx = jnp.max(g, axis=0, keepdims=True)
        first = jnp.min(jnp.where(g == mx, blk, float(gate.shape[0])), axis=0, keepdims=True)
        hit = blk == first
        sel = sel | (hit & valid)
        g = jnp.where(hit, -jnp.inf, g)
    return sel


def _attn_prompt_body(slope_ref, q_ref, k_ref, v_ref, o_ref,
                      ka_ref, vt_ref, km_ref, pos_ref, qa_ref, m_ref, l_ref, acc_ref,
                      u_ref, p_ref, mx_ref, alpha_ref):
    h = pl.program_id(0)
    i = pl.program_id(1)
    n_blocks = k_ref.shape[0] // MOBA_BLOCK
    slope2 = slope_ref[h] * LOG2E

    @pl.when(i == 0)
    def _():
        km_ref[...] = jnp.zeros_like(km_ref)
        lane = lax.broadcasted_iota(jnp.int32, (MOBA_BLOCK, LANES), 1)
        offs = lax.broadcasted_iota(jnp.int32, (MOBA_BLOCK, LANES), 0).astype(F32)

        def prep(n, carry):
            rows = pl.ds(pl.multiple_of(n * MOBA_BLOCK, MOBA_BLOCK), MOBA_BLOCK)
            kblk = k_ref[rows, :]
            in_chunk = lax.convert_element_type(n % CHUNK_BLOCKS, F32)
            aug = jnp.where(lane == n, 1.0, 0.0)
            aug = jnp.where((lane >= N_SEL_ROWS) & (lane < N_SEL_ROWS + SPLIT), in_chunk, aug)
            aug = jnp.where((lane >= N_SEL_ROWS + SPLIT) & (lane < N_SEL_ROWS + 2 * SPLIT), offs, aug)
            ka_ref[rows, :HEAD_DIM] = kblk.astype(BF16)
            ka_ref[rows, HEAD_DIM:] = aug.astype(BF16)
            vt_ref[n] = v_ref[rows, :].T.astype(BF16)
            km_ref[pl.ds(n, 1), :] = jnp.mean(kblk, axis=0, keepdims=True)
            return carry

        lax.fori_loop(0, n_blocks, prep, 0)
        r = lax.broadcasted_iota(jnp.int32, pos_ref.shape, 0)
        rest = jnp.full(pos_ref.shape, slope2, F32)
        rows = jnp.zeros(pos_ref.shape, F32)
        for t in range(SPLIT):
            term = rest.astype(BF16).astype(F32)
            rest = rest - term
            rows = jnp.where(r == t, term * MOBA_BLOCK, jnp.where(r == t + SPLIT, term, rows))
        pos_ref[...] = rows.astype(BF16)

    qt = q_ref[...].astype(F32).T.astype(BF16)
    gate = _dot(km_ref[...].astype(BF16), qt)
    own = i * CHUNK_BLOCKS + lax.broadcasted_iota(jnp.int32, (1, KEY_CHUNK), 1) // MOBA_BLOCK
    sel = _top_blocks_t(gate, own.astype(F32))
    blk = lax.broadcasted_iota(jnp.int32, gate.shape, 0)
    qa_ref[:HEAD_DIM, :] = qt
    qa_ref[HEAD_DIM:HEAD_DIM + N_SEL_ROWS, :] = jnp.where(sel | (blk == own), 0.0, MASKED_BF16).astype(BF16)
    qa_ref[HEAD_DIM + N_SEL_ROWS:HEAD_DIM + N_SEL_ROWS + N_POS_ROWS, :] = jnp.concatenate(
        [pos_ref[...]] * CHUNK_BLOCKS, axis=1)
    qa_ref[HEAD_DIM + N_SEL_ROWS + N_POS_ROWS:, :] = jnp.zeros(
        (HEAD_DIM - N_SEL_ROWS - N_POS_ROWS, KEY_CHUNK), BF16)

    half = KEY_CHUNK // 2
    key = lax.broadcasted_iota(jnp.int32, (half, MOBA_BLOCK), 0)
    qry = lax.broadcasted_iota(jnp.int32, (half, MOBA_BLOCK), 1)

    def stage_logits(c, j, diagonal):
        cols = slice(j * MOBA_BLOCK, (j + 1) * MOBA_BLOCK)
        mx = None
        for s in range(2):
            rows = pl.ds(pl.multiple_of(c * KEY_CHUNK + s * half, half), half)
            u = _dot(ka_ref[rows, :], qa_ref[:, cols])
            if diagonal:
                u = jnp.where(key + s * half <= qry + j * MOBA_BLOCK, u, MASKED)
            u_ref[j, s * half:(s + 1) * half, :] = u
            hmax = jnp.max(u, axis=0, keepdims=True)
            mx = hmax if mx is None else jnp.maximum(mx, hmax)
        mx_ref[:, cols] = mx

    def stage_softmax(c, j, diagonal):
        cols = slice(j * MOBA_BLOCK, (j + 1) * MOBA_BLOCK)
        blocks_ahead = (c - i) * CHUNK_BLOCKS - j
        off = slope2 * lax.convert_element_type(blocks_ahead * MOBA_BLOCK, F32)
        m_new = mx_ref[:, cols] + off
        if not diagonal:
            m_old = m_ref[:, cols]
            m_new = jnp.maximum(m_old, m_new)
            alpha = jnp.exp2(m_old - m_new)
            alpha_ref[:, cols] = alpha
        p = jnp.exp2(u_ref[j] - (m_new - off))
        p_ref[j] = p.astype(BF16)
        psum = jnp.sum(p, axis=0, keepdims=True)
        l_ref[:, cols] = psum if diagonal else alpha * l_ref[:, cols] + psum
        m_ref[:, cols] = m_new

    def stage_values(c, j, diagonal):
        cols = slice(j * MOBA_BLOCK, (j + 1) * MOBA_BLOCK)
        parts = []
        for s in range(2):
            pv = None
            for n in range(s * CHUNK_BLOCKS // 2, (s + 1) * CHUNK_BLOCKS // 2):
                d = _dot(vt_ref[c * CHUNK_BLOCKS + n], p_ref[j, n * MOBA_BLOCK:(n + 1) * MOBA_BLOCK, :])
                pv = d if pv is None else pv + d
            parts.append(pv)
        pv = parts[0] + parts[1]
        acc_ref[:, cols] = pv if diagonal else alpha_ref[:, cols] * acc_ref[:, cols] + pv

    def sweep(c, diagonal):
        stage_logits(c, 0, diagonal)
        for j in range(CHUNK_BLOCKS):
            if j + 1 < CHUNK_BLOCKS:
                stage_logits(c, j + 1, diagonal)
            stage_softmax(c, j, diagonal)
            stage_values(c, j, diagonal)

    sweep(i, True)

    def past(c, carry):
        sweep(c, False)
        return carry

    lax.fori_loop(0, i, past, 0)
    o_ref[...] = (acc_ref[...] / l_ref[...]).T.astype(o_ref.dtype)


def _attn_prompt(slopes, q, k, v):
    t_len = q.shape[0]
    n_blocks = t_len // MOBA_BLOCK
    assert n_blocks <= N_SEL_ROWS and t_len % KEY_CHUNK == 0
    grid_spec = pltpu.PrefetchScalarGridSpec(
        num_scalar_prefetch=1,
        grid=(N_HEADS, t_len // KEY_CHUNK),
        in_specs=[
            pl.BlockSpec((KEY_CHUNK, HEAD_DIM), lambda h, i, s: (i, h)),
            pl.BlockSpec((t_len, HEAD_DIM), lambda h, i, s: (0, h)),
            pl.BlockSpec((t_len, HEAD_DIM), lambda h, i, s: (0, h)),
        ],
        out_specs=pl.BlockSpec((KEY_CHUNK, HEAD_DIM), lambda h, i, s: (i, h)),
        scratch_shapes=[
            pltpu.VMEM((t_len, 2 * HEAD_DIM), BF16),
            pltpu.VMEM((n_blocks, HEAD_DIM, MOBA_BLOCK), BF16),
            pltpu.VMEM((N_SEL_ROWS, HEAD_DIM), F32),
            pltpu.VMEM((N_POS_ROWS, MOBA_BLOCK), BF16),
            pltpu.VMEM((2 * HEAD_DIM, KEY_CHUNK), BF16),
            pltpu.VMEM((1, KEY_CHUNK), F32),
            pltpu.VMEM((1, KEY_CHUNK), F32),
            pltpu.VMEM((HEAD_DIM, KEY_CHUNK), F32),
            pltpu.VMEM((CHUNK_BLOCKS, KEY_CHUNK, MOBA_BLOCK), F32),
            pltpu.VMEM((CHUNK_BLOCKS, KEY_CHUNK, MOBA_BLOCK), BF16),
            pltpu.VMEM((1, KEY_CHUNK), F32),
            pltpu.VMEM((1, KEY_CHUNK), F32),
        ],
    )
    return pl.pallas_call(
        _attn_prompt_body,
        grid_spec=grid_spec,
        out_shape=jax.ShapeDtypeStruct((t_len, WIDTH), BF16),
        compiler_params=pltpu.CompilerParams(
            dimension_semantics=("parallel", "arbitrary"), vmem_limit_bytes=VMEM_LIMIT),
        name="attn_prompt",
    )(slopes, q, k, v)


GROUP_PAGES = 8
GROUP_BLOCKS = GROUP_PAGES // PAGES_PER_BLOCK


class _SelectStep:
    def __init__(self, t, pt_ref, q_ref, ck_ref, sel_ref, buf_ref, sem_ref, ksum_ref):
        self.t, self.pt_ref, self.q_ref, self.ck_ref = t, pt_ref, q_ref, ck_ref
        self.sel_ref, self.buf_ref, self.sem_ref, self.ksum_ref = sel_ref, buf_ref, sem_ref, ksum_ref
        n_seq, n_pages = pt_ref.shape
        self.groups_per_seq = n_pages // GROUP_PAGES
        self.n_steps = n_seq * self.groups_per_seq
        self.n_full = n_pages // PAGES_PER_BLOCK
        self.active = t < self.n_steps
        self.slot = t % 2
        self.seq = t // self.groups_per_seq
        self.group = t % self.groups_per_seq

    def _page_copy(self, page, slot, p):
        return pltpu.make_async_copy(
            self.ck_ref.at[page, :, 0], self.buf_ref.at[slot, pl.ds(p * PAGE_SIZE, PAGE_SIZE)],
            self.sem_ref.at[slot])

    def _start(self, t, slot):
        b = t // self.groups_per_seq
        g = t % self.groups_per_seq
        for p in range(GROUP_PAGES):
            self._page_copy(self.pt_ref[b, g * GROUP_PAGES + p], slot, p).start()

    def copies(self):
        @pl.when(self.t == 0)
        def _():
            self._start(0, 0)

        @pl.when(self.t + 1 < self.n_steps)
        def _():
            self._start(self.t + 1, 1 - self.slot)

        @pl.when(self.active)
        def _():
            for p in range(GROUP_PAGES):
                self._page_copy(0, self.slot, p).wait()

    def block_sums(self):
        base = jnp.where(self.active, self.group * GROUP_BLOCKS, self.n_full)
        for n in range(GROUP_BLOCKS):
            blk = self.buf_ref[self.slot, pl.ds(n * MOBA_BLOCK, MOBA_BLOCK)]
            self.ksum_ref[base + n] = jnp.sum(blk, axis=0)

    def pick(self):
        n_full = self.n_full

        @pl.when(self.active & (self.group == self.groups_per_seq - 1))
        def _():
            kmean = self.ksum_ref[:n_full] / MOBA_BLOCK
            gate = jnp.sum(kmean * self.q_ref[self.seq], axis=-1, keepdims=True)
            blk_id = lax.broadcasted_iota(jnp.int32, gate.shape, 0).astype(F32)
            lane = lax.broadcasted_iota(jnp.int32, (N_HEADS, LANES), 1)
            picks = jnp.zeros((N_HEADS, LANES), F32)
            for j in range(MOBA_TOPK):
                mx = jnp.max(gate, axis=0, keepdims=True)
                first = jnp.min(jnp.where(gate == mx, blk_id, float(n_full)), axis=0, keepdims=True)
                picks = jnp.where(lane == j, first[0], picks)
                gate = jnp.where(blk_id == first, -jnp.inf, gate)
            self.sel_ref[self.seq] = picks.astype(jnp.int32)


def _ffn_ln_select_body(pt_ref, x_ref, wg_ref, wu_ref, wd_ref, g_ref, b_ref, q_ref, ck_ref,
                        o_ref, sel_ref, xb_ref, acc_ref, buf_ref, sem_ref, ksum_ref):
    t = pl.program_id(0) * pl.num_programs(1) + pl.program_id(1)
    step = _SelectStep(t, pt_ref, q_ref, ck_ref, sel_ref, buf_ref, sem_ref, ksum_ref)
    step.copies()
    _ffn_ln_body(x_ref, wg_ref, wu_ref, wd_ref, g_ref, b_ref, o_ref, xb_ref, acc_ref,
                 side_work=step.block_sums)
    step.pick()


def _ffn_ln_select(x, wg, wu, wd, g, b, page_table, q_sample, cache_k, *, tm, tf):
    rows = x.shape[0]
    n_seq, n_pages = page_table.shape
    n_full = n_pages // PAGES_PER_BLOCK
    grid = (rows // tm, D_FF // tf)
    assert n_pages % GROUP_PAGES == 0 and grid[0] * grid[1] >= n_seq * n_pages // GROUP_PAGES
    grid_spec = pltpu.PrefetchScalarGridSpec(
        num_scalar_prefetch=1,
        grid=grid,
        in_specs=[
            pl.BlockSpec((tm, D_MODEL), lambda i, j, pt: (i, 0)),
            pl.BlockSpec((D_MODEL, tf), lambda i, j, pt: (0, j)),
            pl.BlockSpec((D_MODEL, tf), lambda i, j, pt: (0, j)),
            pl.BlockSpec((tf, D_MODEL), lambda i, j, pt: (j, 0)),
            pl.BlockSpec((1, D_MODEL), lambda i, j, pt: (0, 0)),
            pl.BlockSpec((1, D_MODEL), lambda i, j, pt: (0, 0)),
            pl.BlockSpec((n_seq, N_HEADS, HEAD_DIM), lambda i, j, pt: (0, 0, 0)),
            pl.BlockSpec(memory_space=pl.ANY),
        ],
        out_specs=[
            pl.BlockSpec((tm, D_MODEL), lambda i, j, pt: (i, 0)),
            pl.BlockSpec((n_seq, N_HEADS, LANES), lambda i, j, pt: (0, 0, 0)),
        ],
        scratch_shapes=[
            pltpu.VMEM((tm, D_MODEL), BF16),
            pltpu.VMEM((tm, D_MODEL), F32),
            pltpu.VMEM((2, GROUP_PAGES * PAGE_SIZE, N_HEADS, HEAD_DIM), F32),
            pltpu.SemaphoreType.DMA((2,)),
            pltpu.VMEM((n_full + GROUP_BLOCKS, N_HEADS, HEAD_DIM), F32),
        ],
    )
    return pl.pallas_call(
        _ffn_ln_select_body,
        grid_spec=grid_spec,
        out_shape=[jax.ShapeDtypeStruct((rows, D_MODEL), F32),
                   jax.ShapeDtypeStruct((n_seq, N_HEADS, LANES), jnp.int32)],
        compiler_params=pltpu.CompilerParams(
            dimension_semantics=("arbitrary", "arbitrary"), vmem_limit_bytes=VMEM_LIMIT),
        name="ffn1_prompt_select",
    )(page_table, x, wg, wu, wd, g, b, q_sample.reshape(n_seq, N_HEADS, HEAD_DIM), cache_k)


def _attn_sample_body(pt_ref, sel_ref, slope_ref, q_ref, kn_ref, vn_ref, ck_ref, cv_ref, o_ref,
                      kbuf_ref, vbuf_ref, sem_ref, *, past_len):
    b = pl.program_id(0)
    n_sel = MOBA_TOPK

    def copies(h, j, pp):
        page = pt_ref[b, sel_ref[b, h * MOBA_TOPK + j] * PAGES_PER_BLOCK + pp]
        rows = pl.ds((j * PAGES_PER_BLOCK + pp) * PAGE_SIZE, PAGE_SIZE)
        return (pltpu.make_async_copy(ck_ref.at[page, :, 0, h, :], kbuf_ref.at[h, rows, :], sem_ref.at[0]),
                pltpu.make_async_copy(cv_ref.at[page, :, 0, h, :], vbuf_ref.at[h, rows, :], sem_ref.at[1]))

    gathers = [copies(h, j, pp) for h in range(N_HEADS) for j in range(n_sel)
               for pp in range(PAGES_PER_BLOCK)]
    for ck, cv in gathers:
        ck.start()
        cv.start()
    for ck, cv in gathers:
        ck.wait()
        cv.wait()

    n_keys = n_sel * MOBA_BLOCK
    lane = lax.broadcasted_iota(jnp.int32, (1, n_keys), 1)
    slot = lane // MOBA_BLOCK
    offs = lane % MOBA_BLOCK
    outs = []
    for h in range(N_HEADS):
        sl = slice(h * HEAD_DIM, (h + 1) * HEAD_DIM)
        slope = slope_ref[h]
        qh = q_ref[:, sl]
        kn = kn_ref[:, sl]
        vn = vn_ref[:, sl]
        q8 = jnp.broadcast_to(qh, (8, HEAD_DIM)).astype(BF16)
        s = _dot_nt(q8, kbuf_ref[h].astype(BF16))[:1] * ATTN_SCALE
        blk = jnp.zeros((1, n_keys), jnp.int32)
        for j in range(n_sel):
            blk = jnp.where(slot == j, sel_ref[b, h * MOBA_TOPK + j], blk)
        dist = (past_len - (blk * MOBA_BLOCK + offs)).astype(F32)
        s = s - slope * dist
        s_own = jnp.sum(qh * kn, axis=-1, keepdims=True) * ATTN_SCALE
        mx = jnp.maximum(jnp.max(s, axis=-1, keepdims=True), s_own)
        p = jnp.exp(s - mx)
        p_own = jnp.exp(s_own - mx)
        denom = jnp.sum(p, axis=-1, keepdims=True) + p_own
        p8 = jnp.broadcast_to(p, (8, n_keys)).astype(BF16)
        pv = _dot(p8, vbuf_ref[h].astype(BF16))[:1]
        outs.append((pv + p_own * vn) / denom)
    o_ref[...] = jnp.concatenate(outs, axis=-1)


def _attn_sample(page_table, sel, slopes, q, k_new, v_new, cache_k, cache_v):
    n_seq, n_pages = page_table.shape
    per_seq = pl.BlockSpec((None, 1, WIDTH), lambda b, *_: (b, 0, 0))
    grid_spec = pltpu.PrefetchScalarGridSpec(
        num_scalar_prefetch=3,
        grid=(n_seq,),
        in_specs=[
            per_seq, per_seq, per_seq,
            pl.BlockSpec(memory_space=pl.ANY),
            pl.BlockSpec(memory_space=pl.ANY),
        ],
        out_specs=per_seq,
        scratch_shapes=[
            pltpu.VMEM((N_HEADS, MOBA_TOPK * MOBA_BLOCK, HEAD_DIM), F32),
            pltpu.VMEM((N_HEADS, MOBA_TOPK * MOBA_BLOCK, HEAD_DIM), F32),
            pltpu.SemaphoreType.DMA((2,)),
        ],
    )
    return pl.pallas_call(
        functools.partial(_attn_sample_body, past_len=n_pages * PAGE_SIZE),
        grid_spec=grid_spec,
        out_shape=jax.ShapeDtypeStruct((n_seq, 1, WIDTH), F32),
        compiler_params=pltpu.CompilerParams(
            dimension_semantics=("arbitrary",), vmem_limit_bytes=VMEM_LIMIT),
        name="attn_sample",
    )(page_table, sel, slopes, q.reshape(n_seq, 1, WIDTH), k_new.reshape(n_seq, 1, WIDTH),
      v_new.reshape(n_seq, 1, WIDTH), cache_k, cache_v).reshape(n_seq, WIDTH)


def kernel(x_prompt, x_sample, cache_k, cache_v, page_table, ffn1_w_gate, ffn1_w_up, ffn1_w_down, ln1_g, ln1_b, w_in, a_norm_g, a_norm_b, a_spatial_w, a_spatial_b, w_out, ln2_g, ln2_b, ffn2_w_gate, ffn2_w_up, ffn2_w_down, ln3_g, ln3_b):
    assert x_prompt.shape[0] == 1 and x_sample.shape[1] == 1 and w_in.shape[0] == DEPTH == 1
    t_len = x_prompt.shape[1]
    n_seq = x_sample.shape[0]
    xp = x_prompt.reshape(t_len, D_MODEL)
    xs = x_sample.reshape(n_seq, D_MODEL)
    slopes = jnp.exp2(-8.0 * jnp.arange(1, N_HEADS + 1, dtype=F32) / N_HEADS)

    row = lambda v: v.reshape(1, -1)
    ln1, ln2, ln3 = ((row(g[0]), row(b[0])) for g, b in ((ln1_g, ln1_b), (ln2_g, ln2_b), (ln3_g, ln3_b)))
    ag, ab = row(a_norm_g[0]), row(a_norm_b[0])
    ws, bs = a_spatial_w[0], a_spatial_b[0]
    bs_exp = jnp.repeat(bs.T, HEAD_DIM, axis=1)
    w00 = row(jnp.repeat(ws[:, 0, 0], HEAD_DIM))
    b0 = row(jnp.repeat(bs[:, 0], HEAD_DIM))

    hs, w1g, w1u, w1d = _ffn_ln_sample(xs, ffn1_w_gate[0], ffn1_w_up[0], ffn1_w_down[0], *ln1,
                                       tf=256, accurate=True, name="ffn1_sample")
    qs, ks, vs, gm_s, van_s, w_in_b = _proj_sample(hs, w_in[0], ag, ab, w00, b0)
    hp, sel = _ffn_ln_select(xp, w1g, w1u, w1d, *ln1, page_table, qs, cache_k, tm=512, tf=256)
    qp, kp, vp, gm_p = _proj_prompt(hp, w_in_b, ag, ab, ws, bs_exp, tm=512)

    sel = sel[:, :, :MOBA_TOPK].reshape(n_seq, N_HEADS * MOBA_TOPK)
    attn_s = _attn_sample(page_table, sel, slopes, qs, ks, vs, cache_k, cache_v)
    attn_p = _attn_prompt(slopes, qp, kp, vp)

    hs, w_out_b = _out_ln_sample(jnp.concatenate([attn_s, gm_s], axis=-1), hs, w_out[0], *ln2, tk=512)
    hp = _out_ln(attn_p, gm_p, hp, w_out_b, *ln2, tm=512)

    ys, w2g, w2u, w2d = _ffn_ln_sample(hs, ffn2_w_gate[0], ffn2_w_up[0], ffn2_w_down[0], *ln3,
                                       tf=256, accurate=False, name="ffn2_sample")
    yp = _ffn_ln(hp, w2g, w2u, w2d, *ln3, tm=512, tf=512, name="ffn2_prompt")

    heads = (1, N_HEADS, HEAD_DIM)
    return (yp.reshape(1, t_len, D_MODEL), ys.reshape(n_seq, 1, D_MODEL),
            kp.reshape((1, t_len) + heads), vp.reshape((1, t_len) + heads),
            ks.reshape((n_seq, 1) + heads), vs.reshape((n_seq, 1) + heads),
            van_s.reshape((n_seq, 1) + heads))
```

```python
import functools

import jax
import jax.numpy as jnp
from jax import lax
from jax.experimental import pallas as pl
from jax.experimental.pallas import tpu as pltpu

D_MODEL = 2048
HEAD_DIM = 128
N_HEADS = 8
WIDTH = N_HEADS * HEAD_DIM
CHUNK = 128
MOBA_BLOCK = 256
MOBA_TOPK = 3
PAGE_SIZE = 128
PAGES_PER_BLOCK = MOBA_BLOCK // PAGE_SIZE
D_FF = 5632
LN_EPS = 1e-5
DEPTH = 1
DN_ALPHA = (2 * DEPTH) ** 0.25
ATTN_SCALE = HEAD_DIM ** -0.5
LOG2E = 1.4426950408889634

MASKED = -1e30
MASKED_BF16 = -(2.0 ** 100)

LANES = 128
VMEM_LIMIT = 56 * 1024 * 1024

BF16 = jnp.bfloat16
F32 = jnp.float32


def _layer_norm(x, g, b):
    mu = jnp.mean(x, axis=-1, keepdims=True)
    xc = x - mu
    var = jnp.mean(xc * xc, axis=-1, keepdims=True)
    return xc * lax.rsqrt(var + LN_EPS) * g + b


def _dot(a, b):
    return jnp.dot(a, b, preferred_element_type=F32)


def _dot_nt(a, b):
    return lax.dot_general(a, b, (((1,), (1,)), ((), ())), preferred_element_type=F32)


def _ffn_ln_body(x_ref, wg_ref, wu_ref, wd_ref, g_ref, b_ref, o_ref, xb_ref, side_work=None):
    j = pl.program_id(1)

    @pl.when(j == 0)
    def _():
        xb_ref[...] = x_ref[...].astype(BF16)
        o_ref[...] = jnp.zeros_like(o_ref)

    if side_work is not None:
        side_work()
    xb = xb_ref[...]
    gate = _dot(xb, wg_ref[...])
    up = _dot(xb, wu_ref[...])
    act = gate * jax.nn.sigmoid(gate) * up
    o_ref[...] += _dot(act.astype(BF16), wd_ref[...])

    @pl.when(j == pl.num_programs(1) - 1)
    def _():
        y = DN_ALPHA * x_ref[...] + 0.5 * o_ref[...]
        o_ref[...] = _layer_norm(y, g_ref[...], b_ref[...])


def _ffn_ln(x, wg, wu, wd, g, b, *, tm, tf, name):
    rows = x.shape[0]
    return pl.pallas_call(
        _ffn_ln_body,
        grid=(rows // tm, D_FF // tf),
        in_specs=[
            pl.BlockSpec((tm, D_MODEL), lambda i, j: (i, 0)),
            pl.BlockSpec((D_MODEL, tf), lambda i, j: (0, j)),
            pl.BlockSpec((D_MODEL, tf), lambda i, j: (0, j)),
            pl.BlockSpec((tf, D_MODEL), lambda i, j: (j, 0)),
            pl.BlockSpec((1, D_MODEL), lambda i, j: (0, 0)),
            pl.BlockSpec((1, D_MODEL), lambda i, j: (0, 0)),
        ],
        out_specs=pl.BlockSpec((tm, D_MODEL), lambda i, j: (i, 0)),
        out_shape=jax.ShapeDtypeStruct((rows, D_MODEL), F32),
        scratch_shapes=[pltpu.VMEM((tm, D_MODEL), BF16)],
        compiler_params=pltpu.CompilerParams(
            dimension_semantics=("parallel", "arbitrary"), vmem_limit_bytes=VMEM_LIMIT),
        name=name,
    )(x, wg, wu, wd, g, b)


def _split(x):
    hi = x.astype(BF16)
    return hi, (x - hi.astype(F32)).astype(BF16)


def _dot_split(x, w):
    x_hi, x_lo = _split(x)
    w_hi, w_lo = _split(w)
    n = x.shape[0]
    both = _dot(jnp.concatenate([x_hi, x_lo], axis=0), w_hi)
    return both[:n] + (both[n:] + _dot(x_hi, w_lo)), w_hi


def _dot_cast(x, w):
    w_hi = w.astype(BF16)
    return _dot(x.astype(BF16), w_hi), w_hi


def _ffn_ln_sample_body(x_ref, wg_ref, wu_ref, wd_ref, g_ref, b_ref,
                        o_ref, wgb_ref, wub_ref, wdb_ref, acc_ref, *, accurate):
    j = pl.program_id(0)
    dot = _dot_split if accurate else _dot_cast

    @pl.when(j == 0)
    def _():
        acc_ref[...] = jnp.zeros_like(acc_ref)

    x = x_ref[...]
    gate, wgb_ref[...] = dot(x, wg_ref[...])
    up, wub_ref[...] = dot(x, wu_ref[...])
    down, wdb_ref[...] = dot(gate * jax.nn.sigmoid(gate) * up, wd_ref[...])
    acc_ref[...] += down

    @pl.when(j == pl.num_programs(0) - 1)
    def _():
        y = DN_ALPHA * x + 0.5 * acc_ref[...]
        o_ref[...] = _layer_norm(y, g_ref[...], b_ref[...])


def _ffn_ln_sample(x, wg, wu, wd, g, b, *, tf, accurate, name):
    rows = x.shape[0]
    whole = lambda j: (0, 0)
    cols = pl.BlockSpec((D_MODEL, tf), lambda j: (0, j))
    rws = pl.BlockSpec((tf, D_MODEL), lambda j: (j, 0))
    return pl.pallas_call(
        functools.partial(_ffn_ln_sample_body, accurate=accurate),
        grid=(D_FF // tf,),
        in_specs=[pl.BlockSpec((rows, D_MODEL), whole), cols, cols, rws,
                  pl.BlockSpec((1, D_MODEL), whole), pl.BlockSpec((1, D_MODEL), whole)],
        out_specs=[pl.BlockSpec((rows, D_MODEL), whole), cols, cols, rws],
        out_shape=[jax.ShapeDtypeStruct((rows, D_MODEL), F32),
                   jax.ShapeDtypeStruct(wg.shape, BF16),
                   jax.ShapeDtypeStruct(wu.shape, BF16),
                   jax.ShapeDtypeStruct(wd.shape, BF16)],
        scratch_shapes=[pltpu.VMEM((rows, D_MODEL), F32)],
        compiler_params=pltpu.CompilerParams(
            dimension_semantics=("arbitrary",), vmem_limit_bytes=VMEM_LIMIT),
        name=name,
    )(x, wg, wu, wd, g, b)


def _head_layer_norm(va, g, b):
    parts = []
    for h in range(N_HEADS):
        sl = slice(h * HEAD_DIM, (h + 1) * HEAD_DIM)
        parts.append(_layer_norm(va[:, sl], g[:, sl], b[:, sl]))
    return jnp.concatenate(parts, axis=-1)


def _proj_prompt_body(x_ref, w_ref, ag_ref, ab_ref, ws_ref, bs_ref,
                      q_ref, k_ref, v_ref, gm_ref, xb_ref, u_ref):
    j = pl.program_id(1)

    @pl.when(j == 0)
    def _():
        xb_ref[...] = x_ref[...].astype(BF16)

    p = _dot(xb_ref[...], w_ref[...])

    @pl.when(j == 0)
    def _():
        q_ref[...] = (p * (ATTN_SCALE * LOG2E)).astype(BF16)

    @pl.when(j == 1)
    def _():
        k_ref[...] = p

    @pl.when(j == 2)
    def _():
        v_ref[...] = p

    @pl.when(j == 3)
    def _():
        u_ref[...] = p

    @pl.when(j == 4)
    def _():
        van = _head_layer_norm(p, ag_ref[...], ab_ref[...]).astype(BF16)
        row = lax.broadcasted_iota(jnp.int32, (CHUNK, CHUNK), 0)
        col = lax.broadcasted_iota(jnp.int32, (CHUNK, CHUNK), 1)
        causal = row >= col
        n_chunks = p.shape[0] // CHUNK
        for h in range(N_HEADS):
            sl = slice(h * HEAD_DIM, (h + 1) * HEAD_DIM)
            w_h = jnp.where(causal, ws_ref[h], 0.0).astype(BF16)
            for c in range(n_chunks):
                rs = slice(c * CHUNK, (c + 1) * CHUNK)
                mixed = _dot(w_h, van[rs, sl]) + bs_ref[:, sl]
                gm_ref[rs, sl] = (u_ref[rs, sl] * mixed).astype(BF16)


def _proj_prompt(x, w_in, ag, ab, ws, bs_exp, *, tm):
    rows = x.shape[0]
    row_blk = lambda i, j: (i, 0)
    const2 = lambda i, j: (0, 0)
    return pl.pallas_call(
        _proj_prompt_body,
        grid=(rows // tm, 5),
        in_specs=[
            pl.BlockSpec((tm, D_MODEL), row_blk),
            pl.BlockSpec((D_MODEL, WIDTH), lambda i, j: (0, j)),
            pl.BlockSpec((1, WIDTH), const2),
            pl.BlockSpec((1, WIDTH), const2),
            pl.BlockSpec((N_HEADS, CHUNK, CHUNK), lambda i, j: (0, 0, 0)),
            pl.BlockSpec((CHUNK, WIDTH), const2),
        ],
        out_specs=[
            pl.BlockSpec((tm, WIDTH), row_blk),
            pl.BlockSpec((tm, WIDTH), row_blk),
            pl.BlockSpec((tm, WIDTH), row_blk),
            pl.BlockSpec((tm, WIDTH), row_blk),
        ],
        out_shape=[
            jax.ShapeDtypeStruct((rows, WIDTH), BF16),
            jax.ShapeDtypeStruct((rows, WIDTH), F32),
            jax.ShapeDtypeStruct((rows, WIDTH), F32),
            jax.ShapeDtypeStruct((rows, WIDTH), BF16),
        ],
        scratch_shapes=[pltpu.VMEM((tm, D_MODEL), BF16), pltpu.VMEM((tm, WIDTH), F32)],
        compiler_params=pltpu.CompilerParams(
            dimension_semantics=("parallel", "arbitrary"), vmem_limit_bytes=VMEM_LIMIT),
        name="proj_prompt",
    )(x, w_in, ag, ab, ws, bs_exp)


def _proj_sample_body(x_ref, w_ref, ag_ref, ab_ref, w00_ref, b0_ref,
                      q_ref, k_ref, v_ref, gm_ref, van_ref, wb_ref, u_ref):
    j = pl.program_id(0)
    p, w_hi = _dot_split(x_ref[...], w_ref[...])
    wb_ref[...] = w_hi

    @pl.when(j == 0)
    def _():
        q_ref[...] = p

    @pl.when(j == 1)
    def _():
        k_ref[...] = p

    @pl.when(j == 2)
    def _():
        v_ref[...] = p

    @pl.when(j == 3)
    def _():
        u_ref[...] = p

    @pl.when(j == 4)
    def _():
        van = _head_layer_norm(p, ag_ref[...], ab_ref[...])
        van_ref[...] = van
        gm_ref[...] = u_ref[...] * (w00_ref[...] * van + b0_ref[...])


def _proj_sample(x, w_in, ag, ab, w00, b0):
    rows = x.shape[0]
    whole = lambda j: (0, 0)
    w_cols = lambda j: (0, j)
    return pl.pallas_call(
        _proj_sample_body,
        grid=(5,),
        in_specs=[
            pl.BlockSpec((rows, D_MODEL), whole),
            pl.BlockSpec((D_MODEL, WIDTH), w_cols),
            pl.BlockSpec((1, WIDTH), whole),
            pl.BlockSpec((1, WIDTH), whole),
            pl.BlockSpec((1, WIDTH), whole),
            pl.BlockSpec((1, WIDTH), whole),
        ],
        out_specs=[pl.BlockSpec((rows, WIDTH), whole)] * 5 + [pl.BlockSpec((D_MODEL, WIDTH), w_cols)],
        out_shape=[
            jax.ShapeDtypeStruct((rows, WIDTH), F32),
            jax.ShapeDtypeStruct((rows, WIDTH), F32),
            jax.ShapeDtypeStruct((rows, WIDTH), F32),
            jax.ShapeDtypeStruct((rows, WIDTH), F32),
            jax.ShapeDtypeStruct((rows, WIDTH), F32),
            jax.ShapeDtypeStruct(w_in.shape, BF16),
        ],
        scratch_shapes=[pltpu.VMEM((rows, WIDTH), F32)],
        compiler_params=pltpu.CompilerParams(
            dimension_semantics=("arbitrary",), vmem_limit_bytes=VMEM_LIMIT),
        name="proj_sample",
    )(x, w_in, ag, ab, w00, b0)


def _out_ln_body(a_ref, m_ref, h_ref, wa_ref, wm_ref, g_ref, b_ref, o_ref):
    mix = _dot(a_ref[...], wa_ref[...]) + _dot(m_ref[...], wm_ref[...])
    o_ref[...] = _layer_norm(DN_ALPHA * h_ref[...] + mix, g_ref[...], b_ref[...])


def _out_ln(attn, gm, h, w_out, g, b, *, tm):
    rows = h.shape[0]
    row_blk = lambda i: (i, 0)
    const = lambda i: (0, 0)
    return pl.pallas_call(
        _out_ln_body,
        grid=(rows // tm,),
        in_specs=[
            pl.BlockSpec((tm, WIDTH), row_blk),
            pl.BlockSpec((tm, WIDTH), row_blk),
            pl.BlockSpec((tm, D_MODEL), row_blk),
            pl.BlockSpec((WIDTH, D_MODEL), const),
            pl.BlockSpec((WIDTH, D_MODEL), lambda i: (1, 0)),
            pl.BlockSpec((1, D_MODEL), const),
            pl.BlockSpec((1, D_MODEL), const),
        ],
        out_specs=pl.BlockSpec((tm, D_MODEL), row_blk),
        out_shape=jax.ShapeDtypeStruct((rows, D_MODEL), F32),
        compiler_params=pltpu.CompilerParams(
            dimension_semantics=("parallel",), vmem_limit_bytes=VMEM_LIMIT),
        name="out_prompt",
    )(attn, gm, h, w_out, w_out, g, b)


def _out_ln_sample_body(x_ref, h_ref, w_ref, g_ref, b_ref, o_ref, wb_ref, acc_ref):
    j = pl.program_id(0)

    @pl.when(j == 0)
    def _():
        acc_ref[...] = jnp.zeros_like(acc_ref)

    part, wb_ref[...] = _dot_cast(x_ref[...], w_ref[...])
    acc_ref[...] += part

    @pl.when(j == pl.num_programs(0) - 1)
    def _():
        o_ref[...] = _layer_norm(DN_ALPHA * h_ref[...] + acc_ref[...], g_ref[...], b_ref[...])


def _out_ln_sample(mix, h, w_out, g, b, *, tk):
    rows = h.shape[0]
    whole = lambda j: (0, 0)
    slab = pl.BlockSpec((tk, D_MODEL), lambda j: (j, 0))
    return pl.pallas_call(
        _out_ln_sample_body,
        grid=(w_out.shape[0] // tk,),
        in_specs=[pl.BlockSpec((rows, tk), lambda j: (0, j)), pl.BlockSpec((rows, D_MODEL), whole), slab,
                  pl.BlockSpec((1, D_MODEL), whole), pl.BlockSpec((1, D_MODEL), whole)],
        out_specs=[pl.BlockSpec((rows, D_MODEL), whole), slab],
        out_shape=[jax.ShapeDtypeStruct((rows, D_MODEL), F32), jax.ShapeDtypeStruct(w_out.shape, BF16)],
        scratch_shapes=[pltpu.VMEM((rows, D_MODEL), F32)],
        compiler_params=pltpu.CompilerParams(
            dimension_semantics=("arbitrary",), vmem_limit_bytes=VMEM_LIMIT),
        name="out_sample",
    )(mix, h, w_out, g, b)


CHUNK_BLOCKS = 4
KEY_CHUNK = CHUNK_BLOCKS * MOBA_BLOCK
N_SEL_ROWS = 32
N_POS_ROWS = 16
SPLIT = 3


def _top_blocks_t(gate, n_valid):
    blk = lax.broadcasted_iota(jnp.int32, gate.shape, 0).astype(F32)
    valid = blk < n_valid
    g = jnp.where(valid, gate, -jnp.inf)
    sel = jnp.zeros(gate.shape, dtype=jnp.bool_)
    for _ in range(MOBA_TOPK):
        mx = jnp.max(g, axis=0, keepdims=True)
        first = jnp.min(jnp.where(g == mx, blk, float(gate.shape[0])), axis=0, keepdims=True)
        hit = blk == first
        sel = sel | (hit & valid)
        g = jnp.where(hit, -jnp.inf, g)
    return sel


def _attn_prompt_body(slope_ref, q_ref, k_ref, v_ref, o_ref,
                      ka_ref, vt_ref, km_ref, pos_ref, qa_ref, m_ref, l_ref, acc_ref,
                      u_ref, p_ref, mx_ref, alpha_ref):
    h = pl.program_id(0)
    i = pl.program_id(1)
    n_blocks = k_ref.shape[0] // MOBA_BLOCK
    slope2 = slope_ref[h] * LOG2E

    @pl.when(i == 0)
    def _():
        km_ref[...] = jnp.zeros_like(km_ref)
        lane = lax.broadcasted_iota(jnp.int32, (MOBA_BLOCK, LANES), 1)
        offs = lax.broadcasted_iota(jnp.int32, (MOBA_BLOCK, LANES), 0).astype(F32)

        def prep(n, carry):
            rows = pl.ds(pl.multiple_of(n * MOBA_BLOCK, MOBA_BLOCK), MOBA_BLOCK)
            kblk = k_ref[rows, :]
            in_chunk = lax.convert_element_type(n % CHUNK_BLOCKS, F32)
            aug = jnp.where(lane == n, 1.0, 0.0)
            aug = jnp.where((lane >= N_SEL_ROWS) & (lane < N_SEL_ROWS + SPLIT), in_chunk, aug)
            aug = jnp.where((lane >= N_SEL_ROWS + SPLIT) & (lane < N_SEL_ROWS + 2 * SPLIT), offs, aug)
            ka_ref[rows, :HEAD_DIM] = kblk.astype(BF16)
            ka_ref[rows, HEAD_DIM:] = aug.astype(BF16)
            vt_ref[n] = v_ref[rows, :].T.astype(BF16)
            km_ref[pl.ds(n, 1), :] = jnp.mean(kblk, axis=0, keepdims=True)
            return carry

        lax.fori_loop(0, n_blocks, prep, 0)
        r = lax.broadcasted_iota(jnp.int32, pos_ref.shape, 0)
        rest = jnp.full(pos_ref.shape, slope2, F32)
        rows = jnp.zeros(pos_ref.shape, F32)
        for t in range(SPLIT):
            term = rest.astype(BF16).astype(F32)
            rest = rest - term
            rows = jnp.where(r == t, term * MOBA_BLOCK, jnp.where(r == t + SPLIT, term, rows))
        pos_ref[...] = rows.astype(BF16)

    qt = q_ref[...].astype(F32).T.astype(BF16)
    gate = _dot(km_ref[...].astype(BF16), qt)
    own = i * CHUNK_BLOCKS + lax.broadcasted_iota(jnp.int32, (1, KEY_CHUNK), 1) // MOBA_BLOCK
    sel = _top_blocks_t(gate, own.astype(F32))
    blk = lax.broadcasted_iota(jnp.int32, gate.shape, 0)
    qa_ref[:HEAD_DIM, :] = qt
    qa_ref[HEAD_DIM:HEAD_DIM + N_SEL_ROWS, :] = jnp.where(sel | (blk == own), 0.0, MASKED_BF16).astype(BF16)
    qa_ref[HEAD_DIM + N_SEL_ROWS:HEAD_DIM + N_SEL_ROWS + N_POS_ROWS, :] = jnp.concatenate(
        [pos_ref[...]] * CHUNK_BLOCKS, axis=1)
    qa_ref[HEAD_DIM + N_SEL_ROWS + N_POS_ROWS:, :] = jnp.zeros(
        (HEAD_DIM - N_SEL_ROWS - N_POS_ROWS, KEY_CHUNK), BF16)

    half = KEY_CHUNK // 2
    key = lax.broadcasted_iota(jnp.int32, (half, MOBA_BLOCK), 0)
    qry = lax.broadcasted_iota(jnp.int32, (half, MOBA_BLOCK), 1)

    def stage_logits(c, j, diagonal):
        cols = slice(j * MOBA_BLOCK, (j + 1) * MOBA_BLOCK)
        mx = None
        for s in range(2):
            rows = pl.ds(pl.multiple_of(c * KEY_CHUNK + s * half, half), half)
            u = _dot(ka_ref[rows, :], qa_ref[:, cols])
            if diagonal:
                u = jnp.where(key + s * half <= qry + j * MOBA_BLOCK, u, MASKED)
            u_ref[j, s * half:(s + 1) * half, :] = u
            hmax = jnp.max(u, axis=0, keepdims=True)
            mx = hmax if mx is None else jnp.maximum(mx, hmax)
        mx_ref[:, cols] = mx

    def stage_softmax(c, j, diagonal):
        cols = slice(j * MOBA_BLOCK, (j + 1) * MOBA_BLOCK)
        blocks_ahead = (c - i) * CHUNK_BLOCKS - j
        off = slope2 * lax.convert_element_type(blocks_ahead * MOBA_BLOCK, F32)
        m_new = mx_ref[:, cols] + off
        if not diagonal:
            m_old = m_ref[:, cols]
            m_new = jnp.maximum(m_old, m_new)
            alpha = jnp.exp2(m_old - m_new)
            alpha_ref[:, cols] = alpha
        p = jnp.exp2(u_ref[j] - (m_new - off))
        p_ref[j] = p.astype(BF16)
        psum = jnp.sum(p, axis=0, keepdims=True)
        l_ref[:, cols] = psum if diagonal else alpha * l_ref[:, cols] + psum
        m_ref[:, cols] = m_new

    def stage_values(c, j, diagonal):
        cols = slice(j * MOBA_BLOCK, (j + 1) * MOBA_BLOCK)
        parts = []
        for s in range(2):
            pv = None
            for n in range(s * CHUNK_BLOCKS // 2, (s + 1) * CHUNK_BLOCKS // 2):
                d = _dot(vt_ref[c * CHUNK_BLOCKS + n], p_ref[j, n * MOBA_BLOCK:(n + 1) * MOBA_BLOCK, :])
                pv = d if pv is None else pv + d
            parts.append(pv)
        pv = parts[0] + parts[1]
        acc_ref[:, cols] = pv if diagonal else alpha_ref[:, cols] * acc_ref[:, cols] + pv

    def sweep(c, diagonal):
        stage_logits(c, 0, diagonal)
        for j in range(CHUNK_BLOCKS):
            if j + 1 < CHUNK_BLOCKS:
                stage_logits(c, j + 1, diagonal)
            stage_softmax(c, j, diagonal)
            stage_values(c, j, diagonal)

    sweep(i, True)

    def past(c, carry):
        sweep(c, False)
        return carry

    lax.fori_loop(0, i, past, 0)
    o_ref[...] = (acc_ref[...] / l_ref[...]).T.astype(o_ref.dtype)


def _attn_prompt(slopes, q, k, v):
    t_len = q.shape[0]
    n_blocks = t_len // MOBA_BLOCK
    assert n_blocks <= N_SEL_ROWS and t_len % KEY_CHUNK == 0
    grid_spec = pltpu.PrefetchScalarGridSpec(
        num_scalar_prefetch=1,
        grid=(N_HEADS, t_len // KEY_CHUNK),
        in_specs=[
            pl.BlockSpec((KEY_CHUNK, HEAD_DIM), lambda h, i, s: (i, h)),
            pl.BlockSpec((t_len, HEAD_DIM), lambda h, i, s: (0, h)),
            pl.BlockSpec((t_len, HEAD_DIM), lambda h, i, s: (0, h)),
        ],
        out_specs=pl.BlockSpec((KEY_CHUNK, HEAD_DIM), lambda h, i, s: (i, h)),
        scratch_shapes=[
            pltpu.VMEM((t_len, 2 * HEAD_DIM), BF16),
            pltpu.VMEM((n_blocks, HEAD_DIM, MOBA_BLOCK), BF16),
            pltpu.VMEM((N_SEL_ROWS, HEAD_DIM), F32),
            pltpu.VMEM((N_POS_ROWS, MOBA_BLOCK), BF16),
            pltpu.VMEM((2 * HEAD_DIM, KEY_CHUNK), BF16),
            pltpu.VMEM((1, KEY_CHUNK), F32),
            pltpu.VMEM((1, KEY_CHUNK), F32),
            pltpu.VMEM((HEAD_DIM, KEY_CHUNK), F32),
            pltpu.VMEM((CHUNK_BLOCKS, KEY_CHUNK, MOBA_BLOCK), F32),
            pltpu.VMEM((CHUNK_BLOCKS, KEY_CHUNK, MOBA_BLOCK), BF16),
            pltpu.VMEM((1, KEY_CHUNK), F32),
            pltpu.VMEM((1, KEY_CHUNK), F32),
        ],
    )
    return pl.pallas_call(
        _attn_prompt_body,
        grid_spec=grid_spec,
        out_shape=jax.ShapeDtypeStruct((t_len, WIDTH), BF16),
        compiler_params=pltpu.CompilerParams(
            dimension_semantics=("parallel", "arbitrary"), vmem_limit_bytes=VMEM_LIMIT),
        name="attn_prompt",
    )(slopes, q, k, v)


GROUP_PAGES = 16
GROUP_BLOCKS = GROUP_PAGES // PAGES_PER_BLOCK


class _SelectStep:
    def __init__(self, t, pt_ref, q_ref, ck_ref, sel_ref, buf_ref, sem_ref, ksum_ref):
        self.t, self.pt_ref, self.q_ref, self.ck_ref = t, pt_ref, q_ref, ck_ref
        self.sel_ref, self.buf_ref, self.sem_ref, self.ksum_ref = sel_ref, buf_ref, sem_ref, ksum_ref
        n_seq, n_pages = pt_ref.shape
        self.groups_per_seq = n_pages // GROUP_PAGES
        self.n_steps = n_seq * self.groups_per_seq
        self.n_full = n_pages // PAGES_PER_BLOCK
        self.active = t < self.n_steps
        self.slot = t % 2
        self.seq = t // self.groups_per_seq
        self.group = t % self.groups_per_seq

    def _page_copy(self, page, slot, p):
        return pltpu.make_async_copy(
            self.ck_ref.at[page, :, 0], self.buf_ref.at[slot, pl.ds(p * PAGE_SIZE, PAGE_SIZE)],
            self.sem_ref.at[slot])

    def _start(self, t, slot):
        b = t // self.groups_per_seq
        g = t % self.groups_per_seq
        for p in range(GROUP_PAGES):
            self._page_copy(self.pt_ref[b, g * GROUP_PAGES + p], slot, p).start()

    def copies(self):
        @pl.when(self.t == 0)
        def _():
            self._start(0, 0)

        @pl.when(self.t + 1 < self.n_steps)
        def _():
            self._start(self.t + 1, 1 - self.slot)

        @pl.when(self.active)
        def _():
            for p in range(GROUP_PAGES):
                self._page_copy(0, self.slot, p).wait()

    def block_sums(self):
        base = jnp.where(self.active, self.group * GROUP_BLOCKS, self.n_full)
        for n in range(GROUP_BLOCKS):
            blk = self.buf_ref[self.slot, pl.ds(n * MOBA_BLOCK, MOBA_BLOCK)]
            self.ksum_ref[base + n] = jnp.sum(blk, axis=0)

    def pick(self):
        n_full = self.n_full

        @pl.when(self.active & (self.group == self.groups_per_seq - 1))
        def _():
            kmean = self.ksum_ref[:n_full] / MOBA_BLOCK
            gate = jnp.sum(kmean * self.q_ref[self.seq], axis=-1, keepdims=True)
            blk_id = lax.broadcasted_iota(jnp.int32, gate.shape, 0).astype(F32)
            lane = lax.broadcasted_iota(jnp.int32, (N_HEADS, LANES), 1)
            picks = jnp.zeros((N_HEADS, LANES), F32)
            for j in range(MOBA_TOPK):
                mx = jnp.max(gate, axis=0, keepdims=True)
                first = jnp.min(jnp.where(gate == mx, blk_id, float(n_full)), axis=0, keepdims=True)
                picks = jnp.where(lane == j, first[0], picks)
                gate = jnp.where(blk_id == first, -jnp.inf, gate)
            self.sel_ref[self.seq] = picks.astype(jnp.int32)


def _ffn_ln_select_body(pt_ref, x_ref, wg_ref, wu_ref, wd_ref, g_ref, b_ref, q_ref, ck_ref,
                        o_ref, sel_ref, xb_ref, buf_ref, sem_ref, ksum_ref):
    t = pl.program_id(0) * pl.num_programs(1) + pl.program_id(1)
    step = _SelectStep(t, pt_ref, q_ref, ck_ref, sel_ref, buf_ref, sem_ref, ksum_ref)
    step.copies()
    _ffn_ln_body(x_ref, wg_ref, wu_ref, wd_ref, g_ref, b_ref, o_ref, xb_ref, side_work=step.block_sums)
    step.pick()


def _ffn_ln_select(x, wg, wu, wd, g, b, page_table, q_sample, cache_k, *, tm, tf):
    rows = x.shape[0]
    n_seq, n_pages = page_table.shape
    n_full = n_pages // PAGES_PER_BLOCK
    grid = (rows // tm, D_FF // tf)
    assert n_pages % GROUP_PAGES == 0 and grid[0] * grid[1] >= n_seq * n_pages // GROUP_PAGES
    once = pl.Buffered(1)
    grid_spec = pltpu.PrefetchScalarGridSpec(
        num_scalar_prefetch=1,
        grid=grid,
        in_specs=[
            pl.BlockSpec((tm, D_MODEL), lambda i, j, pt: (i, 0), pipeline_mode=once),
            pl.BlockSpec((D_MODEL, tf), lambda i, j, pt: (0, j)),
            pl.BlockSpec((D_MODEL, tf), lambda i, j, pt: (0, j)),
            pl.BlockSpec((tf, D_MODEL), lambda i, j, pt: (j, 0)),
            pl.BlockSpec((1, D_MODEL), lambda i, j, pt: (0, 0)),
            pl.BlockSpec((1, D_MODEL), lambda i, j, pt: (0, 0)),
            pl.BlockSpec((n_seq, N_HEADS, HEAD_DIM), lambda i, j, pt: (0, 0, 0)),
            pl.BlockSpec(memory_space=pl.ANY),
        ],
        out_specs=[
            pl.BlockSpec((tm, D_MODEL), lambda i, j, pt: (i, 0), pipeline_mode=once),
            pl.BlockSpec((n_seq, N_HEADS, LANES), lambda i, j, pt: (0, 0, 0)),
        ],
        scratch_shapes=[
            pltpu.VMEM((tm, D_MODEL), BF16),
            pltpu.VMEM((2, GROUP_PAGES * PAGE_SIZE, N_HEADS, HEAD_DIM), F32),
            pltpu.SemaphoreType.DMA((2,)),
            pltpu.VMEM((n_full + GROUP_BLOCKS, N_HEADS, HEAD_DIM), F32),
        ],
    )
    return pl.pallas_call(
        _ffn_ln_select_body,
        grid_spec=grid_spec,
        out_shape=[jax.ShapeDtypeStruct((rows, D_MODEL), F32),
                   jax.ShapeDtypeStruct((n_seq, N_HEADS, LANES), jnp.int32)],
        compiler_params=pltpu.CompilerParams(
            dimension_semantics=("arbitrary", "arbitrary"), vmem_limit_bytes=VMEM_LIMIT),
        name="ffn1_prompt_select",
    )(page_table, x, wg, wu, wd, g, b, q_sample.reshape(n_seq, N_HEADS, HEAD_DIM), cache_k)


def _attn_sample_body(pt_ref, sel_ref, slope_ref, q_ref, kn_ref, vn_ref, ck_ref, cv_ref, o_ref,
                      kbuf_ref, vbuf_ref, sem_ref, *, past_len):
    b = pl.program_id(0)
    n_sel = MOBA_TOPK

    def copies(h, j, pp):
        page = pt_ref[b, sel_ref[b, h * MOBA_TOPK + j] * PAGES_PER_BLOCK + pp]
        rows = pl.ds((j * PAGES_PER_BLOCK + pp) * PAGE_SIZE, PAGE_SIZE)
        return (pltpu.make_async_copy(ck_ref.at[page, :, 0, h, :], kbuf_ref.at[h, rows, :], sem_ref.at[0]),
                pltpu.make_async_copy(cv_ref.at[page, :, 0, h, :], vbuf_ref.at[h, rows, :], sem_ref.at[1]))

    gathers = [copies(h, j, pp) for h in range(N_HEADS) for j in range(n_sel)
               for pp in range(PAGES_PER_BLOCK)]
    for ck, cv in gathers:
        ck.start()
        cv.start()
    for ck, cv in gathers:
        ck.wait()
        cv.wait()

    n_keys = n_sel * MOBA_BLOCK
    lane = lax.broadcasted_iota(jnp.int32, (1, n_keys), 1)
    slot = lane // MOBA_BLOCK
    offs = lane % MOBA_BLOCK
    outs = []
    for h in range(N_HEADS):
        sl = slice(h * HEAD_DIM, (h + 1) * HEAD_DIM)
        slope = slope_ref[h]
        qh = q_ref[:, sl]
        kn = kn_ref[:, sl]
        vn = vn_ref[:, sl]
        q8 = jnp.broadcast_to(qh, (8, HEAD_DIM)).astype(BF16)
        s = _dot_nt(q8, kbuf_ref[h].astype(BF16))[:1] * ATTN_SCALE
        blk = jnp.zeros((1, n_keys), jnp.int32)
        for j in range(n_sel):
            blk = jnp.where(slot == j, sel_ref[b, h * MOBA_TOPK + j], blk)
        dist = (past_len - (blk * MOBA_BLOCK + offs)).astype(F32)
        s = s - slope * dist
        s_own = jnp.sum(qh * kn, axis=-1, keepdims=True) * ATTN_SCALE
        mx = jnp.maximum(jnp.max(s, axis=-1, keepdims=True), s_own)
        p = jnp.exp(s - mx)
        p_own = jnp.exp(s_own - mx)
        denom = jnp.sum(p, axis=-1, keepdims=True) + p_own
        p8 = jnp.broadcast_to(p, (8, n_keys)).astype(BF16)
        pv = _dot(p8, vbuf_ref[h].astype(BF16))[:1]
        outs.append((pv + p_own * vn) / denom)
    o_ref[...] = jnp.concatenate(outs, axis=-1)


def _attn_sample(page_table, sel, slopes, q, k_new, v_new, cache_k, cache_v):
    n_seq, n_pages = page_table.shape
    per_seq = pl.BlockSpec((None, 1, WIDTH), lambda b, *_: (b, 0, 0))
    grid_spec = pltpu.PrefetchScalarGridSpec(
        num_scalar_prefetch=3,
        grid=(n_seq,),
        in_specs=[
            per_seq, per_seq, per_seq,
            pl.BlockSpec(memory_space=pl.ANY),
            pl.BlockSpec(memory_space=pl.ANY),
        ],
        out_specs=per_seq,
        scratch_shapes=[
            pltpu.VMEM((N_HEADS, MOBA_TOPK * MOBA_BLOCK, HEAD_DIM), F32),
            pltpu.VMEM((N_HEADS, MOBA_TOPK * MOBA_BLOCK, HEAD_DIM), F32),
            pltpu.SemaphoreType.DMA((2,)),
        ],
    )
    return pl.pallas_call(
        functools.partial(_attn_sample_body, past_len=n_pages * PAGE_SIZE),
        grid_spec=grid_spec,
        out_shape=jax.ShapeDtypeStruct((n_seq, 1, WIDTH), F32),
        compiler_params=pltpu.CompilerParams(
            dimension_semantics=("arbitrary",), vmem_limit_bytes=VMEM_LIMIT),
        name="attn_sample",
    )(page_table, sel, slopes, q.reshape(n_seq, 1, WIDTH), k_new.reshape(n_seq, 1, WIDTH),
      v_new.reshape(n_seq, 1, WIDTH), cache_k, cache_v).reshape(n_seq, WIDTH)


def kernel(x_prompt, x_sample, cache_k, cache_v, page_table, ffn1_w_gate, ffn1_w_up, ffn1_w_down, ln1_g, ln1_b, w_in, a_norm_g, a_norm_b, a_spatial_w, a_spatial_b, w_out, ln2_g, ln2_b, ffn2_w_gate, ffn2_w_up, ffn2_w_down, ln3_g, ln3_b):
    assert x_prompt.shape[0] == 1 and x_sample.shape[1] == 1 and w_in.shape[0] == DEPTH == 1
    t_len = x_prompt.shape[1]
    n_seq = x_sample.shape[0]
    xp = x_prompt.reshape(t_len, D_MODEL)
    xs = x_sample.reshape(n_seq, D_MODEL)
    slopes = jnp.exp2(-8.0 * jnp.arange(1, N_HEADS + 1, dtype=F32) / N_HEADS)

    row = lambda v: v.reshape(1, -1)
    ln1, ln2, ln3 = ((row(g[0]), row(b[0])) for g, b in ((ln1_g, ln1_b), (ln2_g, ln2_b), (ln3_g, ln3_b)))
    ag, ab = row(a_norm_g[0]), row(a_norm_b[0])
    ws, bs = a_spatial_w[0], a_spatial_b[0]
    bs_exp = jnp.repeat(bs.T, HEAD_DIM, axis=1)
    w00 = row(jnp.repeat(ws[:, 0, 0], HEAD_DIM))
    b0 = row(jnp.repeat(bs[:, 0], HEAD_DIM))

    hs, w1g, w1u, w1d = _ffn_ln_sample(xs, ffn1_w_gate[0], ffn1_w_up[0], ffn1_w_down[0], *ln1,
                                       tf=256, accurate=True, name="ffn1_sample")
    qs, ks, vs, gm_s, van_s, w_in_b = _proj_sample(hs, w_in[0], ag, ab, w00, b0)
    hp, sel = _ffn_ln_select(xp, w1g, w1u, w1d, *ln1, page_table, qs, cache_k, tm=1024, tf=256)
    qp, kp, vp, gm_p = _proj_prompt(hp, w_in_b, ag, ab, ws, bs_exp, tm=512)

    sel = sel[:, :, :MOBA_TOPK].reshape(n_seq, N_HEADS * MOBA_TOPK)
    attn_s = _attn_sample(page_table, sel, slopes, qs, ks, vs, cache_k, cache_v)
    attn_p = _attn_prompt(slopes, qp, kp, vp)

    hs, w_out_b = _out_ln_sample(jnp.concatenate([attn_s, gm_s], axis=-1), hs, w_out[0], *ln2, tk=512)
    hp = _out_ln(attn_p, gm_p, hp, w_out_b, *ln2, tm=512)

    ys, w2g, w2u, w2d = _ffn_ln_sample(hs, ffn2_w_gate[0], ffn2_w_up[0], ffn2_w_down[0], *ln3,
                                       tf=256, accurate=False, name="ffn2_sample")
    yp = _ffn_ln(hp, w2g, w2u, w2d, *ln3, tm=512, tf=512, name="ffn2_prompt")

    heads = (1, N_HEADS, HEAD_DIM)
    return (yp.reshape(1, t_len, D_MODEL), ys.reshape(n_seq, 1, D_MODEL),
            kp.reshape((1, t_len) + heads), vp.reshape((1, t_len) + heads),
            ks.reshape((n_seq, 1) + heads), vs.reshape((n_seq, 1) + heads),
            van_s.reshape((n_seq, 1) + heads))
```

```python
import functools

import jax
import jax.numpy as jnp
from jax import lax
from jax.experimental import pallas as pl
from jax.experimental.pallas import tpu as pltpu

D_MODEL = 2048
HEAD_DIM = 128
N_HEADS = 8
WIDTH = N_HEADS * HEAD_DIM
CHUNK = 128
MOBA_BLOCK = 256
MOBA_TOPK = 3
PAGE_SIZE = 128
PAGES_PER_BLOCK = MOBA_BLOCK // PAGE_SIZE
D_FF = 5632
LN_EPS = 1e-5
DEPTH = 1
DN_ALPHA = (2 * DEPTH) ** 0.25
ATTN_SCALE = HEAD_DIM ** -0.5
LOG2E = 1.4426950408889634

MASKED = -1e30
MASKED_BF16 = -(2.0 ** 100)

LANES = 128
VMEM_LIMIT = 56 * 1024 * 1024

BF16 = jnp.bfloat16
F32 = jnp.float32


def _layer_norm(x, g, b):
    mu = jnp.mean(x, axis=-1, keepdims=True)
    xc = x - mu
    var = jnp.mean(xc * xc, axis=-1, keepdims=True)
    return xc * lax.rsqrt(var + LN_EPS) * g + b


def _dot(a, b):
    return jnp.dot(a, b, preferred_element_type=F32)


def _dot_nt(a, b):
    return lax.dot_general(a, b, (((1,), (1,)), ((), ())), preferred_element_type=F32)


def _ffn_ln_body(x_ref, wg_ref, wu_ref, wd_ref, g_ref, b_ref, o_ref, xb_ref, side_work=None):
    j = pl.program_id(1)

    @pl.when(j == 0)
    def _():
        xb_ref[...] = x_ref[...].astype(BF16)
        o_ref[...] = jnp.zeros_like(o_ref)

    if side_work is not None:
        side_work()
    xb = xb_ref[...]
    gate = _dot(xb, wg_ref[...])
    up = _dot(xb, wu_ref[...])
    act = gate * jax.nn.sigmoid(gate) * up
    o_ref[...] += _dot(act.astype(BF16), wd_ref[...])

    @pl.when(j == pl.num_programs(1) - 1)
    def _():
        y = DN_ALPHA * x_ref[...] + 0.5 * o_ref[...]
        o_ref[...] = _layer_norm(y, g_ref[...], b_ref[...])


def _ffn_ln(x, wg, wu, wd, g, b, *, tm, tf, name):
    rows = x.shape[0]
    return pl.pallas_call(
        _ffn_ln_body,
        grid=(rows // tm, D_FF // tf),
        in_specs=[
            pl.BlockSpec((tm, D_MODEL), lambda i, j: (i, 0)),
            pl.BlockSpec((D_MODEL, tf), lambda i, j: (0, j)),
            pl.BlockSpec((D_MODEL, tf), lambda i, j: (0, j)),
            pl.BlockSpec((tf, D_MODEL), lambda i, j: (j, 0)),
            pl.BlockSpec((1, D_MODEL), lambda i, j: (0, 0)),
            pl.BlockSpec((1, D_MODEL), lambda i, j: (0, 0)),
        ],
        out_specs=pl.BlockSpec((tm, D_MODEL), lambda i, j: (i, 0)),
        out_shape=jax.ShapeDtypeStruct((rows, D_MODEL), F32),
        scratch_shapes=[pltpu.VMEM((tm, D_MODEL), BF16)],
        compiler_params=pltpu.CompilerParams(
            dimension_semantics=("parallel", "arbitrary"), vmem_limit_bytes=VMEM_LIMIT),
        name=name,
    )(x, wg, wu, wd, g, b)


def _split(x):
    hi = x.astype(BF16)
    return hi, (x - hi.astype(F32)).astype(BF16)


def _dot_split(x, w):
    x_hi, x_lo = _split(x)
    w_hi, w_lo = _split(w)
    n = x.shape[0]
    both = _dot(jnp.concatenate([x_hi, x_lo], axis=0), w_hi)
    return both[:n] + (both[n:] + _dot(x_hi, w_lo)), w_hi


def _dot_cast(x, w):
    w_hi = w.astype(BF16)
    return _dot(x.astype(BF16), w_hi), w_hi


def _ffn_ln_sample_body(x_ref, wg_ref, wu_ref, wd_ref, g_ref, b_ref,
                        o_ref, wgb_ref, wub_ref, wdb_ref, acc_ref, *, accurate):
    j = pl.program_id(0)
    dot = _dot_split if accurate else _dot_cast

    @pl.when(j == 0)
    def _():
        acc_ref[...] = jnp.zeros_like(acc_ref)

    x = x_ref[...]
    gate, wgb_ref[...] = dot(x, wg_ref[...])
    up, wub_ref[...] = dot(x, wu_ref[...])
    down, wdb_ref[...] = dot(gate * jax.nn.sigmoid(gate) * up, wd_ref[...])
    acc_ref[...] += down

    @pl.when(j == pl.num_programs(0) - 1)
    def _():
        y = DN_ALPHA * x + 0.5 * acc_ref[...]
        o_ref[...] = _layer_norm(y, g_ref[...], b_ref[...])


def _ffn_ln_sample(x, wg, wu, wd, g, b, *, tf, accurate, name):
    rows = x.shape[0]
    whole = lambda j: (0, 0)
    cols = pl.BlockSpec((D_MODEL, tf), lambda j: (0, j))
    rws = pl.BlockSpec((tf, D_MODEL), lambda j: (j, 0))
    return pl.pallas_call(
        functools.partial(_ffn_ln_sample_body, accurate=accurate),
        grid=(D_FF // tf,),
        in_specs=[pl.BlockSpec((rows, D_MODEL), whole), cols, cols, rws,
                  pl.BlockSpec((1, D_MODEL), whole), pl.BlockSpec((1, D_MODEL), whole)],
        out_specs=[pl.BlockSpec((rows, D_MODEL), whole), cols, cols, rws],
        out_shape=[jax.ShapeDtypeStruct((rows, D_MODEL), F32),
                   jax.ShapeDtypeStruct(wg.shape, BF16),
                   jax.ShapeDtypeStruct(wu.shape, BF16),
                   jax.ShapeDtypeStruct(wd.shape, BF16)],
        scratch_shapes=[pltpu.VMEM((rows, D_MODEL), F32)],
        compiler_params=pltpu.CompilerParams(
            dimension_semantics=("arbitrary",), vmem_limit_bytes=VMEM_LIMIT),
        name=name,
    )(x, wg, wu, wd, g, b)


def _head_layer_norm(va, g, b):
    parts = []
    for h in range(N_HEADS):
        sl = slice(h * HEAD_DIM, (h + 1) * HEAD_DIM)
        parts.append(_layer_norm(va[:, sl], g[:, sl], b[:, sl]))
    return jnp.concatenate(parts, axis=-1)


def _proj_prompt_body(x_ref, w_ref, ag_ref, ab_ref, ws_ref, bs_ref,
                      q_ref, k_ref, v_ref, gm_ref, xb_ref, u_ref):
    j = pl.program_id(1)

    @pl.when(j == 0)
    def _():
        xb_ref[...] = x_ref[...].astype(BF16)

    p = _dot(xb_ref[...], w_ref[...])

    @pl.when(j == 0)
    def _():
        q_ref[...] = (p * (ATTN_SCALE * LOG2E)).astype(BF16)

    @pl.when(j == 1)
    def _():
        k_ref[...] = p

    @pl.when(j == 2)
    def _():
        v_ref[...] = p

    @pl.when(j == 3)
    def _():
        u_ref[...] = p

    @pl.when(j == 4)
    def _():
        van = _head_layer_norm(p, ag_ref[...], ab_ref[...]).astype(BF16)
        row = lax.broadcasted_iota(jnp.int32, (CHUNK, CHUNK), 0)
        col = lax.broadcasted_iota(jnp.int32, (CHUNK, CHUNK), 1)
        causal = row >= col
        n_chunks = p.shape[0] // CHUNK
        for h in range(N_HEADS):
            sl = slice(h * HEAD_DIM, (h + 1) * HEAD_DIM)
            w_h = jnp.where(causal, ws_ref[h], 0.0).astype(BF16)
            for c in range(n_chunks):
                rs = slice(c * CHUNK, (c + 1) * CHUNK)
                mixed = _dot(w_h, van[rs, sl]) + bs_ref[:, sl]
                gm_ref[rs, sl] = (u_ref[rs, sl] * mixed).astype(BF16)


def _proj_prompt(x, w_in, ag, ab, ws, bs_exp, *, tm):
    rows = x.shape[0]
    row_blk = lambda i, j: (i, 0)
    const2 = lambda i, j: (0, 0)
    return pl.pallas_call(
        _proj_prompt_body,
        grid=(rows // tm, 5),
        in_specs=[
            pl.BlockSpec((tm, D_MODEL), row_blk),
            pl.BlockSpec((D_MODEL, WIDTH), lambda i, j: (0, j)),
            pl.BlockSpec((1, WIDTH), const2),
            pl.BlockSpec((1, WIDTH), const2),
            pl.BlockSpec((N_HEADS, CHUNK, CHUNK), lambda i, j: (0, 0, 0)),
            pl.BlockSpec((CHUNK, WIDTH), const2),
        ],
        out_specs=[
            pl.BlockSpec((tm, WIDTH), row_blk),
            pl.BlockSpec((tm, WIDTH), row_blk),
            pl.BlockSpec((tm, WIDTH), row_blk),
            pl.BlockSpec((tm, WIDTH), row_blk),
        ],
        out_shape=[
            jax.ShapeDtypeStruct((rows, WIDTH), BF16),
            jax.ShapeDtypeStruct((rows, WIDTH), F32),
            jax.ShapeDtypeStruct((rows, WIDTH), F32),
            jax.ShapeDtypeStruct((rows, WIDTH), BF16),
        ],
        scratch_shapes=[pltpu.VMEM((tm, D_MODEL), BF16), pltpu.VMEM((tm, WIDTH), F32)],
        compiler_params=pltpu.CompilerParams(
            dimension_semantics=("parallel", "arbitrary"), vmem_limit_bytes=VMEM_LIMIT),
        name="proj_prompt",
    )(x, w_in, ag, ab, ws, bs_exp)


def _proj_sample_body(x_ref, w_ref, ag_ref, ab_ref, w00_ref, b0_ref,
                      q_ref, k_ref, v_ref, gm_ref, van_ref, wb_ref, u_ref):
    j = pl.program_id(0)
    p, w_hi = _dot_split(x_ref[...], w_ref[...])
    wb_ref[...] = w_hi

    @pl.when(j == 0)
    def _():
        q_ref[...] = p

    @pl.when(j == 1)
    def _():
        k_ref[...] = p

    @pl.when(j == 2)
    def _():
        v_ref[...] = p

    @pl.when(j == 3)
    def _():
        u_ref[...] = p

    @pl.when(j == 4)
    def _():
        van = _head_layer_norm(p, ag_ref[...], ab_ref[...])
        van_ref[...] = van
        gm_ref[...] = u_ref[...] * (w00_ref[...] * van + b0_ref[...])


def _proj_sample(x, w_in, ag, ab, w00, b0):
    rows = x.shape[0]
    whole = lambda j: (0, 0)
    w_cols = lambda j: (0, j)
    return pl.pallas_call(
        _proj_sample_body,
        grid=(5,),
        in_specs=[
            pl.BlockSpec((rows, D_MODEL), whole),
            pl.BlockSpec((D_MODEL, WIDTH), w_cols),
            pl.BlockSpec((1, WIDTH), whole),
            pl.BlockSpec((1, WIDTH), whole),
            pl.BlockSpec((1, WIDTH), whole),
            pl.BlockSpec((1, WIDTH), whole),
        ],
        out_specs=[pl.BlockSpec((rows, WIDTH), whole)] * 5 + [pl.BlockSpec((D_MODEL, WIDTH), w_cols)],
        out_shape=[
            jax.ShapeDtypeStruct((rows, WIDTH), F32),
            jax.ShapeDtypeStruct((rows, WIDTH), F32),
            jax.ShapeDtypeStruct((rows, WIDTH), F32),
            jax.ShapeDtypeStruct((rows, WIDTH), F32),
            jax.ShapeDtypeStruct((rows, WIDTH), F32),
            jax.ShapeDtypeStruct(w_in.shape, BF16),
        ],
        scratch_shapes=[pltpu.VMEM((rows, WIDTH), F32)],
        compiler_params=pltpu.CompilerParams(
            dimension_semantics=("arbitrary",), vmem_limit_bytes=VMEM_LIMIT),
        name="proj_sample",
    )(x, w_in, ag, ab, w00, b0)


def _out_ln_body(a_ref, m_ref, h_ref, wa_ref, wm_ref, g_ref, b_ref, o_ref):
    mix = _dot(a_ref[...], wa_ref[...]) + _dot(m_ref[...], wm_ref[...])
    o_ref[...] = _layer_norm(DN_ALPHA * h_ref[...] + mix, g_ref[...], b_ref[...])


def _out_ln(attn, gm, h, w_out, g, b, *, tm):
    rows = h.shape[0]
    row_blk = lambda i: (i, 0)
    const = lambda i: (0, 0)
    return pl.pallas_call(
        _out_ln_body,
        grid=(rows // tm,),
        in_specs=[
            pl.BlockSpec((tm, WIDTH), row_blk),
            pl.BlockSpec((tm, WIDTH), row_blk),
            pl.BlockSpec((tm, D_MODEL), row_blk),
            pl.BlockSpec((WIDTH, D_MODEL), const),
            pl.BlockSpec((WIDTH, D_MODEL), lambda i: (1, 0)),
            pl.BlockSpec((1, D_MODEL), const),
            pl.BlockSpec((1, D_MODEL), const),
        ],
        out_specs=pl.BlockSpec((tm, D_MODEL), row_blk),
        out_shape=jax.ShapeDtypeStruct((rows, D_MODEL), F32),
        compiler_params=pltpu.CompilerParams(
            dimension_semantics=("parallel",), vmem_limit_bytes=VMEM_LIMIT),
        name="out_prompt",
    )(attn, gm, h, w_out, w_out, g, b)


def _out_ln_sample_body(x_ref, h_ref, w_ref, g_ref, b_ref, o_ref, wb_ref, acc_ref):
    j = pl.program_id(0)

    @pl.when(j == 0)
    def _():
        acc_ref[...] = jnp.zeros_like(acc_ref)

    part, wb_ref[...] = _dot_cast(x_ref[...], w_ref[...])
    acc_ref[...] += part

    @pl.when(j == pl.num_programs(0) - 1)
    def _():
        o_ref[...] = _layer_norm(DN_ALPHA * h_ref[...] + acc_ref[...], g_ref[...], b_ref[...])


def _out_ln_sample(mix, h, w_out, g, b, *, tk):
    rows = h.shape[0]
    whole = lambda j: (0, 0)
    slab = pl.BlockSpec((tk, D_MODEL), lambda j: (j, 0))
    return pl.pallas_call(
        _out_ln_sample_body,
        grid=(w_out.shape[0] // tk,),
        in_specs=[pl.BlockSpec((rows, tk), lambda j: (0, j)), pl.BlockSpec((rows, D_MODEL), whole), slab,
                  pl.BlockSpec((1, D_MODEL), whole), pl.BlockSpec((1, D_MODEL), whole)],
        out_specs=[pl.BlockSpec((rows, D_MODEL), whole), slab],
        out_shape=[jax.ShapeDtypeStruct((rows, D_MODEL), F32), jax.ShapeDtypeStruct(w_out.shape, BF16)],
        scratch_shapes=[pltpu.VMEM((rows, D_MODEL), F32)],
        compiler_params=pltpu.CompilerParams(
            dimension_semantics=("arbitrary",), vmem_limit_bytes=VMEM_LIMIT),
        name="out_sample",
    )(mix, h, w_out, g, b)


CHUNK_BLOCKS = 4
KEY_CHUNK = CHUNK_BLOCKS * MOBA_BLOCK
N_SEL_ROWS = 32
N_POS_ROWS = 16
SPLIT = 3


def _top_blocks_t(gate, n_valid):
    blk = lax.broadcasted_iota(jnp.int32, gate.shape, 0).astype(F32)
    valid = blk < n_valid
    g = jnp.where(valid, gate, -jnp.inf)
    sel = jnp.zeros(gate.shape, dtype=jnp.bool_)
    for _ in range(MOBA_TOPK):
        mx = jnp.max(g, axis=0, keepdims=True)
        first = jnp.min(jnp.where(g == mx, blk, float(gate.shape[0])), axis=0, keepdims=True)
        hit = blk == first
        sel = sel | (hit & valid)
        g = jnp.where(hit, -jnp.inf, g)
    return sel


def _attn_prompt_body(slope_ref, q_ref, k_ref, v_ref, o_ref,
                      ka_ref, vt_ref, km_ref, pos_ref, qa_ref, m_ref, l_ref, acc_ref,
                      u_ref, p_ref, mx_ref, alpha_ref):
    h = pl.program_id(0)
    i = pl.program_id(1)
    n_blocks = k_ref.shape[0] // MOBA_BLOCK
    slope2 = slope_ref[h] * LOG2E

    @pl.when(i == 0)
    def _():
        km_ref[...] = jnp.zeros_like(km_ref)
        lane = lax.broadcasted_iota(jnp.int32, (MOBA_BLOCK, LANES), 1)
        offs = lax.broadcasted_iota(jnp.int32, (MOBA_BLOCK, LANES), 0).astype(F32)

        def prep(n, carry):
            rows = pl.ds(pl.multiple_of(n * MOBA_BLOCK, MOBA_BLOCK), MOBA_BLOCK)
            kblk = k_ref[rows, :]
            in_chunk = lax.convert_element_type(n % CHUNK_BLOCKS, F32)
            aug = jnp.where(lane == n, 1.0, 0.0)
            aug = jnp.where((lane >= N_SEL_ROWS) & (lane < N_SEL_ROWS + SPLIT), in_chunk, aug)
            aug = jnp.where((lane >= N_SEL_ROWS + SPLIT) & (lane < N_SEL_ROWS + 2 * SPLIT), offs, aug)
            ka_ref[rows, :HEAD_DIM] = kblk.astype(BF16)
            ka_ref[rows, HEAD_DIM:] = aug.astype(BF16)
            vt_ref[n] = v_ref[rows, :].T.astype(BF16)
            km_ref[pl.ds(n, 1), :] = jnp.mean(kblk, axis=0, keepdims=True)
            return carry

        lax.fori_loop(0, n_blocks, prep, 0)
        r = lax.broadcasted_iota(jnp.int32, pos_ref.shape, 0)
        rest = jnp.full(pos_ref.shape, slope2, F32)
        rows = jnp.zeros(pos_ref.shape, F32)
        for t in range(SPLIT):
            term = rest.astype(BF16).astype(F32)
            rest = rest - term
            rows = jnp.where(r == t, term * MOBA_BLOCK, jnp.where(r == t + SPLIT, term, rows))
        pos_ref[...] = rows.astype(BF16)

    qt = q_ref[...].astype(F32).T.astype(BF16)
    gate = _dot(km_ref[...].astype(BF16), qt)
    own = i * CHUNK_BLOCKS + lax.broadcasted_iota(jnp.int32, (1, KEY_CHUNK), 1) // MOBA_BLOCK
    sel = _top_blocks_t(gate, own.astype(F32))
    blk = lax.broadcasted_iota(jnp.int32, gate.shape, 0)
    qa_ref[:HEAD_DIM, :] = qt
    qa_ref[HEAD_DIM:HEAD_DIM + N_SEL_ROWS, :] = jnp.where(sel | (blk == own), 0.0, MASKED_BF16).astype(BF16)
    qa_ref[HEAD_DIM + N_SEL_ROWS:HEAD_DIM + N_SEL_ROWS + N_POS_ROWS, :] = jnp.concatenate(
        [pos_ref[...]] * CHUNK_BLOCKS, axis=1)
    qa_ref[HEAD_DIM + N_SEL_ROWS + N_POS_ROWS:, :] = jnp.zeros(
        (HEAD_DIM - N_SEL_ROWS - N_POS_ROWS, KEY_CHUNK), BF16)

    half = KEY_CHUNK // 2
    key = lax.broadcasted_iota(jnp.int32, (half, MOBA_BLOCK), 0)
    qry = lax.broadcasted_iota(jnp.int32, (half, MOBA_BLOCK), 1)

    def stage_logits(c, j, diagonal):
        cols = slice(j * MOBA_BLOCK, (j + 1) * MOBA_BLOCK)
        mx = None
        for s in range(2):
            rows = pl.ds(pl.multiple_of(c * KEY_CHUNK + s * half, half), half)
            u = _dot(ka_ref[rows, :], qa_ref[:, cols])
            if diagonal:
                u = jnp.where(key + s * half <= qry + j * MOBA_BLOCK, u, MASKED)
            u_ref[j, s * half:(s + 1) * half, :] = u
            hmax = jnp.max(u, axis=0, keepdims=True)
            mx = hmax if mx is None else jnp.maximum(mx, hmax)
        mx_ref[:, cols] = mx

    def stage_softmax(c, j, diagonal):
        cols = slice(j * MOBA_BLOCK, (j + 1) * MOBA_BLOCK)
        blocks_ahead = (c - i) * CHUNK_BLOCKS - j
        off = slope2 * lax.convert_element_type(blocks_ahead * MOBA_BLOCK, F32)
        m_new = mx_ref[:, cols] + off
        if not diagonal:
            m_old = m_ref[:, cols]
            m_new = jnp.maximum(m_old, m_new)
            alpha = jnp.exp2(m_old - m_new)
            alpha_ref[:, cols] = alpha
        p = jnp.exp2(u_ref[j] - (m_new - off))
        p_ref[j] = p.astype(BF16)
        psum = jnp.sum(p, axis=0, keepdims=True)
        l_ref[:, cols] = psum if diagonal else alpha * l_ref[:, cols] + psum
        m_ref[:, cols] = m_new

    def stage_values(c, j, diagonal):
        cols = slice(j * MOBA_BLOCK, (j + 1) * MOBA_BLOCK)
        parts = []
        for s in range(2):
            pv = None
            for n in range(s * CHUNK_BLOCKS // 2, (s + 1) * CHUNK_BLOCKS // 2):
                d = _dot(vt_ref[c * CHUNK_BLOCKS + n], p_ref[j, n * MOBA_BLOCK:(n + 1) * MOBA_BLOCK, :])
                pv = d if pv is None else pv + d
            parts.append(pv)
        pv = parts[0] + parts[1]
        acc_ref[:, cols] = pv if diagonal else alpha_ref[:, cols] * acc_ref[:, cols] + pv

    def sweep(c, diagonal):
        stage_logits(c, 0, diagonal)
        for j in range(CHUNK_BLOCKS):
            if j + 1 < CHUNK_BLOCKS:
                stage_logits(c, j + 1, diagonal)
            stage_softmax(c, j, diagonal)
            stage_values(c, j, diagonal)

    sweep(i, True)

    def past(c, carry):
        sweep(c, False)
        return carry

    lax.fori_loop(0, i, past, 0)
    o_ref[...] = (acc_ref[...] / l_ref[...]).T.astype(o_ref.dtype)


def _attn_prompt(slopes, q, k, v):
    t_len = q.shape[0]
    n_blocks = t_len // MOBA_BLOCK
    assert n_blocks <= N_SEL_ROWS and t_len % KEY_CHUNK == 0
    grid_spec = pltpu.PrefetchScalarGridSpec(
        num_scalar_prefetch=1,
        grid=(N_HEADS, t_len // KEY_CHUNK),
        in_specs=[
            pl.BlockSpec((KEY_CHUNK, HEAD_DIM), lambda h, i, s: (i, h)),
            pl.BlockSpec((t_len, HEAD_DIM), lambda h, i, s: (0, h)),
            pl.BlockSpec((t_len, HEAD_DIM), lambda h, i, s: (0, h)),
        ],
        out_specs=pl.BlockSpec((KEY_CHUNK, HEAD_DIM), lambda h, i, s: (i, h)),
        scratch_shapes=[
            pltpu.VMEM((t_len, 2 * HEAD_DIM), BF16),
            pltpu.VMEM((n_blocks, HEAD_DIM, MOBA_BLOCK), BF16),
            pltpu.VMEM((N_SEL_ROWS, HEAD_DIM), F32),
            pltpu.VMEM((N_POS_ROWS, MOBA_BLOCK), BF16),
            pltpu.VMEM((2 * HEAD_DIM, KEY_CHUNK), BF16),
            pltpu.VMEM((1, KEY_CHUNK), F32),
            pltpu.VMEM((1, KEY_CHUNK), F32),
            pltpu.VMEM((HEAD_DIM, KEY_CHUNK), F32),
            pltpu.VMEM((CHUNK_BLOCKS, KEY_CHUNK, MOBA_BLOCK), F32),
            pltpu.VMEM((CHUNK_BLOCKS, KEY_CHUNK, MOBA_BLOCK), BF16),
            pltpu.VMEM((1, KEY_CHUNK), F32),
            pltpu.VMEM((1, KEY_CHUNK), F32),
        ],
    )
    return pl.pallas_call(
        _attn_prompt_body,
        grid_spec=grid_spec,
        out_shape=jax.ShapeDtypeStruct((t_len, WIDTH), BF16),
        compiler_params=pltpu.CompilerParams(
            dimension_semantics=("parallel", "arbitrary"), vmem_limit_bytes=VMEM_LIMIT),
        name="attn_prompt",
    )(slopes, q, k, v)


GROUP_PAGES = 12
GROUP_BLOCKS = GROUP_PAGES // PAGES_PER_BLOCK


class _SelectStep:
    def __init__(self, t, pt_ref, q_ref, ck_ref, sel_ref, buf_ref, sem_ref, ksum_ref):
        self.t, self.pt_ref, self.q_ref, self.ck_ref = t, pt_ref, q_ref, ck_ref
        self.sel_ref, self.buf_ref, self.sem_ref, self.ksum_ref = sel_ref, buf_ref, sem_ref, ksum_ref
        self.n_seq, self.n_pages = pt_ref.shape
        self.total_pages = self.n_seq * self.n_pages
        self.total_blocks = self.total_pages // PAGES_PER_BLOCK
        self.n_steps = -(-self.total_pages // GROUP_PAGES)
        self.active = t < self.n_steps
        self.slot = t % 2

    def _page_copy(self, flat, slot, p):
        flat = jnp.minimum(flat, self.total_pages - 1)
        page = self.pt_ref[flat // self.n_pages, flat % self.n_pages]
        return pltpu.make_async_copy(
            self.ck_ref.at[page, :, 0], self.buf_ref.at[slot, pl.ds(p * PAGE_SIZE, PAGE_SIZE)],
            self.sem_ref.at[slot])

    def _step_copies(self, t, slot):
        return [self._page_copy(t * GROUP_PAGES + p, slot, p) for p in range(GROUP_PAGES)]

    def copies(self):
        @pl.when(self.t == 0)
        def _():
            for c in self._step_copies(0, 0):
                c.start()

        @pl.when(self.t + 1 < self.n_steps)
        def _():
            for c in self._step_copies(self.t + 1, 1 - self.slot):
                c.start()

        @pl.when(self.active)
        def _():
            for c in self._step_copies(self.t, self.slot):
                c.wait()

    def block_sums(self):
        for n in range(GROUP_BLOCKS):
            blk = self.buf_ref[self.slot, pl.ds(n * MOBA_BLOCK, MOBA_BLOCK)]
            flat = self.t * GROUP_BLOCKS + n
            self.ksum_ref[jnp.where(flat < self.total_blocks, flat, self.total_blocks + n)] = jnp.sum(blk, axis=0)

    def pick(self):
        n_full = self.n_pages // PAGES_PER_BLOCK

        def pick_seq(b, carry):
            ksum = self.ksum_ref[pl.ds(pl.multiple_of(b * n_full, n_full), n_full)]
            kmean = ksum / MOBA_BLOCK
            gate = jnp.sum(kmean * self.q_ref[b], axis=-1, keepdims=True)
            blk_id = lax.broadcasted_iota(jnp.int32, gate.shape, 0).astype(F32)
            lane = lax.broadcasted_iota(jnp.int32, (N_HEADS, LANES), 1)
            picks = jnp.zeros((N_HEADS, LANES), F32)
            for j in range(MOBA_TOPK):
                mx = jnp.max(gate, axis=0, keepdims=True)
                first = jnp.min(jnp.where(gate == mx, blk_id, float(n_full)), axis=0, keepdims=True)
                picks = jnp.where(lane == j, first[0], picks)
                gate = jnp.where(blk_id == first, -jnp.inf, gate)
            self.sel_ref[b] = picks.astype(jnp.int32)
            return carry

        @pl.when(self.t == self.n_steps - 1)
        def _():
            lax.fori_loop(0, self.n_seq, pick_seq, 0)


def _ffn_ln_select_body(pt_ref, x_ref, wg_ref, wu_ref, wd_ref, g_ref, b_ref, q_ref, ck_ref,
                        o_ref, sel_ref, xb_ref, buf_ref, sem_ref, ksum_ref):
    t = pl.program_id(0) * pl.num_programs(1) + pl.program_id(1)
    step = _SelectStep(t, pt_ref, q_ref, ck_ref, sel_ref, buf_ref, sem_ref, ksum_ref)
    step.copies()
    _ffn_ln_body(x_ref, wg_ref, wu_ref, wd_ref, g_ref, b_ref, o_ref, xb_ref, side_work=step.block_sums)
    step.pick()


def _ffn_ln_select(x, wg, wu, wd, g, b, page_table, q_sample, cache_k, *, tm, tf):
    rows = x.shape[0]
    n_seq, n_pages = page_table.shape
    total_blocks = n_seq * n_pages // PAGES_PER_BLOCK
    grid = (rows // tm, D_FF // tf)
    assert n_pages % PAGES_PER_BLOCK == 0 and grid[0] * grid[1] * GROUP_PAGES >= n_seq * n_pages
    once = pl.Buffered(1)
    grid_spec = pltpu.PrefetchScalarGridSpec(
        num_scalar_prefetch=1,
        grid=grid,
        in_specs=[
            pl.BlockSpec((tm, D_MODEL), lambda i, j, pt: (i, 0), pipeline_mode=once),
            pl.BlockSpec((D_MODEL, tf), lambda i, j, pt: (0, j)),
            pl.BlockSpec((D_MODEL, tf), lambda i, j, pt: (0, j)),
            pl.BlockSpec((tf, D_MODEL), lambda i, j, pt: (j, 0)),
            pl.BlockSpec((1, D_MODEL), lambda i, j, pt: (0, 0)),
            pl.BlockSpec((1, D_MODEL), lambda i, j, pt: (0, 0)),
            pl.BlockSpec((n_seq, N_HEADS, HEAD_DIM), lambda i, j, pt: (0, 0, 0)),
            pl.BlockSpec(memory_space=pl.ANY),
        ],
        out_specs=[
            pl.BlockSpec((tm, D_MODEL), lambda i, j, pt: (i, 0), pipeline_mode=once),
            pl.BlockSpec((n_seq, N_HEADS, LANES), lambda i, j, pt: (0, 0, 0)),
        ],
        scratch_shapes=[
            pltpu.VMEM((tm, D_MODEL), BF16),
            pltpu.VMEM((2, GROUP_PAGES * PAGE_SIZE, N_HEADS, HEAD_DIM), F32),
            pltpu.SemaphoreType.DMA((2,)),
            pltpu.VMEM((total_blocks + GROUP_BLOCKS, N_HEADS, HEAD_DIM), F32),
        ],
    )
    return pl.pallas_call(
        _ffn_ln_select_body,
        grid_spec=grid_spec,
        out_shape=[jax.ShapeDtypeStruct((rows, D_MODEL), F32),
                   jax.ShapeDtypeStruct((n_seq, N_HEADS, LANES), jnp.int32)],
        compiler_params=pltpu.CompilerParams(
            dimension_semantics=("arbitrary", "arbitrary"), vmem_limit_bytes=VMEM_LIMIT),
        name="ffn1_prompt_select",
    )(page_table, x, wg, wu, wd, g, b, q_sample.reshape(n_seq, N_HEADS, HEAD_DIM), cache_k)


def _attn_sample_body(pt_ref, sel_ref, slope_ref, q_ref, kn_ref, vn_ref, ck_ref, cv_ref, o_ref,
                      kbuf_ref, vbuf_ref, sem_ref, *, past_len):
    b = pl.program_id(0)
    n_sel = MOBA_TOPK
    slot = b % 2

    def gathers(b, slot):
        out = []
        for h in range(N_HEADS):
            for j in range(n_sel):
                for pp in range(PAGES_PER_BLOCK):
                    page = pt_ref[b, sel_ref[b, h * MOBA_TOPK + j] * PAGES_PER_BLOCK + pp]
                    rows = pl.ds((j * PAGES_PER_BLOCK + pp) * PAGE_SIZE, PAGE_SIZE)
                    out.append(pltpu.make_async_copy(
                        ck_ref.at[page, :, 0, h, :], kbuf_ref.at[slot, h, rows, :], sem_ref.at[slot, 0]))
                    out.append(pltpu.make_async_copy(
                        cv_ref.at[page, :, 0, h, :], vbuf_ref.at[slot, h, rows, :], sem_ref.at[slot, 1]))
        return out

    @pl.when(b == 0)
    def _():
        for c in gathers(0, 0):
            c.start()

    @pl.when(b + 1 < pl.num_programs(0))
    def _():
        for c in gathers(b + 1, 1 - slot):
            c.start()

    for c in gathers(b, slot):
        c.wait()

    n_keys = n_sel * MOBA_BLOCK
    lane = lax.broadcasted_iota(jnp.int32, (1, n_keys), 1)
    which = lane // MOBA_BLOCK
    offs = lane % MOBA_BLOCK
    outs = []
    for h in range(N_HEADS):
        sl = slice(h * HEAD_DIM, (h + 1) * HEAD_DIM)
        slope = slope_ref[h]
        qh = q_ref[:, sl]
        kn = kn_ref[:, sl]
        vn = vn_ref[:, sl]
        q8 = jnp.broadcast_to(qh, (8, HEAD_DIM)).astype(BF16)
        s = _dot_nt(q8, kbuf_ref[slot, h].astype(BF16))[:1] * ATTN_SCALE
        blk = jnp.zeros((1, n_keys), jnp.int32)
        for j in range(n_sel):
            blk = jnp.where(which == j, sel_ref[b, h * MOBA_TOPK + j], blk)
        dist = (past_len - (blk * MOBA_BLOCK + offs)).astype(F32)
        s = s - slope * dist
        s_own = jnp.sum(qh * kn, axis=-1, keepdims=True) * ATTN_SCALE
        mx = jnp.maximum(jnp.max(s, axis=-1, keepdims=True), s_own)
        p = jnp.exp(s - mx)
        p_own = jnp.exp(s_own - mx)
        denom = jnp.sum(p, axis=-1, keepdims=True) + p_own
        p8 = jnp.broadcast_to(p, (8, n_keys)).astype(BF16)
        pv = _dot(p8, vbuf_ref[slot, h].astype(BF16))[:1]
        outs.append((pv + p_own * vn) / denom)
    o_ref[...] = jnp.concatenate(outs, axis=-1)


def _attn_sample(page_table, sel, slopes, q, k_new, v_new, cache_k, cache_v):
    n_seq, n_pages = page_table.shape
    per_seq = pl.BlockSpec((None, 1, WIDTH), lambda b, *_: (b, 0, 0))
    grid_spec = pltpu.PrefetchScalarGridSpec(
        num_scalar_prefetch=3,
        grid=(n_seq,),
        in_specs=[
            per_seq, per_seq, per_seq,
            pl.BlockSpec(memory_space=pl.ANY),
            pl.BlockSpec(memory_space=pl.ANY),
        ],
        out_specs=per_seq,
        scratch_shapes=[
            pltpu.VMEM((2, N_HEADS, MOBA_TOPK * MOBA_BLOCK, HEAD_DIM), F32),
            pltpu.VMEM((2, N_HEADS, MOBA_TOPK * MOBA_BLOCK, HEAD_DIM), F32),
            pltpu.SemaphoreType.DMA((2, 2)),
        ],
    )
    return pl.pallas_call(
        functools.partial(_attn_sample_body, past_len=n_pages * PAGE_SIZE),
        grid_spec=grid_spec,
        out_shape=jax.ShapeDtypeStruct((n_seq, 1, WIDTH), F32),
        compiler_params=pltpu.CompilerParams(
            dimension_semantics=("arbitrary",), vmem_limit_bytes=VMEM_LIMIT),
        name="attn_sample",
    )(page_table, sel, slopes, q.reshape(n_seq, 1, WIDTH), k_new.reshape(n_seq, 1, WIDTH),
      v_new.reshape(n_seq, 1, WIDTH), cache_k, cache_v).reshape(n_seq, WIDTH)


def kernel(x_prompt, x_sample, cache_k, cache_v, page_table, ffn1_w_gate, ffn1_w_up, ffn1_w_down, ln1_g, ln1_b, w_in, a_norm_g, a_norm_b, a_spatial_w, a_spatial_b, w_out, ln2_g, ln2_b, ffn2_w_gate, ffn2_w_up, ffn2_w_down, ln3_g, ln3_b):
    assert x_prompt.shape[0] == 1 and x_sample.shape[1] == 1 and w_in.shape[0] == DEPTH == 1
    t_len = x_prompt.shape[1]
    n_seq = x_sample.shape[0]
    xp = x_prompt.reshape(t_len, D_MODEL)
    xs = x_sample.reshape(n_seq, D_MODEL)
    slopes = jnp.exp2(-8.0 * jnp.arange(1, N_HEADS + 1, dtype=F32) / N_HEADS)

    row = lambda v: v.reshape(1, -1)
    ln1, ln2, ln3 = ((row(g[0]), row(b[0])) for g, b in ((ln1_g, ln1_b), (ln2_g, ln2_b), (ln3_g, ln3_b)))
    ag, ab = row(a_norm_g[0]), row(a_norm_b[0])
    ws, bs = a_spatial_w[0], a_spatial_b[0]
    bs_exp = jnp.repeat(bs.T, HEAD_DIM, axis=1)
    w00 = row(jnp.repeat(ws[:, 0, 0], HEAD_DIM))
    b0 = row(jnp.repeat(bs[:, 0], HEAD_DIM))

    hs, w1g, w1u, w1d = _ffn_ln_sample(xs, ffn1_w_gate[0], ffn1_w_up[0], ffn1_w_down[0], *ln1,
                                       tf=256, accurate=True, name="ffn1_sample")
    qs, ks, vs, gm_s, van_s, w_in_b = _proj_sample(hs, w_in[0], ag, ab, w00, b0)
    hp, sel = _ffn_ln_select(xp, w1g, w1u, w1d, *ln1, page_table, qs, cache_k, tm=1024, tf=256)
    qp, kp, vp, gm_p = _proj_prompt(hp, w_in_b, ag, ab, ws, bs_exp, tm=512)

    sel = sel[:, :, :MOBA_TOPK].reshape(n_seq, N_HEADS * MOBA_TOPK)
    attn_s = _attn_sample(page_table, sel, slopes, qs, ks, vs, cache_k, cache_v)
    attn_p = _attn_prompt(slopes, qp, kp, vp)

    hs, w_out_b = _out_ln_sample(jnp.concatenate([attn_s, gm_s], axis=-1), hs, w_out[0], *ln2, tk=512)
    hp = _out_ln(attn_p, gm_p, hp, w_out_b, *ln2, tm=512)

    ys, w2g, w2u, w2d = _ffn_ln_sample(hs, ffn2_w_gate[0], ffn2_w_up[0], ffn2_w_down[0], *ln3,
                                       tf=256, accurate=False, name="ffn2_sample")
    yp = _ffn_ln(hp, w2g, w2u, w2d, *ln3, tm=512, tf=512, name="ffn2_prompt")

    heads = (1, N_HEADS, HEAD_DIM)
    return (yp.reshape(1, t_len, D_MODEL), ys.reshape(n_seq, 1, D_MODEL),
            kp.reshape((1, t_len) + heads), vp.reshape((1, t_len) + heads),
            ks.reshape((n_seq, 1) + heads), vs.reshape((n_seq, 1) + heads),
            van_s.reshape((n_seq, 1) + heads))
```

```python
import functools

import jax
import jax.numpy as jnp
from jax import lax
from jax.experimental import pallas as pl
from jax.experimental.pallas import tpu as pltpu

D_MODEL = 2048
HEAD_DIM = 128
N_HEADS = 8
WIDTH = N_HEADS * HEAD_DIM
CHUNK = 128
MOBA_BLOCK = 256
MOBA_TOPK = 3
PAGE_SIZE = 128
PAGES_PER_BLOCK = MOBA_BLOCK // PAGE_SIZE
D_FF = 5632
LN_EPS = 1e-5
DEPTH = 1
DN_ALPHA = (2 * DEPTH) ** 0.25
ATTN_SCALE = HEAD_DIM ** -0.5
LOG2E = 1.4426950408889634

MASKED = -1e30
MASKED_BF16 = -(2.0 ** 100)

LANES = 128
VMEM_LIMIT = 56 * 1024 * 1024

BF16 = jnp.bfloat16
F32 = jnp.float32


def _layer_norm(x, g, b):
    mu = jnp.mean(x, axis=-1, keepdims=True)
    xc = x - mu
    var = jnp.mean(xc * xc, axis=-1, keepdims=True)
    return xc * lax.rsqrt(var + LN_EPS) * g + b


def _dot(a, b):
    return jnp.dot(a, b, preferred_element_type=F32)


def _dot_nt(a, b):
    return lax.dot_general(a, b, (((1,), (1,)), ((), ())), preferred_element_type=F32)


def _ffn_ln_body(x_ref, wg_ref, wu_ref, wd_ref, g_ref, b_ref, o_ref, xb_ref, side_work=None):
    j = pl.program_id(1)

    @pl.when(j == 0)
    def _():
        xb_ref[...] = x_ref[...].astype(BF16)
        o_ref[...] = jnp.zeros_like(o_ref)

    side = side_work if side_work is not None else (lambda part: None)
    xb = xb_ref[...]
    side(0)
    gate = _dot(xb, wg_ref[...])
    side(1)
    up = _dot(xb, wu_ref[...])
    act = gate * jax.nn.sigmoid(gate) * up
    side(2)
    o_ref[...] += _dot(act.astype(BF16), wd_ref[...])

    @pl.when(j == pl.num_programs(1) - 1)
    def _():
        y = DN_ALPHA * x_ref[...] + 0.5 * o_ref[...]
        o_ref[...] = _layer_norm(y, g_ref[...], b_ref[...])


def _ffn_ln(x, wg, wu, wd, g, b, *, tm, tf, name):
    rows = x.shape[0]
    return pl.pallas_call(
        _ffn_ln_body,
        grid=(rows // tm, D_FF // tf),
        in_specs=[
            pl.BlockSpec((tm, D_MODEL), lambda i, j: (i, 0)),
            pl.BlockSpec((D_MODEL, tf), lambda i, j: (0, j)),
            pl.BlockSpec((D_MODEL, tf), lambda i, j: (0, j)),
            pl.BlockSpec((tf, D_MODEL), lambda i, j: (j, 0)),
            pl.BlockSpec((1, D_MODEL), lambda i, j: (0, 0)),
            pl.BlockSpec((1, D_MODEL), lambda i, j: (0, 0)),
        ],
        out_specs=pl.BlockSpec((tm, D_MODEL), lambda i, j: (i, 0)),
        out_shape=jax.ShapeDtypeStruct((rows, D_MODEL), F32),
        scratch_shapes=[pltpu.VMEM((tm, D_MODEL), BF16)],
        compiler_params=pltpu.CompilerParams(
            dimension_semantics=("parallel", "arbitrary"), vmem_limit_bytes=VMEM_LIMIT),
        name=name,
    )(x, wg, wu, wd, g, b)


def _split(x):
    hi = x.astype(BF16)
    return hi, (x - hi.astype(F32)).astype(BF16)


def _dot_split(x, w):
    x_hi, x_lo = _split(x)
    w_hi, w_lo = _split(w)
    n = x.shape[0]
    both = _dot(jnp.concatenate([x_hi, x_lo], axis=0), w_hi)
    return both[:n] + (both[n:] + _dot(x_hi, w_lo)), w_hi


def _dot_cast(x, w):
    w_hi = w.astype(BF16)
    return _dot(x.astype(BF16), w_hi), w_hi


def _ffn_ln_sample_body(x_ref, wg_ref, wu_ref, wd_ref, g_ref, b_ref,
                        o_ref, wgb_ref, wub_ref, wdb_ref, acc_ref, *, accurate):
    j = pl.program_id(0)
    dot = _dot_split if accurate else _dot_cast

    @pl.when(j == 0)
    def _():
        acc_ref[...] = jnp.zeros_like(acc_ref)

    x = x_ref[...]
    gate, wgb_ref[...] = dot(x, wg_ref[...])
    up, wub_ref[...] = dot(x, wu_ref[...])
    down, wdb_ref[...] = dot(gate * jax.nn.sigmoid(gate) * up, wd_ref[...])
    acc_ref[...] += down

    @pl.when(j == pl.num_programs(0) - 1)
    def _():
        y = DN_ALPHA * x + 0.5 * acc_ref[...]
        o_ref[...] = _layer_norm(y, g_ref[...], b_ref[...])


def _ffn_ln_sample(x, wg, wu, wd, g, b, *, tf, accurate, name):
    rows = x.shape[0]
    whole = lambda j: (0, 0)
    cols = pl.BlockSpec((D_MODEL, tf), lambda j: (0, j))
    rws = pl.BlockSpec((tf, D_MODEL), lambda j: (j, 0))
    return pl.pallas_call(
        functools.partial(_ffn_ln_sample_body, accurate=accurate),
        grid=(D_FF // tf,),
        in_specs=[pl.BlockSpec((rows, D_MODEL), whole), cols, cols, rws,
                  pl.BlockSpec((1, D_MODEL), whole), pl.BlockSpec((1, D_MODEL), whole)],
        out_specs=[pl.BlockSpec((rows, D_MODEL), whole), cols, cols, rws],
        out_shape=[jax.ShapeDtypeStruct((rows, D_MODEL), F32),
                   jax.ShapeDtypeStruct(wg.shape, BF16),
                   jax.ShapeDtypeStruct(wu.shape, BF16),
                   jax.ShapeDtypeStruct(wd.shape, BF16)],
        scratch_shapes=[pltpu.VMEM((rows, D_MODEL), F32)],
        compiler_params=pltpu.CompilerParams(
            dimension_semantics=("arbitrary",), vmem_limit_bytes=VMEM_LIMIT),
        name=name,
    )(x, wg, wu, wd, g, b)


def _head_layer_norm(va, g, b):
    parts = []
    for h in range(N_HEADS):
        sl = slice(h * HEAD_DIM, (h + 1) * HEAD_DIM)
        parts.append(_layer_norm(va[:, sl], g[:, sl], b[:, sl]))
    return jnp.concatenate(parts, axis=-1)


def _proj_prompt_body(x_ref, w_ref, ag_ref, ab_ref, ws_ref, bs_ref,
                      q_ref, k_ref, v_ref, gm_ref, xb_ref, u_ref):
    j = pl.program_id(1)

    @pl.when(j == 0)
    def _():
        xb_ref[...] = x_ref[...].astype(BF16)

    p = _dot(xb_ref[...], w_ref[...])

    @pl.when(j == 0)
    def _():
        q_ref[...] = (p * (ATTN_SCALE * LOG2E)).astype(BF16)

    @pl.when(j == 1)
    def _():
        k_ref[...] = p

    @pl.when(j == 2)
    def _():
        v_ref[...] = p

    @pl.when(j == 3)
    def _():
        u_ref[...] = p

    @pl.when(j == 4)
    def _():
        van = _head_layer_norm(p, ag_ref[...], ab_ref[...]).astype(BF16)
        row = lax.broadcasted_iota(jnp.int32, (CHUNK, CHUNK), 0)
        col = lax.broadcasted_iota(jnp.int32, (CHUNK, CHUNK), 1)
        causal = row >= col
        n_chunks = p.shape[0] // CHUNK
        for h in range(N_HEADS):
            sl = slice(h * HEAD_DIM, (h + 1) * HEAD_DIM)
            w_h = jnp.where(causal, ws_ref[h], 0.0).astype(BF16)
            for c in range(n_chunks):
                rs = slice(c * CHUNK, (c + 1) * CHUNK)
                mixed = _dot(w_h, van[rs, sl]) + bs_ref[:, sl]
                gm_ref[rs, sl] = (u_ref[rs, sl] * mixed).astype(BF16)


def _proj_prompt(x, w_in, ag, ab, ws, bs_exp, *, tm):
    rows = x.shape[0]
    row_blk = lambda i, j: (i, 0)
    const2 = lambda i, j: (0, 0)
    return pl.pallas_call(
        _proj_prompt_body,
        grid=(rows // tm, 5),
        in_specs=[
            pl.BlockSpec((tm, D_MODEL), row_blk),
            pl.BlockSpec((D_MODEL, WIDTH), lambda i, j: (0, j)),
            pl.BlockSpec((1, WIDTH), const2),
            pl.BlockSpec((1, WIDTH), const2),
            pl.BlockSpec((N_HEADS, CHUNK, CHUNK), lambda i, j: (0, 0, 0)),
            pl.BlockSpec((CHUNK, WIDTH), const2),
        ],
        out_specs=[
            pl.BlockSpec((tm, WIDTH), row_blk),
            pl.BlockSpec((tm, WIDTH), row_blk),
            pl.BlockSpec((tm, WIDTH), row_blk),
            pl.BlockSpec((tm, WIDTH), row_blk),
        ],
        out_shape=[
            jax.ShapeDtypeStruct((rows, WIDTH), BF16),
            jax.ShapeDtypeStruct((rows, WIDTH), F32),
            jax.ShapeDtypeStruct((rows, WIDTH), F32),
            jax.ShapeDtypeStruct((rows, WIDTH), BF16),
        ],
        scratch_shapes=[pltpu.VMEM((tm, D_MODEL), BF16), pltpu.VMEM((tm, WIDTH), F32)],
        compiler_params=pltpu.CompilerParams(
            dimension_semantics=("parallel", "arbitrary"), vmem_limit_bytes=VMEM_LIMIT),
        name="proj_prompt",
    )(x, w_in, ag, ab, ws, bs_exp)


def _proj_sample_body(x_ref, w_ref, ag_ref, ab_ref, w00_ref, b0_ref,
                      q_ref, k_ref, v_ref, gm_ref, van_ref, wb_ref, u_ref):
    j = pl.program_id(0)
    p, w_hi = _dot_split(x_ref[...], w_ref[...])
    wb_ref[...] = w_hi

    @pl.when(j == 0)
    def _():
        q_ref[...] = p

    @pl.when(j == 1)
    def _():
        k_ref[...] = p

    @pl.when(j == 2)
    def _():
        v_ref[...] = p

    @pl.when(j == 3)
    def _():
        u_ref[...] = p

    @pl.when(j == 4)
    def _():
        van = _head_layer_norm(p, ag_ref[...], ab_ref[...])
        van_ref[...] = van
        gm_ref[...] = u_ref[...] * (w00_ref[...] * van + b0_ref[...])


def _proj_sample(x, w_in, ag, ab, w00, b0):
    rows = x.shape[0]
    whole = lambda j: (0, 0)
    w_cols = lambda j: (0, j)
    return pl.pallas_call(
        _proj_sample_body,
        grid=(5,),
        in_specs=[
            pl.BlockSpec((rows, D_MODEL), whole),
            pl.BlockSpec((D_MODEL, WIDTH), w_cols),
            pl.BlockSpec((1, WIDTH), whole),
            pl.BlockSpec((1, WIDTH), whole),
            pl.BlockSpec((1, WIDTH), whole),
            pl.BlockSpec((1, WIDTH), whole),
        ],
        out_specs=[pl.BlockSpec((rows, WIDTH), whole)] * 5 + [pl.BlockSpec((D_MODEL, WIDTH), w_cols)],
        out_shape=[
            jax.ShapeDtypeStruct((rows, WIDTH), F32),
            jax.ShapeDtypeStruct((rows, WIDTH), F32),
            jax.ShapeDtypeStruct((rows, WIDTH), F32),
            jax.ShapeDtypeStruct((rows, WIDTH), F32),
            jax.ShapeDtypeStruct((rows, WIDTH), F32),
            jax.ShapeDtypeStruct(w_in.shape, BF16),
        ],
        scratch_shapes=[pltpu.VMEM((rows, WIDTH), F32)],
        compiler_params=pltpu.CompilerParams(
            dimension_semantics=("arbitrary",), vmem_limit_bytes=VMEM_LIMIT),
        name="proj_sample",
    )(x, w_in, ag, ab, w00, b0)


def _out_ln_body(a_ref, m_ref, h_ref, wa_ref, wm_ref, g_ref, b_ref, o_ref):
    mix = _dot(a_ref[...], wa_ref[...]) + _dot(m_ref[...], wm_ref[...])
    o_ref[...] = _layer_norm(DN_ALPHA * h_ref[...] + mix, g_ref[...], b_ref[...])


def _out_ln(attn, gm, h, w_out, g, b, *, tm):
    rows = h.shape[0]
    row_blk = lambda i: (i, 0)
    const = lambda i: (0, 0)
    return pl.pallas_call(
        _out_ln_body,
        grid=(rows // tm,),
        in_specs=[
            pl.BlockSpec((tm, WIDTH), row_blk),
            pl.BlockSpec((tm, WIDTH), row_blk),
            pl.BlockSpec((tm, D_MODEL), row_blk),
            pl.BlockSpec((WIDTH, D_MODEL), const),
            pl.BlockSpec((WIDTH, D_MODEL), lambda i: (1, 0)),
            pl.BlockSpec((1, D_MODEL), const),
            pl.BlockSpec((1, D_MODEL), const),
        ],
        out_specs=pl.BlockSpec((tm, D_MODEL), row_blk),
        out_shape=jax.ShapeDtypeStruct((rows, D_MODEL), F32),
        compiler_params=pltpu.CompilerParams(
            dimension_semantics=("parallel",), vmem_limit_bytes=VMEM_LIMIT),
        name="out_prompt",
    )(attn, gm, h, w_out, w_out, g, b)


def _out_ln_sample_body(x_ref, h_ref, w_ref, g_ref, b_ref, o_ref, wb_ref, acc_ref):
    j = pl.program_id(0)

    @pl.when(j == 0)
    def _():
        acc_ref[...] = jnp.zeros_like(acc_ref)

    part, wb_ref[...] = _dot_cast(x_ref[...], w_ref[...])
    acc_ref[...] += part

    @pl.when(j == pl.num_programs(0) - 1)
    def _():
        o_ref[...] = _layer_norm(DN_ALPHA * h_ref[...] + acc_ref[...], g_ref[...], b_ref[...])


def _out_ln_sample(mix, h, w_out, g, b, *, tk):
    rows = h.shape[0]
    whole = lambda j: (0, 0)
    slab = pl.BlockSpec((tk, D_MODEL), lambda j: (j, 0))
    return pl.pallas_call(
        _out_ln_sample_body,
        grid=(w_out.shape[0] // tk,),
        in_specs=[pl.BlockSpec((rows, tk), lambda j: (0, j)), pl.BlockSpec((rows, D_MODEL), whole), slab,
                  pl.BlockSpec((1, D_MODEL), whole), pl.BlockSpec((1, D_MODEL), whole)],
        out_specs=[pl.BlockSpec((rows, D_MODEL), whole), slab],
        out_shape=[jax.ShapeDtypeStruct((rows, D_MODEL), F32), jax.ShapeDtypeStruct(w_out.shape, BF16)],
        scratch_shapes=[pltpu.VMEM((rows, D_MODEL), F32)],
        compiler_params=pltpu.CompilerParams(
            dimension_semantics=("arbitrary",), vmem_limit_bytes=VMEM_LIMIT),
        name="out_sample",
    )(mix, h, w_out, g, b)


CHUNK_BLOCKS = 4
KEY_CHUNK = CHUNK_BLOCKS * MOBA_BLOCK
N_SEL_ROWS = 32
N_POS_ROWS = 16
SPLIT = 3


def _top_blocks_t(gate, n_valid):
    blk = lax.broadcasted_iota(jnp.int32, gate.shape, 0).astype(F32)
    valid = blk < n_valid
    g = jnp.where(valid, gate, -jnp.inf)
    sel = jnp.zeros(gate.shape, dtype=jnp.bool_)
    for _ in range(MOBA_TOPK):
        mx = jnp.max(g, axis=0, keepdims=True)
        first = jnp.min(jnp.where(g == mx, blk, float(gate.shape[0])), axis=0, keepdims=True)
        hit = blk == first
        sel = sel | (hit & valid)
        g = jnp.where(hit, -jnp.inf, g)
    return sel


def _attn_prompt_body(slope_ref, q_ref, k_ref, v_ref, o_ref,
                      ka_ref, vt_ref, km_ref, pos_ref, qa_ref, m_ref, l_ref, acc_ref,
                      u_ref, p_ref, mx_ref, alpha_ref):
    h = pl.program_id(0)
    i = pl.program_id(1)
    n_blocks = k_ref.shape[0] // MOBA_BLOCK
    slope2 = slope_ref[h] * LOG2E

    @pl.when(i == 0)
    def _():
        km_ref[...] = jnp.zeros_like(km_ref)
        lane = lax.broadcasted_iota(jnp.int32, (MOBA_BLOCK, LANES), 1)
        offs = lax.broadcasted_iota(jnp.int32, (MOBA_BLOCK, LANES), 0).astype(F32)

        def prep(n, carry):
            rows = pl.ds(pl.multiple_of(n * MOBA_BLOCK, MOBA_BLOCK), MOBA_BLOCK)
            kblk = k_ref[rows, :]
            in_chunk = lax.convert_element_type(n % CHUNK_BLOCKS, F32)
            aug = jnp.where(lane == n, 1.0, 0.0)
            aug = jnp.where((lane >= N_SEL_ROWS) & (lane < N_SEL_ROWS + SPLIT), in_chunk, aug)
            aug = jnp.where((lane >= N_SEL_ROWS + SPLIT) & (lane < N_SEL_ROWS + 2 * SPLIT), offs, aug)
            ka_ref[rows, :HEAD_DIM] = kblk.astype(BF16)
            ka_ref[rows, HEAD_DIM:] = aug.astype(BF16)
            vt_ref[n] = v_ref[rows, :].T.astype(BF16)
            km_ref[pl.ds(n, 1), :] = jnp.mean(kblk, axis=0, keepdims=True)
            return carry

        lax.fori_loop(0, n_blocks, prep, 0)
        r = lax.broadcasted_iota(jnp.int32, pos_ref.shape, 0)
        rest = jnp.full(pos_ref.shape, slope2, F32)
        rows = jnp.zeros(pos_ref.shape, F32)
        for t in range(SPLIT):
            term = rest.astype(BF16).astype(F32)
            rest = rest - term
            rows = jnp.where(r == t, term * MOBA_BLOCK, jnp.where(r == t + SPLIT, term, rows))
        pos_ref[...] = rows.astype(BF16)

    qt = q_ref[...].astype(F32).T.astype(BF16)
    gate = _dot(km_ref[...].astype(BF16), qt)
    own = i * CHUNK_BLOCKS + lax.broadcasted_iota(jnp.int32, (1, KEY_CHUNK), 1) // MOBA_BLOCK
    sel = _top_blocks_t(gate, own.astype(F32))
    blk = lax.broadcasted_iota(jnp.int32, gate.shape, 0)
    qa_ref[:HEAD_DIM, :] = qt
    qa_ref[HEAD_DIM:HEAD_DIM + N_SEL_ROWS, :] = jnp.where(sel | (blk == own), 0.0, MASKED_BF16).astype(BF16)
    qa_ref[HEAD_DIM + N_SEL_ROWS:HEAD_DIM + N_SEL_ROWS + N_POS_ROWS, :] = jnp.concatenate(
        [pos_ref[...]] * CHUNK_BLOCKS, axis=1)
    qa_ref[HEAD_DIM + N_SEL_ROWS + N_POS_ROWS:, :] = jnp.zeros(
        (HEAD_DIM - N_SEL_ROWS - N_POS_ROWS, KEY_CHUNK), BF16)

    half = KEY_CHUNK // 2
    key = lax.broadcasted_iota(jnp.int32, (half, MOBA_BLOCK), 0)
    qry = lax.broadcasted_iota(jnp.int32, (half, MOBA_BLOCK), 1)

    def stage_logits(c, j, diagonal):
        cols = slice(j * MOBA_BLOCK, (j + 1) * MOBA_BLOCK)
        mx = None
        for s in range(2):
            rows = pl.ds(pl.multiple_of(c * KEY_CHUNK + s * half, half), half)
            u = _dot(ka_ref[rows, :], qa_ref[:, cols])
            if diagonal:
                u = jnp.where(key + s * half <= qry + j * MOBA_BLOCK, u, MASKED)
            u_ref[j, s * half:(s + 1) * half, :] = u
            hmax = jnp.max(u, axis=0, keepdims=True)
            mx = hmax if mx is None else jnp.maximum(mx, hmax)
        mx_ref[:, cols] = mx

    def stage_softmax(c, j, diagonal):
        cols = slice(j * MOBA_BLOCK, (j + 1) * MOBA_BLOCK)
        blocks_ahead = (c - i) * CHUNK_BLOCKS - j
        off = slope2 * lax.convert_element_type(blocks_ahead * MOBA_BLOCK, F32)
        m_new = mx_ref[:, cols] + off
        if not diagonal:
            m_old = m_ref[:, cols]
            m_new = jnp.maximum(m_old, m_new)
            alpha = jnp.exp2(m_old - m_new)
            alpha_ref[:, cols] = alpha
        p = jnp.exp2(u_ref[j] - (m_new - off))
        p_ref[j] = p.astype(BF16)
        psum = jnp.sum(p, axis=0, keepdims=True)
        l_ref[:, cols] = psum if diagonal else alpha * l_ref[:, cols] + psum
        m_ref[:, cols] = m_new

    def stage_values(c, j, diagonal):
        cols = slice(j * MOBA_BLOCK, (j + 1) * MOBA_BLOCK)
        parts = []
        for s in range(2):
            pv = None
            for n in range(s * CHUNK_BLOCKS // 2, (s + 1) * CHUNK_BLOCKS // 2):
                d = _dot(vt_ref[c * CHUNK_BLOCKS + n], p_ref[j, n * MOBA_BLOCK:(n + 1) * MOBA_BLOCK, :])
                pv = d if pv is None else pv + d
            parts.append(pv)
        pv = parts[0] + parts[1]
        acc_ref[:, cols] = pv if diagonal else alpha_ref[:, cols] * acc_ref[:, cols] + pv

    def sweep(c, diagonal):
        stage_logits(c, 0, diagonal)
        for j in range(CHUNK_BLOCKS):
            if j + 1 < CHUNK_BLOCKS:
                stage_logits(c, j + 1, diagonal)
            stage_softmax(c, j, diagonal)
            stage_values(c, j, diagonal)

    sweep(i, True)

    def past(c, carry):
        sweep(c, False)
        return carry

    lax.fori_loop(0, i, past, 0)
    o_ref[...] = (acc_ref[...] / l_ref[...]).T.astype(o_ref.dtype)


def _attn_prompt(slopes, q, k, v):
    t_len = q.shape[0]
    n_blocks = t_len // MOBA_BLOCK
    assert n_blocks <= N_SEL_ROWS and t_len % KEY_CHUNK == 0
    grid_spec = pltpu.PrefetchScalarGridSpec(
        num_scalar_prefetch=1,
        grid=(N_HEADS, t_len // KEY_CHUNK),
        in_specs=[
            pl.BlockSpec((KEY_CHUNK, HEAD_DIM), lambda h, i, s: (i, h)),
            pl.BlockSpec((t_len, HEAD_DIM), lambda h, i, s: (0, h)),
            pl.BlockSpec((t_len, HEAD_DIM), lambda h, i, s: (0, h)),
        ],
        out_specs=pl.BlockSpec((KEY_CHUNK, HEAD_DIM), lambda h, i, s: (i, h)),
        scratch_shapes=[
            pltpu.VMEM((t_len, 2 * HEAD_DIM), BF16),
            pltpu.VMEM((n_blocks, HEAD_DIM, MOBA_BLOCK), BF16),
            pltpu.VMEM((N_SEL_ROWS, HEAD_DIM), F32),
            pltpu.VMEM((N_POS_ROWS, MOBA_BLOCK), BF16),
            pltpu.VMEM((2 * HEAD_DIM, KEY_CHUNK), BF16),
            pltpu.VMEM((1, KEY_CHUNK), F32),
            pltpu.VMEM((1, KEY_CHUNK), F32),
            pltpu.VMEM((HEAD_DIM, KEY_CHUNK), F32),
            pltpu.VMEM((CHUNK_BLOCKS, KEY_CHUNK, MOBA_BLOCK), F32),
            pltpu.VMEM((CHUNK_BLOCKS, KEY_CHUNK, MOBA_BLOCK), BF16),
            pltpu.VMEM((1, KEY_CHUNK), F32),
            pltpu.VMEM((1, KEY_CHUNK), F32),
        ],
    )
    return pl.pallas_call(
        _attn_prompt_body,
        grid_spec=grid_spec,
        out_shape=jax.ShapeDtypeStruct((t_len, WIDTH), BF16),
        compiler_params=pltpu.CompilerParams(
            dimension_semantics=("parallel", "arbitrary"), vmem_limit_bytes=VMEM_LIMIT),
        name="attn_prompt",
    )(slopes, q, k, v)


GROUP_PAGES = 12
GROUP_BLOCKS = GROUP_PAGES // PAGES_PER_BLOCK


class _SelectStep:
    def __init__(self, t, pt_ref, q_ref, ck_ref, sel_ref, buf_ref, sem_ref, stage_ref, ksum_ref):
        self.t, self.pt_ref, self.q_ref, self.ck_ref = t, pt_ref, q_ref, ck_ref
        self.sel_ref, self.buf_ref, self.sem_ref = sel_ref, buf_ref, sem_ref
        self.stage_ref, self.ksum_ref = stage_ref, ksum_ref
        self.n_seq, self.n_pages = pt_ref.shape
        self.total_pages = self.n_seq * self.n_pages
        self.total_blocks = self.total_pages // PAGES_PER_BLOCK
        self.n_steps = -(-self.total_pages // GROUP_PAGES)
        self.active = t < self.n_steps
        self.slot = t % 2

    def _page_copy(self, flat, slot, p):
        flat = jnp.minimum(flat, self.total_pages - 1)
        page = self.pt_ref[flat // self.n_pages, flat % self.n_pages]
        return pltpu.make_async_copy(
            self.ck_ref.at[page, :, 0], self.buf_ref.at[slot, pl.ds(p * PAGE_SIZE, PAGE_SIZE)],
            self.sem_ref.at[slot])

    def _step_copies(self, t, slot):
        return [self._page_copy(t * GROUP_PAGES + p, slot, p) for p in range(GROUP_PAGES)]

    def copies(self):
        @pl.when(self.t == 0)
        def _():
            for c in self._step_copies(0, 0):
                c.start()

        @pl.when(self.t + 1 < self.n_steps)
        def _():
            for c in self._step_copies(self.t + 1, 1 - self.slot):
                c.start()

        @pl.when(self.active)
        def _():
            for c in self._step_copies(self.t, self.slot):
                c.wait()

    def block_sums(self, part):
        for n in range(part * GROUP_BLOCKS // 3, (part + 1) * GROUP_BLOCKS // 3):
            blk = self.buf_ref[self.slot, pl.ds(n * MOBA_BLOCK, MOBA_BLOCK)]
            self.stage_ref[n] = jnp.sum(blk, axis=0)

    def pick(self):
        n_full = self.n_pages // PAGES_PER_BLOCK
        for n in range(GROUP_BLOCKS):
            flat = self.t * GROUP_BLOCKS + n
            self.ksum_ref[jnp.where(flat < self.total_blocks, flat, self.total_blocks + n)] = self.stage_ref[n]

        def pick_seq(b, carry):
            ksum = self.ksum_ref[pl.ds(pl.multiple_of(b * n_full, n_full), n_full)]
            kmean = ksum / MOBA_BLOCK
            gate = jnp.sum(kmean * self.q_ref[b], axis=-1, keepdims=True)
            blk_id = lax.broadcasted_iota(jnp.int32, gate.shape, 0).astype(F32)
            lane = lax.broadcasted_iota(jnp.int32, (N_HEADS, LANES), 1)
            picks = jnp.zeros((N_HEADS, LANES), F32)
            for j in range(MOBA_TOPK):
                mx = jnp.max(gate, axis=0, keepdims=True)
                first = jnp.min(jnp.where(gate == mx, blk_id, float(n_full)), axis=0, keepdims=True)
                picks = jnp.where(lane == j, first[0], picks)
                gate = jnp.where(blk_id == first, -jnp.inf, gate)
            self.sel_ref[b] = picks.astype(jnp.int32)
            return carry

        @pl.when(self.t == self.n_steps - 1)
        def _():
            lax.fori_loop(0, self.n_seq, pick_seq, 0)


def _ffn_ln_select_body(pt_ref, x_ref, wg_ref, wu_ref, wd_ref, g_ref, b_ref, q_ref, ck_ref,
                        o_ref, sel_ref, xb_ref, buf_ref, sem_ref, stage_ref, ksum_ref):
    t = pl.program_id(0) * pl.num_programs(1) + pl.program_id(1)
    step = _SelectStep(t, pt_ref, q_ref, ck_ref, sel_ref, buf_ref, sem_ref, stage_ref, ksum_ref)
    step.copies()
    _ffn_ln_body(x_ref, wg_ref, wu_ref, wd_ref, g_ref, b_ref, o_ref, xb_ref, side_work=step.block_sums)
    step.pick()


def _ffn_ln_select(x, wg, wu, wd, g, b, page_table, q_sample, cache_k, *, tm, tf):
    rows = x.shape[0]
    n_seq, n_pages = page_table.shape
    total_blocks = n_seq * n_pages // PAGES_PER_BLOCK
    grid = (rows // tm, D_FF // tf)
    assert n_pages % PAGES_PER_BLOCK == 0 and grid[0] * grid[1] * GROUP_PAGES >= n_seq * n_pages
    once = pl.Buffered(1)
    grid_spec = pltpu.PrefetchScalarGridSpec(
        num_scalar_prefetch=1,
        grid=grid,
        in_specs=[
            pl.BlockSpec((tm, D_MODEL), lambda i, j, pt: (i, 0), pipeline_mode=once),
            pl.BlockSpec((D_MODEL, tf), lambda i, j, pt: (0, j)),
            pl.BlockSpec((D_MODEL, tf), lambda i, j, pt: (0, j)),
            pl.BlockSpec((tf, D_MODEL), lambda i, j, pt: (j, 0)),
            pl.BlockSpec((1, D_MODEL), lambda i, j, pt: (0, 0)),
            pl.BlockSpec((1, D_MODEL), lambda i, j, pt: (0, 0)),
            pl.BlockSpec((n_seq, N_HEADS, HEAD_DIM), lambda i, j, pt: (0, 0, 0)),
            pl.BlockSpec(memory_space=pl.ANY),
        ],
        out_specs=[
            pl.BlockSpec((tm, D_MODEL), lambda i, j, pt: (i, 0), pipeline_mode=once),
            pl.BlockSpec((n_seq, N_HEADS, LANES), lambda i, j, pt: (0, 0, 0)),
        ],
        scratch_shapes=[
            pltpu.VMEM((tm, D_MODEL), BF16),
            pltpu.VMEM((2, GROUP_PAGES * PAGE_SIZE, N_HEADS, HEAD_DIM), F32),
            pltpu.SemaphoreType.DMA((2,)),
            pltpu.VMEM((GROUP_BLOCKS, N_HEADS, HEAD_DIM), F32),
            pltpu.VMEM((total_blocks + GROUP_BLOCKS, N_HEADS, HEAD_DIM), F32),
        ],
    )
    return pl.pallas_call(
        _ffn_ln_select_body,
        grid_spec=grid_spec,
        out_shape=[jax.ShapeDtypeStruct((rows, D_MODEL), F32),
                   jax.ShapeDtypeStruct((n_seq, N_HEADS, LANES), jnp.int32)],
        compiler_params=pltpu.CompilerParams(
            dimension_semantics=("arbitrary", "arbitrary"), vmem_limit_bytes=VMEM_LIMIT),
        name="ffn1_prompt_select",
    )(page_table, x, wg, wu, wd, g, b, q_sample.reshape(n_seq, N_HEADS, HEAD_DIM), cache_k)


def _attn_sample_body(pt_ref, sel_ref, slope_ref, q_ref, kn_ref, vn_ref, ck_ref, cv_ref, o_ref,
                      kbuf_ref, vbuf_ref, sem_ref, *, past_len):
    b = pl.program_id(0)
    n_sel = MOBA_TOPK
    slot = b % 2

    def gathers(b, slot):
        out = []
        for h in range(N_HEADS):
            for j in range(n_sel):
                for pp in range(PAGES_PER_BLOCK):
                    page = pt_ref[b, sel_ref[b, h * MOBA_TOPK + j] * PAGES_PER_BLOCK + pp]
                    rows = pl.ds((j * PAGES_PER_BLOCK + pp) * PAGE_SIZE, PAGE_SIZE)
                    out.append(pltpu.make_async_copy(
                        ck_ref.at[page, :, 0, h, :], kbuf_ref.at[slot, h, rows, :], sem_ref.at[slot, 0]))
                    out.append(pltpu.make_async_copy(
                        cv_ref.at[page, :, 0, h, :], vbuf_ref.at[slot, h, rows, :], sem_ref.at[slot, 1]))
        return out

    @pl.when(b == 0)
    def _():
        for c in gathers(0, 0):
            c.start()

    @pl.when(b + 1 < pl.num_programs(0))
    def _():
        for c in gathers(b + 1, 1 - slot):
            c.start()

    for c in gathers(b, slot):
        c.wait()

    n_keys = n_sel * MOBA_BLOCK
    lane = lax.broadcasted_iota(jnp.int32, (1, n_keys), 1)
    which = lane // MOBA_BLOCK
    offs = lane % MOBA_BLOCK
    outs = []
    for h in range(N_HEADS):
        sl = slice(h * HEAD_DIM, (h + 1) * HEAD_DIM)
        slope = slope_ref[h]
        qh = q_ref[:, sl]
        kn = kn_ref[:, sl]
        vn = vn_ref[:, sl]
        q8 = jnp.broadcast_to(qh, (8, HEAD_DIM)).astype(BF16)
        s = _dot_nt(q8, kbuf_ref[slot, h].astype(BF16))[:1] * ATTN_SCALE
        blk = jnp.zeros((1, n_keys), jnp.int32)
        for j in range(n_sel):
            blk = jnp.where(which == j, sel_ref[b, h * MOBA_TOPK + j], blk)
        dist = (past_len - (blk * MOBA_BLOCK + offs)).astype(F32)
        s = s - slope * dist
        s_own = jnp.sum(qh * kn, axis=-1, keepdims=True) * ATTN_SCALE
        mx = jnp.maximum(jnp.max(s, axis=-1, keepdims=True), s_own)
        p = jnp.exp(s - mx)
        p_own = jnp.exp(s_own - mx)
        denom = jnp.sum(p, axis=-1, keepdims=True) + p_own
        p8 = jnp.broadcast_to(p, (8, n_keys)).astype(BF16)
        pv = _dot(p8, vbuf_ref[slot, h].astype(BF16))[:1]
        outs.append((pv + p_own * vn) / denom)
    o_ref[...] = jnp.concatenate(outs, axis=-1)


def _attn_sample(page_table, sel, slopes, q, k_new, v_new, cache_k, cache_v):
    n_seq, n_pages = page_table.shape
    per_seq = pl.BlockSpec((None, 1, WIDTH), lambda b, *_: (b, 0, 0))
    grid_spec = pltpu.PrefetchScalarGridSpec(
        num_scalar_prefetch=3,
        grid=(n_seq,),
        in_specs=[
            per_seq, per_seq, per_seq,
            pl.BlockSpec(memory_space=pl.ANY),
            pl.BlockSpec(memory_space=pl.ANY),
        ],
        out_specs=per_seq,
        scratch_shapes=[
            pltpu.VMEM((2, N_HEADS, MOBA_TOPK * MOBA_BLOCK, HEAD_DIM), F32),
            pltpu.VMEM((2, N_HEADS, MOBA_TOPK * MOBA_BLOCK, HEAD_DIM), F32),
            pltpu.SemaphoreType.DMA((2, 2)),
        ],
    )
    return pl.pallas_call(
        functools.partial(_attn_sample_body, past_len=n_pages * PAGE_SIZE),
        grid_spec=grid_spec,
        out_shape=jax.ShapeDtypeStruct((n_seq, 1, WIDTH), F32),
        compiler_params=pltpu.CompilerParams(
            dimension_semantics=("arbitrary",), vmem_limit_bytes=VMEM_LIMIT),
        name="attn_sample",
    )(page_table, sel, slopes, q.reshape(n_seq, 1, WIDTH), k_new.reshape(n_seq, 1, WIDTH),
      v_new.reshape(n_seq, 1, WIDTH), cache_k, cache_v).reshape(n_seq, WIDTH)


def kernel(x_prompt, x_sample, cache_k, cache_v, page_table, ffn1_w_gate, ffn1_w_up, ffn1_w_down, ln1_g, ln1_b, w_in, a_norm_g, a_norm_b, a_spatial_w, a_spatial_b, w_out, ln2_g, ln2_b, ffn2_w_gate, ffn2_w_up, ffn2_w_down, ln3_g, ln3_b):
    assert x_prompt.shape[0] == 1 and x_sample.shape[1] == 1 and w_in.shape[0] == DEPTH == 1
    t_len = x_prompt.shape[1]
    n_seq = x_sample.shape[0]
    xp = x_prompt.reshape(t_len, D_MODEL)
    xs = x_sample.reshape(n_seq, D_MODEL)
    slopes = jnp.exp2(-8.0 * jnp.arange(1, N_HEADS + 1, dtype=F32) / N_HEADS)

    row = lambda v: v.reshape(1, -1)
    ln1, ln2, ln3 = ((row(g[0]), row(b[0])) for g, b in ((ln1_g, ln1_b), (ln2_g, ln2_b), (ln3_g, ln3_b)))
    ag, ab = row(a_norm_g[0]), row(a_norm_b[0])
    ws, bs = a_spatial_w[0], a_spatial_b[0]
    bs_exp = jnp.repeat(bs.T, HEAD_DIM, axis=1)
    w00 = row(jnp.repeat(ws[:, 0, 0], HEAD_DIM))
    b0 = row(jnp.repeat(bs[:, 0], HEAD_DIM))

    hs, w1g, w1u, w1d = _ffn_ln_sample(xs, ffn1_w_gate[0], ffn1_w_up[0], ffn1_w_down[0], *ln1,
                                       tf=256, accurate=True, name="ffn1_sample")
    qs, ks, vs, gm_s, van_s, w_in_b = _proj_sample(hs, w_in[0], ag, ab, w00, b0)
    hp, sel = _ffn_ln_select(xp, w1g, w1u, w1d, *ln1, page_table, qs, cache_k, tm=1024, tf=256)
    qp, kp, vp, gm_p = _proj_prompt(hp, w_in_b, ag, ab, ws, bs_exp, tm=512)

    sel = sel[:, :, :MOBA_TOPK].reshape(n_seq, N_HEADS * MOBA_TOPK)
    attn_s = _attn_sample(page_table, sel, slopes, qs, ks, vs, cache_k, cache_v)
    attn_p = _attn_prompt(slopes, qp, kp, vp)

    hs, w_out_b = _out_ln_sample(jnp.concatenate([attn_s, gm_s], axis=-1), hs, w_out[0], *ln2, tk=512)
    hp = _out_ln(attn_p, gm_p, hp, w_out_b, *ln2, tm=512)

    ys, w2g, w2u, w2d = _ffn_ln_sample(hs, ffn2_w_gate[0], ffn2_w_up[0], ffn2_w_down[0], *ln3,
                                       tf=256, accurate=False, name="ffn2_sample")
    yp = _ffn_ln(hp, w2g, w2u, w2d, *ln3, tm=512, tf=512, name="ffn2_prompt")

    heads = (1, N_HEADS, HEAD_DIM)
    return (yp.reshape(1, t_len, D_MODEL), ys.reshape(n_seq, 1, D_MODEL),
            kp.reshape((1, t_len) + heads), vp.reshape((1, t_len) + heads),
            ks.reshape((n_seq, 1) + heads), vs.reshape((n_seq, 1) + heads),
            van_s.reshape((n_seq, 1) + heads))
```

```python
import functools

import jax
import jax.numpy as jnp
from jax import lax
from jax.experimental import pallas as pl
from jax.experimental.pallas import tpu as pltpu

D_MODEL = 2048
HEAD_DIM = 128
N_HEADS = 8
WIDTH = N_HEADS * HEAD_DIM
CHUNK = 128
MOBA_BLOCK = 256
MOBA_TOPK = 3
PAGE_SIZE = 128
PAGES_PER_BLOCK = MOBA_BLOCK // PAGE_SIZE
D_FF = 5632
LN_EPS = 1e-5
DEPTH = 1
DN_ALPHA = (2 * DEPTH) ** 0.25
ATTN_SCALE = HEAD_DIM ** -0.5
LOG2E = 1.4426950408889634

MASKED = -1e30
MASKED_BF16 = -(2.0 ** 100)

LANES = 128
VMEM_LIMIT = 56 * 1024 * 1024

BF16 = jnp.bfloat16
F32 = jnp.float32


def _layer_norm(x, g, b):
    mu = jnp.mean(x, axis=-1, keepdims=True)
    xc = x - mu
    var = jnp.mean(xc * xc, axis=-1, keepdims=True)
    return xc * lax.rsqrt(var + LN_EPS) * g + b


def _dot(a, b):
    return jnp.dot(a, b, preferred_element_type=F32)


def _dot_nt(a, b):
    return lax.dot_general(a, b, (((1,), (1,)), ((), ())), preferred_element_type=F32)


def _ffn_ln_body(x_ref, wg_ref, wu_ref, wd_ref, g_ref, b_ref, o_ref, xb_ref, side_work=None):
    j = pl.program_id(1)

    @pl.when(j == 0)
    def _():
        xb_ref[...] = x_ref[...].astype(BF16)
        o_ref[...] = jnp.zeros_like(o_ref)

    side = side_work if side_work is not None else (lambda part: None)
    xb = xb_ref[...]
    side(0)
    gate = _dot(xb, wg_ref[...])
    side(1)
    up = _dot(xb, wu_ref[...])
    act = gate * jax.nn.sigmoid(gate) * up
    side(2)
    o_ref[...] += _dot(act.astype(BF16), wd_ref[...])

    @pl.when(j == pl.num_programs(1) - 1)
    def _():
        y = DN_ALPHA * x_ref[...] + 0.5 * o_ref[...]
        o_ref[...] = _layer_norm(y, g_ref[...], b_ref[...])


def _ffn_ln(x, wg, wu, wd, g, b, *, tm, tf, name):
    rows = x.shape[0]
    return pl.pallas_call(
        _ffn_ln_body,
        grid=(rows // tm, D_FF // tf),
        in_specs=[
            pl.BlockSpec((tm, D_MODEL), lambda i, j: (i, 0)),
            pl.BlockSpec((D_MODEL, tf), lambda i, j: (0, j)),
            pl.BlockSpec((D_MODEL, tf), lambda i, j: (0, j)),
            pl.BlockSpec((tf, D_MODEL), lambda i, j: (j, 0)),
            pl.BlockSpec((1, D_MODEL), lambda i, j: (0, 0)),
            pl.BlockSpec((1, D_MODEL), lambda i, j: (0, 0)),
        ],
        out_specs=pl.BlockSpec((tm, D_MODEL), lambda i, j: (i, 0)),
        out_shape=jax.ShapeDtypeStruct((rows, D_MODEL), F32),
        scratch_shapes=[pltpu.VMEM((tm, D_MODEL), BF16)],
        compiler_params=pltpu.CompilerParams(
            dimension_semantics=("parallel", "arbitrary"), vmem_limit_bytes=VMEM_LIMIT),
        name=name,
    )(x, wg, wu, wd, g, b)


def _split(x):
    hi = x.astype(BF16)
    return hi, (x - hi.astype(F32)).astype(BF16)


def _dot_split(x, w):
    x_hi, x_lo = _split(x)
    w_hi, w_lo = _split(w)
    n = x.shape[0]
    both = _dot(jnp.concatenate([x_hi, x_lo], axis=0), w_hi)
    return both[:n] + (both[n:] + _dot(x_hi, w_lo)), w_hi


def _dot_cast(x, w):
    w_hi = w.astype(BF16)
    return _dot(x.astype(BF16), w_hi), w_hi


def _ffn_ln_sample_body(x_ref, wg_ref, wu_ref, wd_ref, g_ref, b_ref,
                        o_ref, wgb_ref, wub_ref, wdb_ref, acc_ref, *, accurate):
    j = pl.program_id(0)
    dot = _dot_split if accurate else _dot_cast

    @pl.when(j == 0)
    def _():
        acc_ref[...] = jnp.zeros_like(acc_ref)

    x = x_ref[...]
    gate, wgb_ref[...] = dot(x, wg_ref[...])
    up, wub_ref[...] = dot(x, wu_ref[...])
    down, wdb_ref[...] = dot(gate * jax.nn.sigmoid(gate) * up, wd_ref[...])
    acc_ref[...] += down

    @pl.when(j == pl.num_programs(0) - 1)
    def _():
        y = DN_ALPHA * x + 0.5 * acc_ref[...]
        o_ref[...] = _layer_norm(y, g_ref[...], b_ref[...])


def _ffn_ln_sample(x, wg, wu, wd, g, b, *, tf, accurate, name):
    rows = x.shape[0]
    whole = lambda j: (0, 0)
    cols = pl.BlockSpec((D_MODEL, tf), lambda j: (0, j))
    rws = pl.BlockSpec((tf, D_MODEL), lambda j: (j, 0))
    return pl.pallas_call(
        functools.partial(_ffn_ln_sample_body, accurate=accurate),
        grid=(D_FF // tf,),
        in_specs=[pl.BlockSpec((rows, D_MODEL), whole), cols, cols, rws,
                  pl.BlockSpec((1, D_MODEL), whole), pl.BlockSpec((1, D_MODEL), whole)],
        out_specs=[pl.BlockSpec((rows, D_MODEL), whole), cols, cols, rws],
        out_shape=[jax.ShapeDtypeStruct((rows, D_MODEL), F32),
                   jax.ShapeDtypeStruct(wg.shape, BF16),
                   jax.ShapeDtypeStruct(wu.shape, BF16),
                   jax.ShapeDtypeStruct(wd.shape, BF16)],
        scratch_shapes=[pltpu.VMEM((rows, D_MODEL), F32)],
        compiler_params=pltpu.CompilerParams(
            dimension_semantics=("arbitrary",), vmem_limit_bytes=VMEM_LIMIT),
        name=name,
    )(x, wg, wu, wd, g, b)


def _head_layer_norm(va, g, b):
    parts = []
    for h in range(N_HEADS):
        sl = slice(h * HEAD_DIM, (h + 1) * HEAD_DIM)
        parts.append(_layer_norm(va[:, sl], g[:, sl], b[:, sl]))
    return jnp.concatenate(parts, axis=-1)


def _proj_prompt_body(x_ref, w_ref, ag_ref, ab_ref, ws_ref, bs_ref,
                      q_ref, k_ref, v_ref, gm_ref, xb_ref, u_ref):
    j = pl.program_id(1)

    @pl.when(j == 0)
    def _():
        xb_ref[...] = x_ref[...].astype(BF16)

    p = _dot(xb_ref[...], w_ref[...])

    @pl.when(j == 0)
    def _():
        q_ref[...] = (p * (ATTN_SCALE * LOG2E)).astype(BF16)

    @pl.when(j == 1)
    def _():
        k_ref[...] = p

    @pl.when(j == 2)
    def _():
        v_ref[...] = p

    @pl.when(j == 3)
    def _():
        u_ref[...] = p

    @pl.when(j == 4)
    def _():
        van = _head_layer_norm(p, ag_ref[...], ab_ref[...]).astype(BF16)
        row = lax.broadcasted_iota(jnp.int32, (CHUNK, CHUNK), 0)
        col = lax.broadcasted_iota(jnp.int32, (CHUNK, CHUNK), 1)
        causal = row >= col
        n_chunks = p.shape[0] // CHUNK
        for h in range(N_HEADS):
            sl = slice(h * HEAD_DIM, (h + 1) * HEAD_DIM)
            w_h = jnp.where(causal, ws_ref[h], 0.0).astype(BF16)
            for c in range(n_chunks):
                rs = slice(c * CHUNK, (c + 1) * CHUNK)
                mixed = _dot(w_h, van[rs, sl]) + bs_ref[:, sl]
                gm_ref[rs, sl] = (u_ref[rs, sl] * mixed).astype(BF16)


def _proj_prompt(x, w_in, ag, ab, ws, bs_exp, *, tm):
    rows = x.shape[0]
    row_blk = lambda i, j: (i, 0)
    const2 = lambda i, j: (0, 0)
    return pl.pallas_call(
        _proj_prompt_body,
        grid=(rows // tm, 5),
        in_specs=[
            pl.BlockSpec((tm, D_MODEL), row_blk),
            pl.BlockSpec((D_MODEL, WIDTH), lambda i, j: (0, j)),
            pl.BlockSpec((1, WIDTH), const2),
            pl.BlockSpec((1, WIDTH), const2),
            pl.BlockSpec((N_HEADS, CHUNK, CHUNK), lambda i, j: (0, 0, 0)),
            pl.BlockSpec((CHUNK, WIDTH), const2),
        ],
        out_specs=[
            pl.BlockSpec((tm, WIDTH), row_blk),
            pl.BlockSpec((tm, WIDTH), row_blk),
            pl.BlockSpec((tm, WIDTH), row_blk),
            pl.BlockSpec((tm, WIDTH), row_blk),
        ],
        out_shape=[
            jax.ShapeDtypeStruct((rows, WIDTH), BF16),
            jax.ShapeDtypeStruct((rows, WIDTH), F32),
            jax.ShapeDtypeStruct((rows, WIDTH), F32),
            jax.ShapeDtypeStruct((rows, WIDTH), BF16),
        ],
        scratch_shapes=[pltpu.VMEM((tm, D_MODEL), BF16), pltpu.VMEM((tm, WIDTH), F32)],
        compiler_params=pltpu.CompilerParams(
            dimension_semantics=("parallel", "arbitrary"), vmem_limit_bytes=VMEM_LIMIT),
        name="proj_prompt",
    )(x, w_in, ag, ab, ws, bs_exp)


def _proj_sample_body(x_ref, w_ref, ag_ref, ab_ref, w00_ref, b0_ref,
                      q_ref, k_ref, v_ref, gm_ref, van_ref, wb_ref, u_ref):
    j = pl.program_id(0)
    p, w_hi = _dot_split(x_ref[...], w_ref[...])
    wb_ref[...] = w_hi

    @pl.when(j == 0)
    def _():
        q_ref[...] = p

    @pl.when(j == 1)
    def _():
        k_ref[...] = p

    @pl.when(j == 2)
    def _():
        v_ref[...] = p

    @pl.when(j == 3)
    def _():
        u_ref[...] = p

    @pl.when(j == 4)
    def _():
        van = _head_layer_norm(p, ag_ref[...], ab_ref[...])
        van_ref[...] = van
        gm_ref[...] = u_ref[...] * (w00_ref[...] * van + b0_ref[...])


def _proj_sample(x, w_in, ag, ab, w00, b0):
    rows = x.shape[0]
    whole = lambda j: (0, 0)
    w_cols = lambda j: (0, j)
    return pl.pallas_call(
        _proj_sample_body,
        grid=(5,),
        in_specs=[
            pl.BlockSpec((rows, D_MODEL), whole),
            pl.BlockSpec((D_MODEL, WIDTH), w_cols),
            pl.BlockSpec((1, WIDTH), whole),
            pl.BlockSpec((1, WIDTH), whole),
            pl.BlockSpec((1, WIDTH), whole),
            pl.BlockSpec((1, WIDTH), whole),
        ],
        out_specs=[pl.BlockSpec((rows, WIDTH), whole)] * 5 + [pl.BlockSpec((D_MODEL, WIDTH), w_cols)],
        out_shape=[
            jax.ShapeDtypeStruct((rows, WIDTH), F32),
            jax.ShapeDtypeStruct((rows, WIDTH), F32),
            jax.ShapeDtypeStruct((rows, WIDTH), F32),
            jax.ShapeDtypeStruct((rows, WIDTH), F32),
            jax.ShapeDtypeStruct((rows, WIDTH), F32),
            jax.ShapeDtypeStruct(w_in.shape, BF16),
        ],
        scratch_shapes=[pltpu.VMEM((rows, WIDTH), F32)],
        compiler_params=pltpu.CompilerParams(
            dimension_semantics=("arbitrary",), vmem_limit_bytes=VMEM_LIMIT),
        name="proj_sample",
    )(x, w_in, ag, ab, w00, b0)


def _out_ln_body(a_ref, m_ref, h_ref, wa_ref, wm_ref, g_ref, b_ref, o_ref):
    mix = _dot(a_ref[...], wa_ref[...]) + _dot(m_ref[...], wm_ref[...])
    o_ref[...] = _layer_norm(DN_ALPHA * h_ref[...] + mix, g_ref[...], b_ref[...])


def _out_ln(attn, gm, h, w_out, g, b, *, tm):
    rows = h.shape[0]
    row_blk = lambda i: (i, 0)
    const = lambda i: (0, 0)
    return pl.pallas_call(
        _out_ln_body,
        grid=(rows // tm,),
        in_specs=[
            pl.BlockSpec((tm, WIDTH), row_blk),
            pl.BlockSpec((tm, WIDTH), row_blk),
            pl.BlockSpec((tm, D_MODEL), row_blk),
            pl.BlockSpec((WIDTH, D_MODEL), const),
            pl.BlockSpec((WIDTH, D_MODEL), lambda i: (1, 0)),
            pl.BlockSpec((1, D_MODEL), const),
            pl.BlockSpec((1, D_MODEL), const),
        ],
        out_specs=pl.BlockSpec((tm, D_MODEL), row_blk),
        out_shape=jax.ShapeDtypeStruct((rows, D_MODEL), F32),
        compiler_params=pltpu.CompilerParams(
            dimension_semantics=("parallel",), vmem_limit_bytes=VMEM_LIMIT),
        name="out_prompt",
    )(attn, gm, h, w_out, w_out, g, b)


def _out_ln_sample_body(x_ref, h_ref, w_ref, g_ref, b_ref, o_ref, wb_ref, acc_ref):
    j = pl.program_id(0)

    @pl.when(j == 0)
    def _():
        acc_ref[...] = jnp.zeros_like(acc_ref)

    part, wb_ref[...] = _dot_cast(x_ref[...], w_ref[...])
    acc_ref[...] += part

    @pl.when(j == pl.num_programs(0) - 1)
    def _():
        o_ref[...] = _layer_norm(DN_ALPHA * h_ref[...] + acc_ref[...], g_ref[...], b_ref[...])


def _out_ln_sample(mix, h, w_out, g, b, *, tk):
    rows = h.shape[0]
    whole = lambda j: (0, 0)
    slab = pl.BlockSpec((tk, D_MODEL), lambda j: (j, 0))
    return pl.pallas_call(
        _out_ln_sample_body,
        grid=(w_out.shape[0] // tk,),
        in_specs=[pl.BlockSpec((rows, tk), lambda j: (0, j)), pl.BlockSpec((rows, D_MODEL), whole), slab,
                  pl.BlockSpec((1, D_MODEL), whole), pl.BlockSpec((1, D_MODEL), whole)],
        out_specs=[pl.BlockSpec((rows, D_MODEL), whole), slab],
        out_shape=[jax.ShapeDtypeStruct((rows, D_MODEL), F32), jax.ShapeDtypeStruct(w_out.shape, BF16)],
        scratch_shapes=[pltpu.VMEM((rows, D_MODEL), F32)],
        compiler_params=pltpu.CompilerParams(
            dimension_semantics=("arbitrary",), vmem_limit_bytes=VMEM_LIMIT),
        name="out_sample",
    )(mix, h, w_out, g, b)


CHUNK_BLOCKS = 4
KEY_CHUNK = CHUNK_BLOCKS * MOBA_BLOCK
N_SEL_ROWS = 32
N_POS_ROWS = 16
SPLIT = 3
V_ROWS = HEAD_DIM + 16
STAGE_LOOKAHEAD = 3
N_STAGE_BUFS = 4


def _top_blocks_t(gate, n_valid):
    blk = lax.broadcasted_iota(jnp.int32, gate.shape, 0).astype(F32)
    valid = blk < n_valid
    g = jnp.where(valid, gate, -jnp.inf)
    sel = jnp.zeros(gate.shape, dtype=jnp.bool_)
    for _ in range(MOBA_TOPK):
        mx = jnp.max(g, axis=0, keepdims=True)
        first = jnp.min(jnp.where(g == mx, blk, float(gate.shape[0])), axis=0, keepdims=True)
        hit = blk == first
        sel = sel | (hit & valid)
        g = jnp.where(hit, -jnp.inf, g)
    return sel


def _attn_prompt_body(slope_ref, q_ref, k_ref, v_ref, o_ref,
                      ka_ref, vt_ref, km_ref, pos_ref, qa_ref, m_ref, acc_ref,
                      u_ref, p_ref, mx_ref, alpha_ref):
    h = pl.program_id(0)
    i = pl.program_id(1)
    n_blocks = k_ref.shape[0] // MOBA_BLOCK
    slope2 = slope_ref[h] * LOG2E

    @pl.when(i == 0)
    def _():
        km_ref[...] = jnp.zeros_like(km_ref)
        lane = lax.broadcasted_iota(jnp.int32, (MOBA_BLOCK, LANES), 1)
        offs = lax.broadcasted_iota(jnp.int32, (MOBA_BLOCK, LANES), 0).astype(F32)
        tail = lax.broadcasted_iota(jnp.int32, (V_ROWS - HEAD_DIM, MOBA_BLOCK), 0)
        ones_row = jnp.where(tail == 0, 1.0, 0.0).astype(BF16)

        def prep(n, carry):
            rows = pl.ds(pl.multiple_of(n * MOBA_BLOCK, MOBA_BLOCK), MOBA_BLOCK)
            kblk = k_ref[rows, :]
            in_chunk = lax.convert_element_type(n % CHUNK_BLOCKS, F32)
            aug = jnp.where(lane == n, 1.0, 0.0)
            aug = jnp.where((lane >= N_SEL_ROWS) & (lane < N_SEL_ROWS + SPLIT), in_chunk, aug)
            aug = jnp.where((lane >= N_SEL_ROWS + SPLIT) & (lane < N_SEL_ROWS + 2 * SPLIT), offs, aug)
            ka_ref[rows, :HEAD_DIM] = kblk.astype(BF16)
            ka_ref[rows, HEAD_DIM:] = aug.astype(BF16)
            vt_ref[n, :HEAD_DIM] = v_ref[rows, :].T.astype(BF16)
            vt_ref[n, HEAD_DIM:] = ones_row
            km_ref[pl.ds(n, 1), :] = jnp.mean(kblk, axis=0, keepdims=True)
            return carry

        lax.fori_loop(0, n_blocks, prep, 0)
        r = lax.broadcasted_iota(jnp.int32, pos_ref.shape, 0)
        rest = jnp.full(pos_ref.shape, slope2, F32)
        rows = jnp.zeros(pos_ref.shape, F32)
        for t in range(SPLIT):
            term = rest.astype(BF16).astype(F32)
            rest = rest - term
            rows = jnp.where(r == t, term * MOBA_BLOCK, jnp.where(r == t + SPLIT, term, rows))
        pos_ref[...] = rows.astype(BF16)

    qt = q_ref[...].astype(F32).T.astype(BF16)
    gate = _dot(km_ref[...].astype(BF16), qt)
    own = i * CHUNK_BLOCKS + lax.broadcasted_iota(jnp.int32, (1, KEY_CHUNK), 1) // MOBA_BLOCK
    sel = _top_blocks_t(gate, own.astype(F32))
    blk = lax.broadcasted_iota(jnp.int32, gate.shape, 0)
    qa_ref[:HEAD_DIM, :] = qt
    qa_ref[HEAD_DIM:HEAD_DIM + N_SEL_ROWS, :] = jnp.where(sel | (blk == own), 0.0, MASKED_BF16).astype(BF16)
    qa_ref[HEAD_DIM + N_SEL_ROWS:HEAD_DIM + N_SEL_ROWS + N_POS_ROWS, :] = jnp.concatenate(
        [pos_ref[...]] * CHUNK_BLOCKS, axis=1)
    qa_ref[HEAD_DIM + N_SEL_ROWS + N_POS_ROWS:, :] = jnp.zeros(
        (HEAD_DIM - N_SEL_ROWS - N_POS_ROWS, KEY_CHUNK), BF16)

    half = KEY_CHUNK // 2
    key = lax.broadcasted_iota(jnp.int32, (half, MOBA_BLOCK), 0)
    qry = lax.broadcasted_iota(jnp.int32, (half, MOBA_BLOCK), 1)

    def stage_logits(c, j, buf, diagonal):
        cols = slice(j * MOBA_BLOCK, (j + 1) * MOBA_BLOCK)
        mx = None
        for s in range(2):
            rows = pl.ds(pl.multiple_of(c * KEY_CHUNK + s * half, half), half)
            u = _dot(ka_ref[rows, :], qa_ref[:, cols])
            if diagonal:
                u = jnp.where(key + s * half <= qry + j * MOBA_BLOCK, u, MASKED)
            u_ref[buf, s * half:(s + 1) * half, :] = u
            hmax = jnp.max(u, axis=0, keepdims=True)
            mx = hmax if mx is None else jnp.maximum(mx, hmax)
        mx_ref[buf] = mx

    def stage_softmax(c, j, buf, diagonal):
        cols = slice(j * MOBA_BLOCK, (j + 1) * MOBA_BLOCK)
        blocks_ahead = (c - i) * CHUNK_BLOCKS - j
        off = slope2 * lax.convert_element_type(blocks_ahead * MOBA_BLOCK, F32)
        m_new = mx_ref[buf] + off
        if not diagonal:
            m_old = m_ref[:, cols]
            m_new = jnp.maximum(m_old, m_new)
            alpha_ref[buf] = jnp.exp2(m_old - m_new)
        p_ref[buf] = jnp.exp2((u_ref[buf] - (m_new - off)).astype(BF16))
        m_ref[:, cols] = m_new

    def stage_values(c, j, buf, diagonal):
        cols = slice(j * MOBA_BLOCK, (j + 1) * MOBA_BLOCK)
        parts = []
        for s in range(2):
            pv = None
            for n in range(s * CHUNK_BLOCKS // 2, (s + 1) * CHUNK_BLOCKS // 2):
                d = _dot(vt_ref[c * CHUNK_BLOCKS + n], p_ref[buf, n * MOBA_BLOCK:(n + 1) * MOBA_BLOCK, :])
                pv = d if pv is None else pv + d
            parts.append(pv)
        pv = parts[0] + parts[1]
        acc_ref[:, cols] = pv if diagonal else alpha_ref[buf] * acc_ref[:, cols] + pv

    def sweep(chunks, diagonal):
        stages = [(c, j) for c in chunks for j in range(CHUNK_BLOCKS)]
        for s, (c, j) in enumerate(stages[:STAGE_LOOKAHEAD]):
            stage_logits(c, j, s % N_STAGE_BUFS, diagonal)
        for s, (c, j) in enumerate(stages):
            if s + STAGE_LOOKAHEAD < len(stages):
                nc, nj = stages[s + STAGE_LOOKAHEAD]
                stage_logits(nc, nj, (s + STAGE_LOOKAHEAD) % N_STAGE_BUFS, diagonal)
            stage_softmax(c, j, s % N_STAGE_BUFS, diagonal)
            stage_values(c, j, s % N_STAGE_BUFS, diagonal)

    sweep([i], True)

    def past_pair(cc, carry):
        sweep([2 * cc, 2 * cc + 1], False)
        return carry

    lax.fori_loop(0, i // 2, past_pair, 0)

    @pl.when(i % 2 == 1)
    def _():
        sweep([i - 1], False)

    o_ref[...] = (acc_ref[:HEAD_DIM] / acc_ref[HEAD_DIM:HEAD_DIM + 1]).T.astype(o_ref.dtype)


def _attn_prompt(slopes, q, k, v):
    t_len = q.shape[0]
    n_blocks = t_len // MOBA_BLOCK
    assert n_blocks <= N_SEL_ROWS and t_len % KEY_CHUNK == 0
    grid_spec = pltpu.PrefetchScalarGridSpec(
        num_scalar_prefetch=1,
        grid=(N_HEADS, t_len // KEY_CHUNK),
        in_specs=[
            pl.BlockSpec((KEY_CHUNK, HEAD_DIM), lambda h, i, s: (i, h)),
            pl.BlockSpec((t_len, HEAD_DIM), lambda h, i, s: (0, h)),
            pl.BlockSpec((t_len, HEAD_DIM), lambda h, i, s: (0, h)),
        ],
        out_specs=pl.BlockSpec((KEY_CHUNK, HEAD_DIM), lambda h, i, s: (i, h)),
        scratch_shapes=[
            pltpu.VMEM((t_len, 2 * HEAD_DIM), BF16),
            pltpu.VMEM((n_blocks, V_ROWS, MOBA_BLOCK), BF16),
            pltpu.VMEM((N_SEL_ROWS, HEAD_DIM), F32),
            pltpu.VMEM((N_POS_ROWS, MOBA_BLOCK), BF16),
            pltpu.VMEM((2 * HEAD_DIM, KEY_CHUNK), BF16),
            pltpu.VMEM((1, KEY_CHUNK), F32),
            pltpu.VMEM((V_ROWS, KEY_CHUNK), F32),
            pltpu.VMEM((N_STAGE_BUFS, KEY_CHUNK, MOBA_BLOCK), F32),
            pltpu.VMEM((N_STAGE_BUFS, KEY_CHUNK, MOBA_BLOCK), BF16),
            pltpu.VMEM((N_STAGE_BUFS, 1, MOBA_BLOCK), F32),
            pltpu.VMEM((N_STAGE_BUFS, 1, MOBA_BLOCK), F32),
        ],
    )
    return pl.pallas_call(
        _attn_prompt_body,
        grid_spec=grid_spec,
        out_shape=jax.ShapeDtypeStruct((t_len, WIDTH), BF16),
        compiler_params=pltpu.CompilerParams(
            dimension_semantics=("parallel", "arbitrary"), vmem_limit_bytes=VMEM_LIMIT),
        name="attn_prompt",
    )(slopes, q, k, v)


GROUP_PAGES = 12
GROUP_BLOCKS = GROUP_PAGES // PAGES_PER_BLOCK


class _SelectStep:
    def __init__(self, t, pt_ref, q_ref, ck_ref, sel_ref, buf_ref, sem_ref, stage_ref, ksum_ref):
        self.t, self.pt_ref, self.q_ref, self.ck_ref = t, pt_ref, q_ref, ck_ref
        self.sel_ref, self.buf_ref, self.sem_ref = sel_ref, buf_ref, sem_ref
        self.stage_ref, self.ksum_ref = stage_ref, ksum_ref
        self.n_seq, self.n_pages = pt_ref.shape
        self.total_pages = self.n_seq * self.n_pages
        self.total_blocks = self.total_pages // PAGES_PER_BLOCK
        self.n_steps = -(-self.total_pages // GROUP_PAGES)
        self.active = t < self.n_steps
        self.slot = t % 2

    def _page_copy(self, flat, slot, p):
        flat = jnp.minimum(flat, self.total_pages - 1)
        page = self.pt_ref[flat // self.n_pages, flat % self.n_pages]
        return pltpu.make_async_copy(
            self.ck_ref.at[page, :, 0], self.buf_ref.at[slot, pl.ds(p * PAGE_SIZE, PAGE_SIZE)],
            self.sem_ref.at[slot])

    def _step_copies(self, t, slot):
        return [self._page_copy(t * GROUP_PAGES + p, slot, p) for p in range(GROUP_PAGES)]

    def copies(self):
        @pl.when(self.t == 0)
        def _():
            for c in self._step_copies(0, 0):
                c.start()

        @pl.when(self.t + 1 < self.n_steps)
        def _():
            for c in self._step_copies(self.t + 1, 1 - self.slot):
                c.start()

        @pl.when(self.active)
        def _():
            for c in self._step_copies(self.t, self.slot):
                c.wait()

    def block_sums(self, part):
        for n in range(part * GROUP_BLOCKS // 3, (part + 1) * GROUP_BLOCKS // 3):
            blk = self.buf_ref[self.slot, pl.ds(n * MOBA_BLOCK, MOBA_BLOCK)]
            self.stage_ref[n] = jnp.sum(blk, axis=0)

    def pick(self):
        n_full = self.n_pages // PAGES_PER_BLOCK
        for n in range(GROUP_BLOCKS):
            flat = self.t * GROUP_BLOCKS + n
            self.ksum_ref[jnp.where(flat < self.total_blocks, flat, self.total_blocks + n)] = self.stage_ref[n]

        def pick_seq(b, carry):
            ksum = self.ksum_ref[pl.ds(pl.multiple_of(b * n_full, n_full), n_full)]
            kmean = ksum / MOBA_BLOCK
            gate = jnp.sum(kmean * self.q_ref[b], axis=-1, keepdims=True)
            blk_id = lax.broadcasted_iota(jnp.int32, gate.shape, 0).astype(F32)
            lane = lax.broadcasted_iota(jnp.int32, (N_HEADS, LANES), 1)
            picks = jnp.zeros((N_HEADS, LANES), F32)
            for j in range(MOBA_TOPK):
                mx = jnp.max(gate, axis=0, keepdims=True)
                first = jnp.min(jnp.where(gate == mx, blk_id, float(n_full)), axis=0, keepdims=True)
                picks = jnp.where(lane == j, first[0], picks)
                gate = jnp.where(blk_id == first, -jnp.inf, gate)
            self.sel_ref[b] = picks.astype(jnp.int32)
            return carry

        @pl.when(self.t == self.n_steps - 1)
        def _():
            lax.fori_loop(0, self.n_seq, pick_seq, 0)


def _ffn_ln_select_body(pt_ref, x_ref, wg_ref, wu_ref, wd_ref, g_ref, b_ref, q_ref, ck_ref,
                        o_ref, sel_ref, xb_ref, buf_ref, sem_ref, stage_ref, ksum_ref):
    t = pl.program_id(0) * pl.num_programs(1) + pl.program_id(1)
    step = _SelectStep(t, pt_ref, q_ref, ck_ref, sel_ref, buf_ref, sem_ref, stage_ref, ksum_ref)
    step.copies()
    _ffn_ln_body(x_ref, wg_ref, wu_ref, wd_ref, g_ref, b_ref, o_ref, xb_ref, side_work=step.block_sums)
    step.pick()


def _ffn_ln_select(x, wg, wu, wd, g, b, page_table, q_sample, cache_k, *, tm, tf):
    rows = x.shape[0]
    n_seq, n_pages = page_table.shape
    total_blocks = n_seq * n_pages // PAGES_PER_BLOCK
    grid = (rows // tm, D_FF // tf)
    assert n_pages % PAGES_PER_BLOCK == 0 and grid[0] * grid[1] * GROUP_PAGES >= n_seq * n_pages
    once = pl.Buffered(1)
    grid_spec = pltpu.PrefetchScalarGridSpec(
        num_scalar_prefetch=1,
        grid=grid,
        in_specs=[
            pl.BlockSpec((tm, D_MODEL), lambda i, j, pt: (i, 0), pipeline_mode=once),
            pl.BlockSpec((D_MODEL, tf), lambda i, j, pt: (0, j)),
            pl.BlockSpec((D_MODEL, tf), lambda i, j, pt: (0, j)),
            pl.BlockSpec((tf, D_MODEL), lambda i, j, pt: (j, 0)),
            pl.BlockSpec((1, D_MODEL), lambda i, j, pt: (0, 0)),
            pl.BlockSpec((1, D_MODEL), lambda i, j, pt: (0, 0)),
            pl.BlockSpec((n_seq, N_HEADS, HEAD_DIM), lambda i, j, pt: (0, 0, 0)),
            pl.BlockSpec(memory_space=pl.ANY),
        ],
        out_specs=[
            pl.BlockSpec((tm, D_MODEL), lambda i, j, pt: (i, 0), pipeline_mode=once),
            pl.BlockSpec((n_seq, N_HEADS, LANES), lambda i, j, pt: (0, 0, 0)),
        ],
        scratch_shapes=[
            pltpu.VMEM((tm, D_MODEL), BF16),
            pltpu.VMEM((2, GROUP_PAGES * PAGE_SIZE, N_HEADS, HEAD_DIM), F32),
            pltpu.SemaphoreType.DMA((2,)),
            pltpu.VMEM((GROUP_BLOCKS, N_HEADS, HEAD_DIM), F32),
            pltpu.VMEM((total_blocks + GROUP_BLOCKS, N_HEADS, HEAD_DIM), F32),
        ],
    )
    return pl.pallas_call(
        _ffn_ln_select_body,
        grid_spec=grid_spec,
        out_shape=[jax.ShapeDtypeStruct((rows, D_MODEL), F32),
                   jax.ShapeDtypeStruct((n_seq, N_HEADS, LANES), jnp.int32)],
        compiler_params=pltpu.CompilerParams(
            dimension_semantics=("arbitrary", "arbitrary"), vmem_limit_bytes=VMEM_LIMIT),
        name="ffn1_prompt_select",
    )(page_table, x, wg, wu, wd, g, b, q_sample.reshape(n_seq, N_HEADS, HEAD_DIM), cache_k)


def _attn_sample_body(pt_ref, sel_ref, slope_ref, q_ref, kn_ref, vn_ref, ck_ref, cv_ref, o_ref,
                      kbuf_ref, vbuf_ref, sem_ref, *, past_len):
    b = pl.program_id(0)
    n_sel = MOBA_TOPK
    slot = b % 2

    def gathers(b, slot):
        out = []
        for h in range(N_HEADS):
            for j in range(n_sel):
                for pp in range(PAGES_PER_BLOCK):
                    page = pt_ref[b, sel_ref[b, h * MOBA_TOPK + j] * PAGES_PER_BLOCK + pp]
                    rows = pl.ds((j * PAGES_PER_BLOCK + pp) * PAGE_SIZE, PAGE_SIZE)
                    out.append(pltpu.make_async_copy(
                        ck_ref.at[page, :, 0, h, :], kbuf_ref.at[slot, h, rows, :], sem_ref.at[slot, 0]))
                    out.append(pltpu.make_async_copy(
                        cv_ref.at[page, :, 0, h, :], vbuf_ref.at[slot, h, rows, :], sem_ref.at[slot, 1]))
        return out

    @pl.when(b == 0)
    def _():
        for c in gathers(0, 0):
            c.start()

    @pl.when(b + 1 < pl.num_programs(0))
    def _():
        for c in gathers(b + 1, 1 - slot):
            c.start()

    for c in gathers(b, slot):
        c.wait()

    n_keys = n_sel * MOBA_BLOCK
    lane = lax.broadcasted_iota(jnp.int32, (1, n_keys), 1)
    which = lane // MOBA_BLOCK
    offs = lane % MOBA_BLOCK
    outs = []
    for h in range(N_HEADS):
        sl = slice(h * HEAD_DIM, (h + 1) * HEAD_DIM)
        slope = slope_ref[h]
        qh = q_ref[:, sl]
        kn = kn_ref[:, sl]
        vn = vn_ref[:, sl]
        q8 = jnp.broadcast_to(qh, (8, HEAD_DIM)).astype(BF16)
        s = _dot_nt(q8, kbuf_ref[slot, h].astype(BF16))[:1] * ATTN_SCALE
        blk = jnp.zeros((1, n_keys), jnp.int32)
        for j in range(n_sel):
            blk = jnp.where(which == j, sel_ref[b, h * MOBA_TOPK + j], blk)
        dist = (past_len - (blk * MOBA_BLOCK + offs)).astype(F32)
        s = s - slope * dist
        s_own = jnp.sum(qh * kn, axis=-1, keepdims=True) * ATTN_SCALE
        mx = jnp.maximum(jnp.max(s, axis=-1, keepdims=True), s_own)
        p = jnp.exp(s - mx)
        p_own = jnp.exp(s_own - mx)
        denom = jnp.sum(p, axis=-1, keepdims=True) + p_own
        p8 = jnp.broadcast_to(p, (8, n_keys)).astype(BF16)
        pv = _dot(p8, vbuf_ref[slot, h].astype(BF16))[:1]
        outs.append((pv + p_own * vn) / denom)
    o_ref[...] = jnp.concatenate(outs, axis=-1)


def _attn_sample(page_table, sel, slopes, q, k_new, v_new, cache_k, cache_v):
    n_seq, n_pages = page_table.shape
    per_seq = pl.BlockSpec((None, 1, WIDTH), lambda b, *_: (b, 0, 0))
    grid_spec = pltpu.PrefetchScalarGridSpec(
        num_scalar_prefetch=3,
        grid=(n_seq,),
        in_specs=[
            per_seq, per_seq, per_seq,
            pl.BlockSpec(memory_space=pl.ANY),
            pl.BlockSpec(memory_space=pl.ANY),
        ],
        out_specs=per_seq,
        scratch_shapes=[
            pltpu.VMEM((2, N_HEADS, MOBA_TOPK * MOBA_BLOCK, HEAD_DIM), F32),
            pltpu.VMEM((2, N_HEADS, MOBA_TOPK * MOBA_BLOCK, HEAD_DIM), F32),
            pltpu.SemaphoreType.DMA((2, 2)),
        ],
    )
    return pl.pallas_call(
        functools.partial(_attn_sample_body, past_len=n_pages * PAGE_SIZE),
        grid_spec=grid_spec,
        out_shape=jax.ShapeDtypeStruct((n_seq, 1, WIDTH), F32),
        compiler_params=pltpu.CompilerParams(
            dimension_semantics=("arbitrary",), vmem_limit_bytes=VMEM_LIMIT),
        name="attn_sample",
    )(page_table, sel, slopes, q.reshape(n_seq, 1, WIDTH), k_new.reshape(n_seq, 1, WIDTH),
      v_new.reshape(n_seq, 1, WIDTH), cache_k, cache_v).reshape(n_seq, WIDTH)


def kernel(x_prompt, x_sample, cache_k, cache_v, page_table, ffn1_w_gate, ffn1_w_up, ffn1_w_down, ln1_g, ln1_b, w_in, a_norm_g, a_norm_b, a_spatial_w, a_spatial_b, w_out, ln2_g, ln2_b, ffn2_w_gate, ffn2_w_up, ffn2_w_down, ln3_g, ln3_b):
    assert x_prompt.shape[0] == 1 and x_sample.shape[1] == 1 and w_in.shape[0] == DEPTH == 1
    t_len = x_prompt.shape[1]
    n_seq = x_sample.shape[0]
    xp = x_prompt.reshape(t_len, D_MODEL)
    xs = x_sample.reshape(n_seq, D_MODEL)
    slopes = jnp.exp2(-8.0 * jnp.arange(1, N_HEADS + 1, dtype=F32) / N_HEADS)

    row = lambda v: v.reshape(1, -1)
    ln1, ln2, ln3 = ((row(g[0]), row(b[0])) for g, b in ((ln1_g, ln1_b), (ln2_g, ln2_b), (ln3_g, ln3_b)))
    ag, ab = row(a_norm_g[0]), row(a_norm_b[0])
    ws, bs = a_spatial_w[0], a_spatial_b[0]
    bs_exp = jnp.repeat(bs.T, HEAD_DIM, axis=1)
    w00 = row(jnp.repeat(ws[:, 0, 0], HEAD_DIM))
    b0 = row(jnp.repeat(bs[:, 0], HEAD_DIM))

    hs, w1g, w1u, w1d = _ffn_ln_sample(xs, ffn1_w_gate[0], ffn1_w_up[0], ffn1_w_down[0], *ln1,
                                       tf=256, accurate=True, name="ffn1_sample")
    qs, ks, vs, gm_s, van_s, w_in_b = _proj_sample(hs, w_in[0], ag, ab, w00, b0)
    hp, sel = _ffn_ln_select(xp, w1g, w1u, w1d, *ln1, page_table, qs, cache_k, tm=1024, tf=256)
    qp, kp, vp, gm_p = _proj_prompt(hp, w_in_b, ag, ab, ws, bs_exp, tm=512)

    sel = sel[:, :, :MOBA_TOPK].reshape(n_seq, N_HEADS * MOBA_TOPK)
    attn_s = _attn_sample(page_table, sel, slopes, qs, ks, vs, cache_k, cache_v)
    attn_p = _attn_prompt(slopes, qp, kp, vp)

    hs, w_out_b = _out_ln_sample(jnp.concatenate([attn_s, gm_s], axis=-1), hs, w_out[0], *ln2, tk=512)
    hp = _out_ln(attn_p, gm_p, hp, w_out_b, *ln2, tm=512)

    ys, w2g, w2u, w2d = _ffn_ln_sample(hs, ffn2_w_gate[0], ffn2_w_up[0], ffn2_w_down[0], *ln3,
                                       tf=256, accurate=False, name="ffn2_sample")
    yp = _ffn_ln(hp, w2g, w2u, w2d, *ln3, tm=512, tf=512, name="ffn2_prompt")

    heads = (1, N_HEADS, HEAD_DIM)
    return (yp.reshape(1, t_len, D_MODEL), ys.reshape(n_seq, 1, D_MODEL),
            kp.reshape((1, t_len) + heads), vp.reshape((1, t_len) + heads),
            ks.reshape((n_seq, 1) + heads), vs.reshape((n_seq, 1) + heads),
            van_s.reshape((n_seq, 1) + heads))
```

```python
import functools

import jax
import jax.numpy as jnp
from jax import lax
from jax.experimental import pallas as pl
from jax.experimental.pallas import tpu as pltpu

D_MODEL = 2048
HEAD_DIM = 128
N_HEADS = 8
WIDTH = N_HEADS * HEAD_DIM
CHUNK = 128
MOBA_BLOCK = 256
MOBA_TOPK = 3
PAGE_SIZE = 128
PAGES_PER_BLOCK = MOBA_BLOCK // PAGE_SIZE
D_FF = 5632
LN_EPS = 1e-5
DEPTH = 1
DN_ALPHA = (2 * DEPTH) ** 0.25
ATTN_SCALE = HEAD_DIM ** -0.5
LOG2E = 1.4426950408889634

MASKED = -1e30
MASKED_BF16 = -(2.0 ** 100)

LANES = 128
VMEM_LIMIT = 61 * 1024 * 1024

BF16 = jnp.bfloat16
F32 = jnp.float32


def _layer_norm(x, g, b):
    mu = jnp.mean(x, axis=-1, keepdims=True)
    xc = x - mu
    var = jnp.mean(xc * xc, axis=-1, keepdims=True)
    return xc * lax.rsqrt(var + LN_EPS) * g + b


def _dot(a, b):
    return jnp.dot(a, b, preferred_element_type=F32)


def _dot_nt(a, b):
    return lax.dot_general(a, b, (((1,), (1,)), ((), ())), preferred_element_type=F32)


def _ffn_ln_body(x_ref, wg_ref, wu_ref, wd_ref, g_ref, b_ref, o_ref, xb_ref, side_work=None):
    j = pl.program_id(1)

    @pl.when(j == 0)
    def _():
        xb_ref[...] = x_ref[...].astype(BF16)
        o_ref[...] = jnp.zeros_like(o_ref)

    side = side_work if side_work is not None else (lambda part: None)
    xb = xb_ref[...]
    side(0)
    gate = _dot(xb, wg_ref[...])
    side(1)
    up = _dot(xb, wu_ref[...])
    act = gate * jax.nn.sigmoid(gate) * up
    side(2)
    o_ref[...] += _dot(act.astype(BF16), wd_ref[...])

    @pl.when(j == pl.num_programs(1) - 1)
    def _():
        y = DN_ALPHA * x_ref[...] + 0.5 * o_ref[...]
        o_ref[...] = _layer_norm(y, g_ref[...], b_ref[...])


def _ffn_ln(x, wg, wu, wd, g, b, *, tm, tf, name):
    rows = x.shape[0]
    return pl.pallas_call(
        _ffn_ln_body,
        grid=(rows // tm, D_FF // tf),
        in_specs=[
            pl.BlockSpec((tm, D_MODEL), lambda i, j: (i, 0)),
            pl.BlockSpec((D_MODEL, tf), lambda i, j: (0, j)),
            pl.BlockSpec((D_MODEL, tf), lambda i, j: (0, j)),
            pl.BlockSpec((tf, D_MODEL), lambda i, j: (j, 0)),
            pl.BlockSpec((1, D_MODEL), lambda i, j: (0, 0)),
            pl.BlockSpec((1, D_MODEL), lambda i, j: (0, 0)),
        ],
        out_specs=pl.BlockSpec((tm, D_MODEL), lambda i, j: (i, 0)),
        out_shape=jax.ShapeDtypeStruct((rows, D_MODEL), F32),
        scratch_shapes=[pltpu.VMEM((tm, D_MODEL), BF16)],
        compiler_params=pltpu.CompilerParams(
            dimension_semantics=("parallel", "arbitrary"), vmem_limit_bytes=VMEM_LIMIT),
        name=name,
    )(x, wg, wu, wd, g, b)


def _split(x):
    hi = x.astype(BF16)
    return hi, (x - hi.astype(F32)).astype(BF16)


def _dot_split(x, w):
    x_hi, x_lo = _split(x)
    w_hi, w_lo = _split(w)
    n = x.shape[0]
    both = _dot(jnp.concatenate([x_hi, x_lo], axis=0), w_hi)
    return both[:n] + (both[n:] + _dot(x_hi, w_lo)), w_hi


def _dot_cast(x, w):
    w_hi = w.astype(BF16)
    return _dot(x.astype(BF16), w_hi), w_hi


def _ffn_ln_sample_body(x_ref, wg_ref, wu_ref, wd_ref, g_ref, b_ref,
                        o_ref, wgb_ref, wub_ref, wdb_ref, acc_ref, *, accurate):
    j = pl.program_id(0)
    dot = _dot_split if accurate else _dot_cast

    @pl.when(j == 0)
    def _():
        acc_ref[...] = jnp.zeros_like(acc_ref)

    x = x_ref[...]
    gate, wgb_ref[...] = dot(x, wg_ref[...])
    up, wub_ref[...] = dot(x, wu_ref[...])
    down, wdb_ref[...] = dot(gate * jax.nn.sigmoid(gate) * up, wd_ref[...])
    acc_ref[...] += down

    @pl.when(j == pl.num_programs(0) - 1)
    def _():
        y = DN_ALPHA * x + 0.5 * acc_ref[...]
        o_ref[...] = _layer_norm(y, g_ref[...], b_ref[...])


def _ffn_ln_sample(x, wg, wu, wd, g, b, *, tf, accurate, name):
    rows = x.shape[0]
    whole = lambda j: (0, 0)
    cols = pl.BlockSpec((D_MODEL, tf), lambda j: (0, j))
    rws = pl.BlockSpec((tf, D_MODEL), lambda j: (j, 0))
    return pl.pallas_call(
        functools.partial(_ffn_ln_sample_body, accurate=accurate),
        grid=(D_FF // tf,),
        in_specs=[pl.BlockSpec((rows, D_MODEL), whole), cols, cols, rws,
                  pl.BlockSpec((1, D_MODEL), whole), pl.BlockSpec((1, D_MODEL), whole)],
        out_specs=[pl.BlockSpec((rows, D_MODEL), whole), cols, cols, rws],
        out_shape=[jax.ShapeDtypeStruct((rows, D_MODEL), F32),
                   jax.ShapeDtypeStruct(wg.shape, BF16),
                   jax.ShapeDtypeStruct(wu.shape, BF16),
                   jax.ShapeDtypeStruct(wd.shape, BF16)],
        scratch_shapes=[pltpu.VMEM((rows, D_MODEL), F32)],
        compiler_params=pltpu.CompilerParams(
            dimension_semantics=("arbitrary",), vmem_limit_bytes=VMEM_LIMIT),
        name=name,
    )(x, wg, wu, wd, g, b)


def _head_layer_norm(va, g, b):
    parts = []
    for h in range(N_HEADS):
        sl = slice(h * HEAD_DIM, (h + 1) * HEAD_DIM)
        parts.append(_layer_norm(va[:, sl], g[:, sl], b[:, sl]))
    return jnp.concatenate(parts, axis=-1)


def _proj_prompt_body(x_ref, w_ref, ag_ref, ab_ref, ws_ref, bs_ref,
                      q_ref, k_ref, v_ref, gm_ref, xb_ref, u_ref):
    j = pl.program_id(1)

    @pl.when(j == 0)
    def _():
        xb_ref[...] = x_ref[...].astype(BF16)

    def project():
        return _dot(xb_ref[...], w_ref[...])

    @pl.when(j == 0)
    def _():
        q_ref[...] = (project() * (ATTN_SCALE * LOG2E)).astype(BF16)

    @pl.when(j == 1)
    def _():
        k_ref[...] = project()

    @pl.when(j == 2)
    def _():
        v_ref[...] = project()

    @pl.when(j == 3)
    def _():
        u_ref[...] = project()

    @pl.when(j == 4)
    def _():
        p = project()
        van = _head_layer_norm(p, ag_ref[...], ab_ref[...]).astype(BF16)
        row = lax.broadcasted_iota(jnp.int32, (CHUNK, CHUNK), 0)
        col = lax.broadcasted_iota(jnp.int32, (CHUNK, CHUNK), 1)
        causal = row >= col
        n_chunks = p.shape[0] // CHUNK
        for h in range(N_HEADS):
            sl = slice(h * HEAD_DIM, (h + 1) * HEAD_DIM)
            w_h = jnp.where(causal, ws_ref[h], 0.0).astype(BF16)
            for c in range(n_chunks):
                rs = slice(c * CHUNK, (c + 1) * CHUNK)
                mixed = _dot(w_h, van[rs, sl]) + bs_ref[:, sl]
                gm_ref[rs, sl] = (u_ref[rs, sl] * mixed).astype(BF16)


def _proj_prompt(x, w_in, ag, ab, ws, bs_exp, *, tm):
    rows = x.shape[0]
    row_blk = lambda i, j: (i, 0)
    const2 = lambda i, j: (0, 0)
    return pl.pallas_call(
        _proj_prompt_body,
        grid=(rows // tm, 5),
        in_specs=[
            pl.BlockSpec((tm, D_MODEL), row_blk),
            pl.BlockSpec((D_MODEL, WIDTH), lambda i, j: (0, j)),
            pl.BlockSpec((1, WIDTH), const2),
            pl.BlockSpec((1, WIDTH), const2),
            pl.BlockSpec((N_HEADS, CHUNK, CHUNK), lambda i, j: (0, 0, 0)),
            pl.BlockSpec((CHUNK, WIDTH), const2),
        ],
        out_specs=[
            pl.BlockSpec((tm, WIDTH), row_blk),
            pl.BlockSpec((tm, WIDTH), row_blk),
            pl.BlockSpec((tm, WIDTH), row_blk),
            pl.BlockSpec((tm, WIDTH), row_blk),
        ],
        out_shape=[
            jax.ShapeDtypeStruct((rows, WIDTH), BF16),
            jax.ShapeDtypeStruct((rows, WIDTH), F32),
            jax.ShapeDtypeStruct((rows, WIDTH), F32),
            jax.ShapeDtypeStruct((rows, WIDTH), BF16),
        ],
        scratch_shapes=[pltpu.VMEM((tm, D_MODEL), BF16), pltpu.VMEM((tm, WIDTH), F32)],
        compiler_params=pltpu.CompilerParams(
            dimension_semantics=("parallel", "arbitrary"), vmem_limit_bytes=VMEM_LIMIT),
        name="proj_prompt",
    )(x, w_in, ag, ab, ws, bs_exp)


def _proj_sample_body(x_ref, w_ref, ag_ref, ab_ref, w00_ref, b0_ref,
                      q_ref, k_ref, v_ref, gm_ref, van_ref, wb_ref, u_ref):
    j = pl.program_id(0)
    p, w_hi = _dot_split(x_ref[...], w_ref[...])
    wb_ref[...] = w_hi

    @pl.when(j == 0)
    def _():
        q_ref[...] = p

    @pl.when(j == 1)
    def _():
        k_ref[...] = p

    @pl.when(j == 2)
    def _():
        v_ref[...] = p

    @pl.when(j == 3)
    def _():
        u_ref[...] = p

    @pl.when(j == 4)
    def _():
        van = _head_layer_norm(p, ag_ref[...], ab_ref[...])
        van_ref[...] = van
        gm_ref[...] = u_ref[...] * (w00_ref[...] * van + b0_ref[...])


def _proj_sample(x, w_in, ag, ab, w00, b0):
    rows = x.shape[0]
    whole = lambda j: (0, 0)
    w_cols = lambda j: (0, j)
    return pl.pallas_call(
        _proj_sample_body,
        grid=(5,),
        in_specs=[
            pl.BlockSpec((rows, D_MODEL), whole),
            pl.BlockSpec((D_MODEL, WIDTH), w_cols),
            pl.BlockSpec((1, WIDTH), whole),
            pl.BlockSpec((1, WIDTH), whole),
            pl.BlockSpec((1, WIDTH), whole),
            pl.BlockSpec((1, WIDTH), whole),
        ],
        out_specs=[pl.BlockSpec((rows, WIDTH), whole)] * 5 + [pl.BlockSpec((D_MODEL, WIDTH), w_cols)],
        out_shape=[
            jax.ShapeDtypeStruct((rows, WIDTH), F32),
            jax.ShapeDtypeStruct((rows, WIDTH), F32),
            jax.ShapeDtypeStruct((rows, WIDTH), F32),
            jax.ShapeDtypeStruct((rows, WIDTH), F32),
            jax.ShapeDtypeStruct((rows, WIDTH), F32),
            jax.ShapeDtypeStruct(w_in.shape, BF16),
        ],
        scratch_shapes=[pltpu.VMEM((rows, WIDTH), F32)],
        compiler_params=pltpu.CompilerParams(
            dimension_semantics=("arbitrary",), vmem_limit_bytes=VMEM_LIMIT),
        name="proj_sample",
    )(x, w_in, ag, ab, w00, b0)


def _out_ln_body(a_ref, m_ref, h_ref, wa_ref, wm_ref, g_ref, b_ref, o_ref):
    mix = _dot(a_ref[...], wa_ref[...]) + _dot(m_ref[...], wm_ref[...])
    o_ref[...] = _layer_norm(DN_ALPHA * h_ref[...] + mix, g_ref[...], b_ref[...])


def _out_ln(attn, gm, h, w_out, g, b, *, tm):
    rows = h.shape[0]
    row_blk = lambda i: (i, 0)
    const = lambda i: (0, 0)
    return pl.pallas_call(
        _out_ln_body,
        grid=(rows // tm,),
        in_specs=[
            pl.BlockSpec((tm, WIDTH), row_blk),
            pl.BlockSpec((tm, WIDTH), row_blk),
            pl.BlockSpec((tm, D_MODEL), row_blk),
            pl.BlockSpec((WIDTH, D_MODEL), const),
            pl.BlockSpec((WIDTH, D_MODEL), lambda i: (1, 0)),
            pl.BlockSpec((1, D_MODEL), const),
            pl.BlockSpec((1, D_MODEL), const),
        ],
        out_specs=pl.BlockSpec((tm, D_MODEL), row_blk),
        out_shape=jax.ShapeDtypeStruct((rows, D_MODEL), F32),
        compiler_params=pltpu.CompilerParams(
            dimension_semantics=("parallel",), vmem_limit_bytes=VMEM_LIMIT),
        name="out_prompt",
    )(attn, gm, h, w_out, w_out, g, b)


def _out_ln_sample_body(x_ref, h_ref, w_ref, g_ref, b_ref, o_ref, wb_ref, acc_ref):
    j = pl.program_id(0)

    @pl.when(j == 0)
    def _():
        acc_ref[...] = jnp.zeros_like(acc_ref)

    part, wb_ref[...] = _dot_cast(x_ref[...], w_ref[...])
    acc_ref[...] += part

    @pl.when(j == pl.num_programs(0) - 1)
    def _():
        o_ref[...] = _layer_norm(DN_ALPHA * h_ref[...] + acc_ref[...], g_ref[...], b_ref[...])


def _out_ln_sample(mix, h, w_out, g, b, *, tk):
    rows = h.shape[0]
    whole = lambda j: (0, 0)
    slab = pl.BlockSpec((tk, D_MODEL), lambda j: (j, 0))
    return pl.pallas_call(
        _out_ln_sample_body,
        grid=(w_out.shape[0] // tk,),
        in_specs=[pl.BlockSpec((rows, tk), lambda j: (0, j)), pl.BlockSpec((rows, D_MODEL), whole), slab,
                  pl.BlockSpec((1, D_MODEL), whole), pl.BlockSpec((1, D_MODEL), whole)],
        out_specs=[pl.BlockSpec((rows, D_MODEL), whole), slab],
        out_shape=[jax.ShapeDtypeStruct((rows, D_MODEL), F32), jax.ShapeDtypeStruct(w_out.shape, BF16)],
        scratch_shapes=[pltpu.VMEM((rows, D_MODEL), F32)],
        compiler_params=pltpu.CompilerParams(
            dimension_semantics=("arbitrary",), vmem_limit_bytes=VMEM_LIMIT),
        name="out_sample",
    )(mix, h, w_out, g, b)


CHUNK_BLOCKS = 4
KEY_CHUNK = CHUNK_BLOCKS * MOBA_BLOCK
N_SEL_ROWS = 32
N_POS_ROWS = 16
SPLIT = 3
V_ROWS = HEAD_DIM + 16
STAGE_LOOKAHEAD = 3
N_STAGE_BUFS = 4


def _top_blocks_t(gate, n_valid):
    blk = lax.broadcasted_iota(jnp.int32, gate.shape, 0).astype(F32)
    valid = blk < n_valid
    g = jnp.where(valid, gate, -jnp.inf)
    sel = jnp.zeros(gate.shape, dtype=jnp.bool_)
    for _ in range(MOBA_TOPK):
        mx = jnp.max(g, axis=0, keepdims=True)
        first = jnp.min(jnp.where(g == mx, blk, float(gate.shape[0])), axis=0, keepdims=True)
        hit = blk == first
        sel = sel | (hit & valid)
        g = jnp.where(hit, -jnp.inf, g)
    return sel


def _attn_prompt_body(slope_ref, q_ref, k_ref, v_ref, o_ref,
                      ka_ref, vt_ref, km_ref, pos_ref, qa_ref, m_ref, acc_ref,
                      u_ref, p_ref, mx_ref, alpha_ref):
    h = pl.program_id(0)
    i = pl.program_id(1)
    n_blocks = k_ref.shape[0] // MOBA_BLOCK
    slope2 = slope_ref[h] * LOG2E

    @pl.when(i == 0)
    def _():
        km_ref[...] = jnp.zeros_like(km_ref)
        lane = lax.broadcasted_iota(jnp.int32, (MOBA_BLOCK, LANES), 1)
        offs = lax.broadcasted_iota(jnp.int32, (MOBA_BLOCK, LANES), 0).astype(F32)
        tail = lax.broadcasted_iota(jnp.int32, (V_ROWS - HEAD_DIM, MOBA_BLOCK), 0)
        ones_row = jnp.where(tail == 0, 1.0, 0.0).astype(BF16)

        def prep(n, carry):
            rows = pl.ds(pl.multiple_of(n * MOBA_BLOCK, MOBA_BLOCK), MOBA_BLOCK)
            kblk = k_ref[rows, :]
            in_chunk = lax.convert_element_type(n % CHUNK_BLOCKS, F32)
            aug = jnp.where(lane == n, 1.0, 0.0)
            aug = jnp.where((lane >= N_SEL_ROWS) & (lane < N_SEL_ROWS + SPLIT), in_chunk, aug)
            aug = jnp.where((lane >= N_SEL_ROWS + SPLIT) & (lane < N_SEL_ROWS + 2 * SPLIT), offs, aug)
            ka_ref[rows, :HEAD_DIM] = kblk.astype(BF16)
            ka_ref[rows, HEAD_DIM:] = aug.astype(BF16)
            vt_ref[n, :HEAD_DIM] = v_ref[rows, :].T.astype(BF16)
            vt_ref[n, HEAD_DIM:] = ones_row
            km_ref[pl.ds(n, 1), :] = jnp.mean(kblk, axis=0, keepdims=True)
            return carry

        lax.fori_loop(0, n_blocks, prep, 0)
        r = lax.broadcasted_iota(jnp.int32, pos_ref.shape, 0)
        rest = jnp.full(pos_ref.shape, slope2, F32)
        rows = jnp.zeros(pos_ref.shape, F32)
        for t in range(SPLIT):
            term = rest.astype(BF16).astype(F32)
            rest = rest - term
            rows = jnp.where(r == t, term * MOBA_BLOCK, jnp.where(r == t + SPLIT, term, rows))
        pos_ref[...] = rows.astype(BF16)

    qt = q_ref[...].astype(F32).T.astype(BF16)
    gate = _dot(km_ref[...].astype(BF16), qt)
    own = i * CHUNK_BLOCKS + lax.broadcasted_iota(jnp.int32, (1, KEY_CHUNK), 1) // MOBA_BLOCK
    sel = _top_blocks_t(gate, own.astype(F32))
    blk = lax.broadcasted_iota(jnp.int32, gate.shape, 0)
    qa_ref[:HEAD_DIM, :] = qt
    qa_ref[HEAD_DIM:HEAD_DIM + N_SEL_ROWS, :] = jnp.where(sel | (blk == own), 0.0, MASKED_BF16).astype(BF16)
    qa_ref[HEAD_DIM + N_SEL_ROWS:HEAD_DIM + N_SEL_ROWS + N_POS_ROWS, :] = jnp.concatenate(
        [pos_ref[...]] * CHUNK_BLOCKS, axis=1)
    qa_ref[HEAD_DIM + N_SEL_ROWS + N_POS_ROWS:, :] = jnp.zeros(
        (HEAD_DIM - N_SEL_ROWS - N_POS_ROWS, KEY_CHUNK), BF16)

    half = KEY_CHUNK // 2
    key = lax.broadcasted_iota(jnp.int32, (half, MOBA_BLOCK), 0)
    qry = lax.broadcasted_iota(jnp.int32, (half, MOBA_BLOCK), 1)

    def stage_logits(c, j, buf, diagonal):
        cols = slice(j * MOBA_BLOCK, (j + 1) * MOBA_BLOCK)
        mx = None
        for s in range(2):
            rows = pl.ds(pl.multiple_of(c * KEY_CHUNK + s * half, half), half)
            u = _dot(ka_ref[rows, :], qa_ref[:, cols])
            if diagonal:
                u = jnp.where(key + s * half <= qry + j * MOBA_BLOCK, u, MASKED)
            u_ref[buf, s * half:(s + 1) * half, :] = u
            hmax = jnp.max(u, axis=0, keepdims=True)
            mx = hmax if mx is None else jnp.maximum(mx, hmax)
        mx_ref[buf] = mx

    def stage_softmax(c, j, buf, diagonal):
        cols = slice(j * MOBA_BLOCK, (j + 1) * MOBA_BLOCK)
        blocks_ahead = (c - i) * CHUNK_BLOCKS - j
        off = slope2 * lax.convert_element_type(blocks_ahead * MOBA_BLOCK, F32)
        m_new = mx_ref[buf] + off
        if not diagonal:
            m_old = m_ref[:, cols]
            m_new = jnp.maximum(m_old, m_new)
            alpha_ref[buf] = jnp.exp2(m_old - m_new)
        p_ref[buf] = jnp.exp2((u_ref[buf] - (m_new - off)).astype(BF16))
        m_ref[:, cols] = m_new

    def stage_values(c, j, buf, diagonal):
        cols = slice(j * MOBA_BLOCK, (j + 1) * MOBA_BLOCK)
        parts = []
        for s in range(2):
            pv = None
            for n in range(s * CHUNK_BLOCKS // 2, (s + 1) * CHUNK_BLOCKS // 2):
                d = _dot(vt_ref[c * CHUNK_BLOCKS + n], p_ref[buf, n * MOBA_BLOCK:(n + 1) * MOBA_BLOCK, :])
                pv = d if pv is None else pv + d
            parts.append(pv)
        pv = parts[0] + parts[1]
        acc_ref[:, cols] = pv if diagonal else alpha_ref[buf] * acc_ref[:, cols] + pv

    def sweep(chunks, diagonal):
        stages = [(c, j) for c in chunks for j in range(CHUNK_BLOCKS)]
        for s, (c, j) in enumerate(stages[:STAGE_LOOKAHEAD]):
            stage_logits(c, j, s % N_STAGE_BUFS, diagonal)
        for s, (c, j) in enumerate(stages):
            if s + STAGE_LOOKAHEAD < len(stages):
                nc, nj = stages[s + STAGE_LOOKAHEAD]
                stage_logits(nc, nj, (s + STAGE_LOOKAHEAD) % N_STAGE_BUFS, diagonal)
            stage_softmax(c, j, s % N_STAGE_BUFS, diagonal)
            stage_values(c, j, s % N_STAGE_BUFS, diagonal)

    sweep([i], True)

    def past_pair(cc, carry):
        sweep([2 * cc, 2 * cc + 1], False)
        return carry

    lax.fori_loop(0, i // 2, past_pair, 0)

    @pl.when(i % 2 == 1)
    def _():
        sweep([i - 1], False)

    o_ref[...] = (acc_ref[:HEAD_DIM] / acc_ref[HEAD_DIM:HEAD_DIM + 1]).T.astype(o_ref.dtype)


def _attn_prompt(slopes, q, k, v):
    t_len = q.shape[0]
    n_blocks = t_len // MOBA_BLOCK
    assert n_blocks <= N_SEL_ROWS and t_len % KEY_CHUNK == 0
    grid_spec = pltpu.PrefetchScalarGridSpec(
        num_scalar_prefetch=1,
        grid=(N_HEADS, t_len // KEY_CHUNK),
        in_specs=[
            pl.BlockSpec((KEY_CHUNK, HEAD_DIM), lambda h, i, s: (i, h)),
            pl.BlockSpec((t_len, HEAD_DIM), lambda h, i, s: (0, h)),
            pl.BlockSpec((t_len, HEAD_DIM), lambda h, i, s: (0, h)),
        ],
        out_specs=pl.BlockSpec((KEY_CHUNK, HEAD_DIM), lambda h, i, s: (i, h)),
        scratch_shapes=[
            pltpu.VMEM((t_len, 2 * HEAD_DIM), BF16),
            pltpu.VMEM((n_blocks, V_ROWS, MOBA_BLOCK), BF16),
            pltpu.VMEM((N_SEL_ROWS, HEAD_DIM), F32),
            pltpu.VMEM((N_POS_ROWS, MOBA_BLOCK), BF16),
            pltpu.VMEM((2 * HEAD_DIM, KEY_CHUNK), BF16),
            pltpu.VMEM((1, KEY_CHUNK), F32),
            pltpu.VMEM((V_ROWS, KEY_CHUNK), F32),
            pltpu.VMEM((N_STAGE_BUFS, KEY_CHUNK, MOBA_BLOCK), F32),
            pltpu.VMEM((N_STAGE_BUFS, KEY_CHUNK, MOBA_BLOCK), BF16),
            pltpu.VMEM((N_STAGE_BUFS, 1, MOBA_BLOCK), F32),
            pltpu.VMEM((N_STAGE_BUFS, 1, MOBA_BLOCK), F32),
        ],
    )
    return pl.pallas_call(
        _attn_prompt_body,
        grid_spec=grid_spec,
        out_shape=jax.ShapeDtypeStruct((t_len, WIDTH), BF16),
        compiler_params=pltpu.CompilerParams(
            dimension_semantics=("parallel", "arbitrary"), vmem_limit_bytes=VMEM_LIMIT),
        name="attn_prompt",
    )(slopes, q, k, v)


GROUP_PAGES = 12
GROUP_BLOCKS = GROUP_PAGES // PAGES_PER_BLOCK


class _SelectStep:
    def __init__(self, t, pt_ref, q_ref, ck_ref, sel_ref, buf_ref, sem_ref, stage_ref, ksum_ref):
        self.t, self.pt_ref, self.q_ref, self.ck_ref = t, pt_ref, q_ref, ck_ref
        self.sel_ref, self.buf_ref, self.sem_ref = sel_ref, buf_ref, sem_ref
        self.stage_ref, self.ksum_ref = stage_ref, ksum_ref
        self.n_seq, self.n_pages = pt_ref.shape
        self.total_pages = self.n_seq * self.n_pages
        self.total_blocks = self.total_pages // PAGES_PER_BLOCK
        self.n_steps = -(-self.total_pages // GROUP_PAGES)
        self.active = t < self.n_steps
        self.slot = t % 2

    def _page_copy(self, flat, slot, p):
        flat = jnp.minimum(flat, self.total_pages - 1)
        page = self.pt_ref[flat // self.n_pages, flat % self.n_pages]
        return pltpu.make_async_copy(
            self.ck_ref.at[page, :, 0], self.buf_ref.at[slot, pl.ds(p * PAGE_SIZE, PAGE_SIZE)],
            self.sem_ref.at[slot])

    def _step_copies(self, t, slot):
        return [self._page_copy(t * GROUP_PAGES + p, slot, p) for p in range(GROUP_PAGES)]

    def copies(self):
        @pl.when(self.t == 0)
        def _():
            for c in self._step_copies(0, 0):
                c.start()

        @pl.when(self.t + 1 < self.n_steps)
        def _():
            for c in self._step_copies(self.t + 1, 1 - self.slot):
                c.start()

        @pl.when(self.active)
        def _():
            for c in self._step_copies(self.t, self.slot):
                c.wait()

    def block_sums(self, part):
        for n in range(part * GROUP_BLOCKS // 3, (part + 1) * GROUP_BLOCKS // 3):
            blk = self.buf_ref[self.slot, pl.ds(n * MOBA_BLOCK, MOBA_BLOCK)]
            self.stage_ref[n] = jnp.sum(blk, axis=0)

    def pick(self):
        n_full = self.n_pages // PAGES_PER_BLOCK
        for n in range(GROUP_BLOCKS):
            flat = self.t * GROUP_BLOCKS + n
            self.ksum_ref[jnp.where(flat < self.total_blocks, flat, self.total_blocks + n)] = self.stage_ref[n]

        def pick_seq(b, carry):
            ksum = self.ksum_ref[pl.ds(pl.multiple_of(b * n_full, n_full), n_full)]
            kmean = ksum / MOBA_BLOCK
            gate = jnp.sum(kmean * self.q_ref[b], axis=-1, keepdims=True)
            blk_id = lax.broadcasted_iota(jnp.int32, gate.shape, 0).astype(F32)
            lane = lax.broadcasted_iota(jnp.int32, (N_HEADS, LANES), 1)
            picks = jnp.zeros((N_HEADS, LANES), F32)
            for j in range(MOBA_TOPK):
                mx = jnp.max(gate, axis=0, keepdims=True)
                first = jnp.min(jnp.where(gate == mx, blk_id, float(n_full)), axis=0, keepdims=True)
                picks = jnp.where(lane == j, first[0], picks)
                gate = jnp.where(blk_id == first, -jnp.inf, gate)
            self.sel_ref[b] = picks.astype(jnp.int32)
            return carry

        @pl.when(self.t == self.n_steps - 1)
        def _():
            lax.fori_loop(0, self.n_seq, pick_seq, 0)


def _ffn_ln_select_body(pt_ref, x_ref, wg_ref, wu_ref, wd_ref, g_ref, b_ref, q_ref, ck_ref,
                        o_ref, sel_ref, xb_ref, buf_ref, sem_ref, stage_ref, ksum_ref):
    t = pl.program_id(0) * pl.num_programs(1) + pl.program_id(1)
    step = _SelectStep(t, pt_ref, q_ref, ck_ref, sel_ref, buf_ref, sem_ref, stage_ref, ksum_ref)
    step.copies()
    _ffn_ln_body(x_ref, wg_ref, wu_ref, wd_ref, g_ref, b_ref, o_ref, xb_ref, side_work=step.block_sums)
    step.pick()


def _ffn_ln_select(x, wg, wu, wd, g, b, page_table, q_sample, cache_k, *, tm, tf):
    rows = x.shape[0]
    n_seq, n_pages = page_table.shape
    total_blocks = n_seq * n_pages // PAGES_PER_BLOCK
    grid = (rows // tm, D_FF // tf)
    assert n_pages % PAGES_PER_BLOCK == 0 and grid[0] * grid[1] * GROUP_PAGES >= n_seq * n_pages
    once = pl.Buffered(1)
    grid_spec = pltpu.PrefetchScalarGridSpec(
        num_scalar_prefetch=1,
        grid=grid,
        in_specs=[
            pl.BlockSpec((tm, D_MODEL), lambda i, j, pt: (i, 0)),
            pl.BlockSpec((D_MODEL, tf), lambda i, j, pt: (0, j)),
            pl.BlockSpec((D_MODEL, tf), lambda i, j, pt: (0, j)),
            pl.BlockSpec((tf, D_MODEL), lambda i, j, pt: (j, 0)),
            pl.BlockSpec((1, D_MODEL), lambda i, j, pt: (0, 0)),
            pl.BlockSpec((1, D_MODEL), lambda i, j, pt: (0, 0)),
            pl.BlockSpec((n_seq, N_HEADS, HEAD_DIM), lambda i, j, pt: (0, 0, 0)),
            pl.BlockSpec(memory_space=pl.ANY),
        ],
        out_specs=[
            pl.BlockSpec((tm, D_MODEL), lambda i, j, pt: (i, 0), pipeline_mode=once),
            pl.BlockSpec((n_seq, N_HEADS, LANES), lambda i, j, pt: (0, 0, 0)),
        ],
        scratch_shapes=[
            pltpu.VMEM((tm, D_MODEL), BF16),
            pltpu.VMEM((2, GROUP_PAGES * PAGE_SIZE, N_HEADS, HEAD_DIM), F32),
            pltpu.SemaphoreType.DMA((2,)),
            pltpu.VMEM((GROUP_BLOCKS, N_HEADS, HEAD_DIM), F32),
            pltpu.VMEM((total_blocks + GROUP_BLOCKS, N_HEADS, HEAD_DIM), F32),
        ],
    )
    return pl.pallas_call(
        _ffn_ln_select_body,
        grid_spec=grid_spec,
        out_shape=[jax.ShapeDtypeStruct((rows, D_MODEL), F32),
                   jax.ShapeDtypeStruct((n_seq, N_HEADS, LANES), jnp.int32)],
        compiler_params=pltpu.CompilerParams(
            dimension_semantics=("arbitrary", "arbitrary"), vmem_limit_bytes=VMEM_LIMIT),
        name="ffn1_prompt_select",
    )(page_table, x, wg, wu, wd, g, b, q_sample.reshape(n_seq, N_HEADS, HEAD_DIM), cache_k)


def _attn_sample_body(pt_ref, sel_ref, slope_ref, q_ref, kn_ref, vn_ref, ck_ref, cv_ref, o_ref,
                      kbuf_ref, vbuf_ref, sem_ref, *, past_len):
    b = pl.program_id(0)
    n_sel = MOBA_TOPK
    slot = b % 2

    def gathers(b, slot):
        out = []
        for h in range(N_HEADS):
            for j in range(n_sel):
                for pp in range(PAGES_PER_BLOCK):
                    page = pt_ref[b, sel_ref[b, h * MOBA_TOPK + j] * PAGES_PER_BLOCK + pp]
                    rows = pl.ds((j * PAGES_PER_BLOCK + pp) * PAGE_SIZE, PAGE_SIZE)
                    out.append(pltpu.make_async_copy(
                        ck_ref.at[page, :, 0, h, :], kbuf_ref.at[slot, h, rows, :], sem_ref.at[slot, 0]))
                    out.append(pltpu.make_async_copy(
                        cv_ref.at[page, :, 0, h, :], vbuf_ref.at[slot, h, rows, :], sem_ref.at[slot, 1]))
        return out

    @pl.when(b == 0)
    def _():
        for c in gathers(0, 0):
            c.start()

    @pl.when(b + 1 < pl.num_programs(0))
    def _():
        for c in gathers(b + 1, 1 - slot):
            c.start()

    for c in gathers(b, slot):
        c.wait()

    n_keys = n_sel * MOBA_BLOCK
    lane = lax.broadcasted_iota(jnp.int32, (1, n_keys), 1)
    which = lane // MOBA_BLOCK
    offs = lane % MOBA_BLOCK
    outs = []
    for h in range(N_HEADS):
        sl = slice(h * HEAD_DIM, (h + 1) * HEAD_DIM)
        slope = slope_ref[h]
        qh = q_ref[:, sl]
        kn = kn_ref[:, sl]
        vn = vn_ref[:, sl]
        q8 = jnp.broadcast_to(qh, (8, HEAD_DIM)).astype(BF16)
        s = _dot_nt(q8, kbuf_ref[slot, h].astype(BF16))[:1] * ATTN_SCALE
        blk = jnp.zeros((1, n_keys), jnp.int32)
        for j in range(n_sel):
            blk = jnp.where(which == j, sel_ref[b, h * MOBA_TOPK + j], blk)
        dist = (past_len - (blk * MOBA_BLOCK + offs)).astype(F32)
        s = s - slope * dist
        s_own = jnp.sum(qh * kn, axis=-1, keepdims=True) * ATTN_SCALE
        mx = jnp.maximum(jnp.max(s, axis=-1, keepdims=True), s_own)
        p = jnp.exp(s - mx)
        p_own = jnp.exp(s_own - mx)
        denom = jnp.sum(p, axis=-1, keepdims=True) + p_own
        p8 = jnp.broadcast_to(p, (8, n_keys)).astype(BF16)
        pv = _dot(p8, vbuf_ref[slot, h].astype(BF16))[:1]
        outs.append((pv + p_own * vn) / denom)
    o_ref[...] = jnp.concatenate(outs, axis=-1)


def _attn_sample(page_table, sel, slopes, q, k_new, v_new, cache_k, cache_v):
    n_seq, n_pages = page_table.shape
    per_seq = pl.BlockSpec((None, 1, WIDTH), lambda b, *_: (b, 0, 0))
    grid_spec = pltpu.PrefetchScalarGridSpec(
        num_scalar_prefetch=3,
        grid=(n_seq,),
        in_specs=[
            per_seq, per_seq, per_seq,
            pl.BlockSpec(memory_space=pl.ANY),
            pl.BlockSpec(memory_space=pl.ANY),
        ],
        out_specs=per_seq,
        scratch_shapes=[
            pltpu.VMEM((2, N_HEADS, MOBA_TOPK * MOBA_BLOCK, HEAD_DIM), F32),
            pltpu.VMEM((2, N_HEADS, MOBA_TOPK * MOBA_BLOCK, HEAD_DIM), F32),
            pltpu.SemaphoreType.DMA((2, 2)),
        ],
    )
    return pl.pallas_call(
        functools.partial(_attn_sample_body, past_len=n_pages * PAGE_SIZE),
        grid_spec=grid_spec,
        out_shape=jax.ShapeDtypeStruct((n_seq, 1, WIDTH), F32),
        compiler_params=pltpu.CompilerParams(
            dimension_semantics=("arbitrary",), vmem_limit_bytes=VMEM_LIMIT),
        name="attn_sample",
    )(page_table, sel, slopes, q.reshape(n_seq, 1, WIDTH), k_new.reshape(n_seq, 1, WIDTH),
      v_new.reshape(n_seq, 1, WIDTH), cache_k, cache_v).reshape(n_seq, WIDTH)


def kernel(x_prompt, x_sample, cache_k, cache_v, page_table, ffn1_w_gate, ffn1_w_up, ffn1_w_down, ln1_g, ln1_b, w_in, a_norm_g, a_norm_b, a_spatial_w, a_spatial_b, w_out, ln2_g, ln2_b, ffn2_w_gate, ffn2_w_up, ffn2_w_down, ln3_g, ln3_b):
    assert x_prompt.shape[0] == 1 and x_sample.shape[1] == 1 and w_in.shape[0] == DEPTH == 1
    t_len = x_prompt.shape[1]
    n_seq = x_sample.shape[0]
    xp = x_prompt.reshape(t_len, D_MODEL)
    xs = x_sample.reshape(n_seq, D_MODEL)
    slopes = jnp.exp2(-8.0 * jnp.arange(1, N_HEADS + 1, dtype=F32) / N_HEADS)

    row = lambda v: v.reshape(1, -1)
    ln1, ln2, ln3 = ((row(g[0]), row(b[0])) for g, b in ((ln1_g, ln1_b), (ln2_g, ln2_b), (ln3_g, ln3_b)))
    ag, ab = row(a_norm_g[0]), row(a_norm_b[0])
    ws, bs = a_spatial_w[0], a_spatial_b[0]
    bs_exp = jnp.repeat(bs.T, HEAD_DIM, axis=1)
    w00 = row(jnp.repeat(ws[:, 0, 0], HEAD_DIM))
    b0 = row(jnp.repeat(bs[:, 0], HEAD_DIM))

    hs, w1g, w1u, w1d = _ffn_ln_sample(xs, ffn1_w_gate[0], ffn1_w_up[0], ffn1_w_down[0], *ln1,
                                       tf=256, accurate=True, name="ffn1_sample")
    qs, ks, vs, gm_s, van_s, w_in_b = _proj_sample(hs, w_in[0], ag, ab, w00, b0)
    hp, sel = _ffn_ln_select(xp, w1g, w1u, w1d, *ln1, page_table, qs, cache_k, tm=1024, tf=256)
    qp, kp, vp, gm_p = _proj_prompt(hp, w_in_b, ag, ab, ws, bs_exp, tm=512)

    sel = sel[:, :, :MOBA_TOPK].reshape(n_seq, N_HEADS * MOBA_TOPK)
    attn_s = _attn_sample(page_table, sel, slopes, qs, ks, vs, cache_k, cache_v)
    attn_p = _attn_prompt(slopes, qp, kp, vp)

    hs, w_out_b = _out_ln_sample(jnp.concatenate([attn_s, gm_s], axis=-1), hs, w_out[0], *ln2, tk=512)
    hp = _out_ln(attn_p, gm_p, hp, w_out_b, *ln2, tm=512)

    ys, w2g, w2u, w2d = _ffn_ln_sample(hs, ffn2_w_gate[0], ffn2_w_up[0], ffn2_w_down[0], *ln3,
                                       tf=256, accurate=False, name="ffn2_sample")
    yp = _ffn_ln(hp, w2g, w2u, w2d, *ln3, tm=512, tf=512, name="ffn2_prompt")

    heads = (1, N_HEADS, HEAD_DIM)
    return (yp.reshape(1, t_len, D_MODEL), ys.reshape(n_seq, 1, D_MODEL),
            kp.reshape((1, t_len) + heads), vp.reshape((1, t_len) + heads),
            ks.reshape((n_seq, 1) + heads), vs.reshape((n_seq, 1) + heads),
            van_s.reshape((n_seq, 1) + heads))
```

```python
import functools

import jax
import jax.numpy as jnp
from jax import lax
from jax.experimental import pallas as pl
from jax.experimental.pallas import tpu as pltpu

D_MODEL = 2048
HEAD_DIM = 128
N_HEADS = 8
WIDTH = N_HEADS * HEAD_DIM
CHUNK = 128
MOBA_BLOCK = 256
MOBA_TOPK = 3
PAGE_SIZE = 128
PAGES_PER_BLOCK = MOBA_BLOCK // PAGE_SIZE
D_FF = 5632
LN_EPS = 1e-5
DEPTH = 1
DN_ALPHA = (2 * DEPTH) ** 0.25
ATTN_SCALE = HEAD_DIM ** -0.5
LOG2E = 1.4426950408889634

MASKED = -1e30
MASKED_BF16 = -(2.0 ** 100)

LANES = 128
VMEM_LIMIT = 61 * 1024 * 1024

BF16 = jnp.bfloat16
F32 = jnp.float32


def _layer_norm(x, g, b):
    mu = jnp.mean(x, axis=-1, keepdims=True)
    xc = x - mu
    var = jnp.mean(xc * xc, axis=-1, keepdims=True)
    return xc * lax.rsqrt(var + LN_EPS) * g + b


def _dot(a, b):
    return jnp.dot(a, b, preferred_element_type=F32)


def _dot_nt(a, b):
    return lax.dot_general(a, b, (((1,), (1,)), ((), ())), preferred_element_type=F32)


def _ffn_ln_body(x_ref, wg_ref, wu_ref, wd_ref, g_ref, b_ref, o_ref, xb_ref, side_work=None,
                 f32_weights=False):
    j = pl.program_id(1)
    weight = (lambda ref: ref[...].astype(BF16)) if f32_weights else (lambda ref: ref[...])

    @pl.when(j == 0)
    def _():
        xb_ref[...] = x_ref[...].astype(BF16)
        o_ref[...] = jnp.zeros_like(o_ref)

    side = side_work if side_work is not None else (lambda part: None)
    xb = xb_ref[...]
    side(0)
    gate = _dot(xb, weight(wg_ref))
    side(1)
    up = _dot(xb, weight(wu_ref))
    act = gate * jax.nn.sigmoid(gate) * up
    side(2)
    o_ref[...] += _dot(act.astype(BF16), weight(wd_ref))

    @pl.when(j == pl.num_programs(1) - 1)
    def _():
        y = DN_ALPHA * x_ref[...] + 0.5 * o_ref[...]
        o_ref[...] = _layer_norm(y, g_ref[...], b_ref[...])


def _ffn_ln_pair_body(x_ref, xs_ref, wg_ref, wu_ref, wd_ref, g_ref, b_ref, o_ref, os_ref, xb_ref, xsb_ref):
    last = pl.num_programs(0) - 1
    i = pl.program_id(0)

    @pl.when(i < last)
    def _():
        _ffn_ln_body(x_ref, wg_ref, wu_ref, wd_ref, g_ref, b_ref, o_ref, xb_ref, f32_weights=True)

    @pl.when(i == last)
    def _():
        _ffn_ln_body(xs_ref, wg_ref, wu_ref, wd_ref, g_ref, b_ref, os_ref, xsb_ref, f32_weights=True)


def _ffn_ln_pair(x, xs, wg, wu, wd, g, b, *, tm, tf):
    rows, rows_s = x.shape[0], xs.shape[0]
    n_tiles = rows // tm
    tile = lambda i, j: (jnp.minimum(i, n_tiles - 1), 0)
    const = lambda i, j: (0, 0)
    return pl.pallas_call(
        _ffn_ln_pair_body,
        grid=(n_tiles + 1, D_FF // tf),
        in_specs=[
            pl.BlockSpec((tm, D_MODEL), tile),
            pl.BlockSpec((rows_s, D_MODEL), const),
            pl.BlockSpec((D_MODEL, tf), lambda i, j: (0, j)),
            pl.BlockSpec((D_MODEL, tf), lambda i, j: (0, j)),
            pl.BlockSpec((tf, D_MODEL), lambda i, j: (j, 0)),
            pl.BlockSpec((1, D_MODEL), const),
            pl.BlockSpec((1, D_MODEL), const),
        ],
        out_specs=[pl.BlockSpec((tm, D_MODEL), tile), pl.BlockSpec((rows_s, D_MODEL), const)],
        out_shape=[jax.ShapeDtypeStruct((rows, D_MODEL), F32), jax.ShapeDtypeStruct((rows_s, D_MODEL), F32)],
        scratch_shapes=[pltpu.VMEM((tm, D_MODEL), BF16), pltpu.VMEM((rows_s, D_MODEL), BF16)],
        compiler_params=pltpu.CompilerParams(
            dimension_semantics=("arbitrary", "arbitrary"), vmem_limit_bytes=VMEM_LIMIT),
        name="ffn2",
    )(x, xs, wg, wu, wd, g, b)


def _split(x):
    hi = x.astype(BF16)
    return hi, (x - hi.astype(F32)).astype(BF16)


def _dot_split(x, w):
    x_hi, x_lo = _split(x)
    w_hi, w_lo = _split(w)
    n = x.shape[0]
    both = _dot(jnp.concatenate([x_hi, x_lo], axis=0), w_hi)
    return both[:n] + (both[n:] + _dot(x_hi, w_lo)), w_hi


def _dot_cast(x, w):
    w_hi = w.astype(BF16)
    return _dot(x.astype(BF16), w_hi), w_hi


def _ffn_ln_sample_body(x_ref, wg_ref, wu_ref, wd_ref, g_ref, b_ref,
                        o_ref, wgb_ref, wub_ref, wdb_ref, acc_ref):
    j = pl.program_id(0)
    dot = _dot_split

    @pl.when(j == 0)
    def _():
        acc_ref[...] = jnp.zeros_like(acc_ref)

    x = x_ref[...]
    gate, wgb_ref[...] = dot(x, wg_ref[...])
    up, wub_ref[...] = dot(x, wu_ref[...])
    down, wdb_ref[...] = dot(gate * jax.nn.sigmoid(gate) * up, wd_ref[...])
    acc_ref[...] += down

    @pl.when(j == pl.num_programs(0) - 1)
    def _():
        y = DN_ALPHA * x + 0.5 * acc_ref[...]
        o_ref[...] = _layer_norm(y, g_ref[...], b_ref[...])


def _ffn_ln_sample(x, wg, wu, wd, g, b, *, tf):
    rows = x.shape[0]
    whole = lambda j: (0, 0)
    cols = pl.BlockSpec((D_MODEL, tf), lambda j: (0, j))
    rws = pl.BlockSpec((tf, D_MODEL), lambda j: (j, 0))
    return pl.pallas_call(
        _ffn_ln_sample_body,
        grid=(D_FF // tf,),
        in_specs=[pl.BlockSpec((rows, D_MODEL), whole), cols, cols, rws,
                  pl.BlockSpec((1, D_MODEL), whole), pl.BlockSpec((1, D_MODEL), whole)],
        out_specs=[pl.BlockSpec((rows, D_MODEL), whole), cols, cols, rws],
        out_shape=[jax.ShapeDtypeStruct((rows, D_MODEL), F32),
                   jax.ShapeDtypeStruct(wg.shape, BF16),
                   jax.ShapeDtypeStruct(wu.shape, BF16),
                   jax.ShapeDtypeStruct(wd.shape, BF16)],
        scratch_shapes=[pltpu.VMEM((rows, D_MODEL), F32)],
        compiler_params=pltpu.CompilerParams(
            dimension_semantics=("arbitrary",), vmem_limit_bytes=VMEM_LIMIT),
        name="ffn1_sample",
    )(x, wg, wu, wd, g, b)


def _head_layer_norm(va, g, b):
    parts = []
    for h in range(N_HEADS):
        sl = slice(h * HEAD_DIM, (h + 1) * HEAD_DIM)
        parts.append(_layer_norm(va[:, sl], g[:, sl], b[:, sl]))
    return jnp.concatenate(parts, axis=-1)


def _proj_prompt_body(x_ref, w_ref, ag_ref, ab_ref, ws_ref, bs_ref,
                      q_ref, k_ref, v_ref, gm_ref, xb_ref, u_ref):
    j = pl.program_id(1)

    @pl.when(j == 0)
    def _():
        xb_ref[...] = x_ref[...].astype(BF16)

    def project():
        return _dot(xb_ref[...], w_ref[...])

    @pl.when(j == 0)
    def _():
        q_ref[...] = (project() * (ATTN_SCALE * LOG2E)).astype(BF16)

    @pl.when(j == 1)
    def _():
        k_ref[...] = project()

    @pl.when(j == 2)
    def _():
        v_ref[...] = project()

    @pl.when(j == 3)
    def _():
        u_ref[...] = project()

    @pl.when(j == 4)
    def _():
        p = project()
        van = _head_layer_norm(p, ag_ref[...], ab_ref[...]).astype(BF16)
        row = lax.broadcasted_iota(jnp.int32, (CHUNK, CHUNK), 0)
        col = lax.broadcasted_iota(jnp.int32, (CHUNK, CHUNK), 1)
        causal = row >= col
        n_chunks = p.shape[0] // CHUNK
        for h in range(N_HEADS):
            sl = slice(h * HEAD_DIM, (h + 1) * HEAD_DIM)
            w_h = jnp.where(causal, ws_ref[h], 0.0).astype(BF16)
            for c in range(n_chunks):
                rs = slice(c * CHUNK, (c + 1) * CHUNK)
                mixed = _dot(w_h, van[rs, sl]) + bs_ref[:, sl]
                gm_ref[rs, sl] = (u_ref[rs, sl] * mixed).astype(BF16)


def _proj_prompt(x, w_in, ag, ab, ws, bs_exp, *, tm):
    rows = x.shape[0]
    row_blk = lambda i, j: (i, 0)
    const2 = lambda i, j: (0, 0)
    return pl.pallas_call(
        _proj_prompt_body,
        grid=(rows // tm, 5),
        in_specs=[
            pl.BlockSpec((tm, D_MODEL), row_blk),
            pl.BlockSpec((D_MODEL, WIDTH), lambda i, j: (0, j)),
            pl.BlockSpec((1, WIDTH), const2),
            pl.BlockSpec((1, WIDTH), const2),
            pl.BlockSpec((N_HEADS, CHUNK, CHUNK), lambda i, j: (0, 0, 0)),
            pl.BlockSpec((CHUNK, WIDTH), const2),
        ],
        out_specs=[
            pl.BlockSpec((tm, WIDTH), row_blk),
            pl.BlockSpec((tm, WIDTH), row_blk),
            pl.BlockSpec((tm, WIDTH), row_blk),
            pl.BlockSpec((tm, WIDTH), row_blk),
        ],
        out_shape=[
            jax.ShapeDtypeStruct((rows, WIDTH), BF16),
            jax.ShapeDtypeStruct((rows, WIDTH), F32),
            jax.ShapeDtypeStruct((rows, WIDTH), F32),
            jax.ShapeDtypeStruct((rows, WIDTH), BF16),
        ],
        scratch_shapes=[pltpu.VMEM((tm, D_MODEL), BF16), pltpu.VMEM((tm, WIDTH), F32)],
        compiler_params=pltpu.CompilerParams(
            dimension_semantics=("parallel", "arbitrary"), vmem_limit_bytes=VMEM_LIMIT),
        name="proj_prompt",
    )(x, w_in, ag, ab, ws, bs_exp)


def _proj_sample_body(x_ref, w_ref, ag_ref, ab_ref, w00_ref, b0_ref,
                      q_ref, k_ref, v_ref, gm_ref, van_ref, wb_ref, u_ref):
    j = pl.program_id(0)
    p, w_hi = _dot_split(x_ref[...], w_ref[...])
    wb_ref[...] = w_hi

    @pl.when(j == 0)
    def _():
        q_ref[...] = p

    @pl.when(j == 1)
    def _():
        k_ref[...] = p

    @pl.when(j == 2)
    def _():
        v_ref[...] = p

    @pl.when(j == 3)
    def _():
        u_ref[...] = p

    @pl.when(j == 4)
    def _():
        van = _head_layer_norm(p, ag_ref[...], ab_ref[...])
        van_ref[...] = van
        gm_ref[...] = u_ref[...] * (w00_ref[...] * van + b0_ref[...])


def _proj_sample(x, w_in, ag, ab, w00, b0):
    rows = x.shape[0]
    whole = lambda j: (0, 0)
    w_cols = lambda j: (0, j)
    return pl.pallas_call(
        _proj_sample_body,
        grid=(5,),
        in_specs=[
            pl.BlockSpec((rows, D_MODEL), whole),
            pl.BlockSpec((D_MODEL, WIDTH), w_cols),
            pl.BlockSpec((1, WIDTH), whole),
            pl.BlockSpec((1, WIDTH), whole),
            pl.BlockSpec((1, WIDTH), whole),
            pl.BlockSpec((1, WIDTH), whole),
        ],
        out_specs=[pl.BlockSpec((rows, WIDTH), whole)] * 5 + [pl.BlockSpec((D_MODEL, WIDTH), w_cols)],
        out_shape=[
            jax.ShapeDtypeStruct((rows, WIDTH), F32),
            jax.ShapeDtypeStruct((rows, WIDTH), F32),
            jax.ShapeDtypeStruct((rows, WIDTH), F32),
            jax.ShapeDtypeStruct((rows, WIDTH), F32),
            jax.ShapeDtypeStruct((rows, WIDTH), F32),
            jax.ShapeDtypeStruct(w_in.shape, BF16),
        ],
        scratch_shapes=[pltpu.VMEM((rows, WIDTH), F32)],
        compiler_params=pltpu.CompilerParams(
            dimension_semantics=("arbitrary",), vmem_limit_bytes=VMEM_LIMIT),
        name="proj_sample",
    )(x, w_in, ag, ab, w00, b0)


def _out_ln_body(a_ref, m_ref, h_ref, wa_ref, wm_ref, g_ref, b_ref, o_ref):
    half = a_ref.shape[0] // 2
    mixes = []
    for rows in (slice(0, half), slice(half, 2 * half)):
        mixes.append(_dot(a_ref[rows, :], wa_ref[...]) + _dot(m_ref[rows, :], wm_ref[...]))
    for rows, mix in zip((slice(0, half), slice(half, 2 * half)), mixes):
        o_ref[rows, :] = _layer_norm(DN_ALPHA * h_ref[rows, :] + mix, g_ref[...], b_ref[...])


def _out_ln(attn, gm, h, w_out, g, b, *, tm):
    rows = h.shape[0]
    row_blk = lambda i: (i, 0)
    const = lambda i: (0, 0)
    return pl.pallas_call(
        _out_ln_body,
        grid=(rows // tm,),
        in_specs=[
            pl.BlockSpec((tm, WIDTH), row_blk),
            pl.BlockSpec((tm, WIDTH), row_blk),
            pl.BlockSpec((tm, D_MODEL), row_blk),
            pl.BlockSpec((WIDTH, D_MODEL), const),
            pl.BlockSpec((WIDTH, D_MODEL), lambda i: (1, 0)),
            pl.BlockSpec((1, D_MODEL), const),
            pl.BlockSpec((1, D_MODEL), const),
        ],
        out_specs=pl.BlockSpec((tm, D_MODEL), row_blk),
        out_shape=jax.ShapeDtypeStruct((rows, D_MODEL), F32),
        compiler_params=pltpu.CompilerParams(
            dimension_semantics=("parallel",), vmem_limit_bytes=VMEM_LIMIT),
        name="out_prompt",
    )(attn, gm, h, w_out, w_out, g, b)


def _out_ln_sample_body(x_ref, h_ref, w_ref, g_ref, b_ref, o_ref, wb_ref, acc_ref):
    j = pl.program_id(0)

    @pl.when(j == 0)
    def _():
        acc_ref[...] = jnp.zeros_like(acc_ref)

    part, wb_ref[...] = _dot_cast(x_ref[...], w_ref[...])
    acc_ref[...] += part

    @pl.when(j == pl.num_programs(0) - 1)
    def _():
        o_ref[...] = _layer_norm(DN_ALPHA * h_ref[...] + acc_ref[...], g_ref[...], b_ref[...])


def _out_ln_sample(mix, h, w_out, g, b, *, tk):
    rows = h.shape[0]
    whole = lambda j: (0, 0)
    slab = pl.BlockSpec((tk, D_MODEL), lambda j: (j, 0))
    return pl.pallas_call(
        _out_ln_sample_body,
        grid=(w_out.shape[0] // tk,),
        in_specs=[pl.BlockSpec((rows, tk), lambda j: (0, j)), pl.BlockSpec((rows, D_MODEL), whole), slab,
                  pl.BlockSpec((1, D_MODEL), whole), pl.BlockSpec((1, D_MODEL), whole)],
        out_specs=[pl.BlockSpec((rows, D_MODEL), whole), slab],
        out_shape=[jax.ShapeDtypeStruct((rows, D_MODEL), F32), jax.ShapeDtypeStruct(w_out.shape, BF16)],
        scratch_shapes=[pltpu.VMEM((rows, D_MODEL), F32)],
        compiler_params=pltpu.CompilerParams(
            dimension_semantics=("arbitrary",), vmem_limit_bytes=VMEM_LIMIT),
        name="out_sample",
    )(mix, h, w_out, g, b)


CHUNK_BLOCKS = 4
KEY_CHUNK = CHUNK_BLOCKS * MOBA_BLOCK
N_SEL_ROWS = 32
N_POS_ROWS = 16
SPLIT = 3
V_ROWS = HEAD_DIM + 16
STAGE_LOOKAHEAD = 3
N_STAGE_BUFS = 4


def _top_blocks_t(gate, n_valid):
    blk = lax.broadcasted_iota(jnp.int32, gate.shape, 0).astype(F32)
    valid = blk < n_valid
    g = jnp.where(valid, gate, -jnp.inf)
    sel = jnp.zeros(gate.shape, dtype=jnp.bool_)
    for _ in range(MOBA_TOPK):
        mx = jnp.max(g, axis=0, keepdims=True)
        first = jnp.min(jnp.where(g == mx, blk, float(gate.shape[0])), axis=0, keepdims=True)
        hit = blk == first
        sel = sel | (hit & valid)
        g = jnp.where(hit, -jnp.inf, g)
    return sel


def _attn_prompt_body(slope_ref, q_ref, k_ref, v_ref, o_ref,
                      ka_ref, vt_ref, km_ref, pos_ref, qa_ref, m_ref, acc_ref,
                      u_ref, p_ref, mx_ref, alpha_ref):
    h = pl.program_id(0)
    i = pl.program_id(1)
    n_blocks = k_ref.shape[0] // MOBA_BLOCK
    slope2 = slope_ref[h] * LOG2E

    @pl.when(i == 0)
    def _():
        km_ref[...] = jnp.zeros_like(km_ref)
        lane = lax.broadcasted_iota(jnp.int32, (MOBA_BLOCK, LANES), 1)
        offs = lax.broadcasted_iota(jnp.int32, (MOBA_BLOCK, LANES), 0).astype(F32)
        tail = lax.broadcasted_iota(jnp.int32, (V_ROWS - HEAD_DIM, MOBA_BLOCK), 0)
        ones_row = jnp.where(tail == 0, 1.0, 0.0).astype(BF16)

        def prep(n, carry):
            rows = pl.ds(pl.multiple_of(n * MOBA_BLOCK, MOBA_BLOCK), MOBA_BLOCK)
            kblk = k_ref[rows, :]
            in_chunk = lax.convert_element_type(n % CHUNK_BLOCKS, F32)
            aug = jnp.where(lane == n, 1.0, 0.0)
            aug = jnp.where((lane >= N_SEL_ROWS) & (lane < N_SEL_ROWS + SPLIT), in_chunk, aug)
            aug = jnp.where((lane >= N_SEL_ROWS + SPLIT) & (lane < N_SEL_ROWS + 2 * SPLIT), offs, aug)
            ka_ref[rows, :HEAD_DIM] = kblk.astype(BF16)
            ka_ref[rows, HEAD_DIM:] = aug.astype(BF16)
            vt_ref[n, :HEAD_DIM] = v_ref[rows, :].T.astype(BF16)
            vt_ref[n, HEAD_DIM:] = ones_row
            km_ref[pl.ds(n, 1), :] = jnp.mean(kblk, axis=0, keepdims=True)
            return carry

        lax.fori_loop(0, n_blocks, prep, 0)
        r = lax.broadcasted_iota(jnp.int32, pos_ref.shape, 0)
        rest = jnp.full(pos_ref.shape, slope2, F32)
        rows = jnp.zeros(pos_ref.shape, F32)
        for t in range(SPLIT):
            term = rest.astype(BF16).astype(F32)
            rest = rest - term
            rows = jnp.where(r == t, term * MOBA_BLOCK, jnp.where(r == t + SPLIT, term, rows))
        pos_ref[...] = rows.astype(BF16)

    qt = q_ref[...].astype(F32).T.astype(BF16)
    gate = _dot(km_ref[...].astype(BF16), qt)
    own = i * CHUNK_BLOCKS + lax.broadcasted_iota(jnp.int32, (1, KEY_CHUNK), 1) // MOBA_BLOCK
    sel = _top_blocks_t(gate, own.astype(F32))
    blk = lax.broadcasted_iota(jnp.int32, gate.shape, 0)
    qa_ref[:HEAD_DIM, :] = qt
    qa_ref[HEAD_DIM:HEAD_DIM + N_SEL_ROWS, :] = jnp.where(sel | (blk == own), 0.0, MASKED_BF16).astype(BF16)
    qa_ref[HEAD_DIM + N_SEL_ROWS:HEAD_DIM + N_SEL_ROWS + N_POS_ROWS, :] = jnp.concatenate(
        [pos_ref[...]] * CHUNK_BLOCKS, axis=1)
    qa_ref[HEAD_DIM + N_SEL_ROWS + N_POS_ROWS:, :] = jnp.zeros(
        (HEAD_DIM - N_SEL_ROWS - N_POS_ROWS, KEY_CHUNK), BF16)

    half = KEY_CHUNK // 2
    key = lax.broadcasted_iota(jnp.int32, (half, MOBA_BLOCK), 0)
    qry = lax.broadcasted_iota(jnp.int32, (half, MOBA_BLOCK), 1)

    def stage_logits(c, j, buf, diagonal):
        cols = slice(j * MOBA_BLOCK, (j + 1) * MOBA_BLOCK)
        mx = None
        for s in range(2):
            rows = pl.ds(pl.multiple_of(c * KEY_CHUNK + s * half, half), half)
            u = _dot(ka_ref[rows, :], qa_ref[:, cols])
            if diagonal:
                u = jnp.where(key + s * half <= qry + j * MOBA_BLOCK, u, MASKED)
            u_ref[buf, s * half:(s + 1) * half, :] = u
            hmax = jnp.max(u, axis=0, keepdims=True)
            mx = hmax if mx is None else jnp.maximum(mx, hmax)
        mx_ref[buf] = mx

    def stage_softmax(c, j, buf, diagonal):
        cols = slice(j * MOBA_BLOCK, (j + 1) * MOBA_BLOCK)
        blocks_ahead = (c - i) * CHUNK_BLOCKS - j
        off = slope2 * lax.convert_element_type(blocks_ahead * MOBA_BLOCK, F32)
        m_new = mx_ref[buf] + off
        if not diagonal:
            m_old = m_ref[:, cols]
            m_new = jnp.maximum(m_old, m_new)
            alpha_ref[buf] = jnp.exp2(m_old - m_new)
        p_ref[buf] = jnp.exp2((u_ref[buf] - (m_new - off)).astype(BF16))
        m_ref[:, cols] = m_new

    def stage_values(c, j, buf, diagonal):
        cols = slice(j * MOBA_BLOCK, (j + 1) * MOBA_BLOCK)
        parts = []
        for s in range(2):
            pv = None
            for n in range(s * CHUNK_BLOCKS // 2, (s + 1) * CHUNK_BLOCKS // 2):
                d = _dot(vt_ref[c * CHUNK_BLOCKS + n], p_ref[buf, n * MOBA_BLOCK:(n + 1) * MOBA_BLOCK, :])
                pv = d if pv is None else pv + d
            parts.append(pv)
        pv = parts[0] + parts[1]
        acc_ref[:, cols] = pv if diagonal else alpha_ref[buf] * acc_ref[:, cols] + pv

    def sweep(chunks, diagonal):
        stages = [(c, j) for c in chunks for j in range(CHUNK_BLOCKS)]
        for s, (c, j) in enumerate(stages[:STAGE_LOOKAHEAD]):
            stage_logits(c, j, s % N_STAGE_BUFS, diagonal)
        for s, (c, j) in enumerate(stages):
            if s + STAGE_LOOKAHEAD < len(stages):
                nc, nj = stages[s + STAGE_LOOKAHEAD]
                stage_logits(nc, nj, (s + STAGE_LOOKAHEAD) % N_STAGE_BUFS, diagonal)
            stage_softmax(c, j, s % N_STAGE_BUFS, diagonal)
            stage_values(c, j, s % N_STAGE_BUFS, diagonal)

    sweep([i], True)

    def past_pair(cc, carry):
        sweep([2 * cc, 2 * cc + 1], False)
        return carry

    lax.fori_loop(0, i // 2, past_pair, 0)

    @pl.when(i % 2 == 1)
    def _():
        sweep([i - 1], False)

    o_ref[...] = (acc_ref[:HEAD_DIM] / acc_ref[HEAD_DIM:HEAD_DIM + 1]).T.astype(o_ref.dtype)


def _attn_prompt(slopes, q, k, v):
    t_len = q.shape[0]
    n_blocks = t_len // MOBA_BLOCK
    assert n_blocks <= N_SEL_ROWS and t_len % KEY_CHUNK == 0
    grid_spec = pltpu.PrefetchScalarGridSpec(
        num_scalar_prefetch=1,
        grid=(N_HEADS, t_len // KEY_CHUNK),
        in_specs=[
            pl.BlockSpec((KEY_CHUNK, HEAD_DIM), lambda h, i, s: (i, h)),
            pl.BlockSpec((t_len, HEAD_DIM), lambda h, i, s: (0, h)),
            pl.BlockSpec((t_len, HEAD_DIM), lambda h, i, s: (0, h)),
        ],
        out_specs=pl.BlockSpec((KEY_CHUNK, HEAD_DIM), lambda h, i, s: (i, h)),
        scratch_shapes=[
            pltpu.VMEM((t_len, 2 * HEAD_DIM), BF16),
            pltpu.VMEM((n_blocks, V_ROWS, MOBA_BLOCK), BF16),
            pltpu.VMEM((N_SEL_ROWS, HEAD_DIM), F32),
            pltpu.VMEM((N_POS_ROWS, MOBA_BLOCK), BF16),
            pltpu.VMEM((2 * HEAD_DIM, KEY_CHUNK), BF16),
            pltpu.VMEM((1, KEY_CHUNK), F32),
            pltpu.VMEM((V_ROWS, KEY_CHUNK), F32),
            pltpu.VMEM((N_STAGE_BUFS, KEY_CHUNK, MOBA_BLOCK), F32),
            pltpu.VMEM((N_STAGE_BUFS, KEY_CHUNK, MOBA_BLOCK), BF16),
            pltpu.VMEM((N_STAGE_BUFS, 1, MOBA_BLOCK), F32),
            pltpu.VMEM((N_STAGE_BUFS, 1, MOBA_BLOCK), F32),
        ],
    )
    return pl.pallas_call(
        _attn_prompt_body,
        grid_spec=grid_spec,
        out_shape=jax.ShapeDtypeStruct((t_len, WIDTH), BF16),
        compiler_params=pltpu.CompilerParams(
            dimension_semantics=("parallel", "arbitrary"), vmem_limit_bytes=VMEM_LIMIT),
        name="attn_prompt",
    )(slopes, q, k, v)


GROUP_PAGES = 12
GROUP_BLOCKS = GROUP_PAGES // PAGES_PER_BLOCK


class _SelectStep:
    def __init__(self, t, pt_ref, q_ref, ck_ref, sel_ref, buf_ref, sem_ref, stage_ref, ksum_ref):
        self.t, self.pt_ref, self.q_ref, self.ck_ref = t, pt_ref, q_ref, ck_ref
        self.sel_ref, self.buf_ref, self.sem_ref = sel_ref, buf_ref, sem_ref
        self.stage_ref, self.ksum_ref = stage_ref, ksum_ref
        self.n_seq, self.n_pages = pt_ref.shape
        self.total_pages = self.n_seq * self.n_pages
        self.total_blocks = self.total_pages // PAGES_PER_BLOCK
        self.n_steps = -(-self.total_pages // GROUP_PAGES)
        self.active = t < self.n_steps
        self.slot = t % 2

    def _page_copy(self, flat, slot, p):
        flat = jnp.minimum(flat, self.total_pages - 1)
        page = self.pt_ref[flat // self.n_pages, flat % self.n_pages]
        return pltpu.make_async_copy(
            self.ck_ref.at[page, :, 0], self.buf_ref.at[slot, pl.ds(p * PAGE_SIZE, PAGE_SIZE)],
            self.sem_ref.at[slot])

    def _step_copies(self, t, slot):
        return [self._page_copy(t * GROUP_PAGES + p, slot, p) for p in range(GROUP_PAGES)]

    def copies(self):
        @pl.when(self.t == 0)
        def _():
            for c in self._step_copies(0, 0):
                c.start()

        @pl.when(self.t + 1 < self.n_steps)
        def _():
            for c in self._step_copies(self.t + 1, 1 - self.slot):
                c.start()

        @pl.when(self.active)
        def _():
            for c in self._step_copies(self.t, self.slot):
                c.wait()

    def block_sums(self, part):
        for n in range(part * GROUP_BLOCKS // 3, (part + 1) * GROUP_BLOCKS // 3):
            blk = self.buf_ref[self.slot, pl.ds(n * MOBA_BLOCK, MOBA_BLOCK)]
            self.stage_ref[n] = jnp.sum(blk, axis=0)

    def pick(self):
        n_full = self.n_pages // PAGES_PER_BLOCK
        for n in range(GROUP_BLOCKS):
            flat = self.t * GROUP_BLOCKS + n
            self.ksum_ref[jnp.where(flat < self.total_blocks, flat, self.total_blocks + n)] = self.stage_ref[n]

        def pick_seq(b, carry):
            ksum = self.ksum_ref[pl.ds(pl.multiple_of(b * n_full, n_full), n_full)]
            kmean = ksum / MOBA_BLOCK
            gate = jnp.sum(kmean * self.q_ref[b], axis=-1, keepdims=True)
            blk_id = lax.broadcasted_iota(jnp.int32, gate.shape, 0).astype(F32)
            lane = lax.broadcasted_iota(jnp.int32, (N_HEADS, LANES), 1)
            picks = jnp.zeros((N_HEADS, LANES), F32)
            for j in range(MOBA_TOPK):
                mx = jnp.max(gate, axis=0, keepdims=True)
                first = jnp.min(jnp.where(gate == mx, blk_id, float(n_full)), axis=0, keepdims=True)
                picks = jnp.where(lane == j, first[0], picks)
                gate = jnp.where(blk_id == first, -jnp.inf, gate)
            self.sel_ref[b] = picks.astype(jnp.int32)
            return carry

        @pl.when(self.t == self.n_steps - 1)
        def _():
            lax.fori_loop(0, self.n_seq, pick_seq, 0)


def _ffn_ln_select_body(pt_ref, x_ref, wg_ref, wu_ref, wd_ref, g_ref, b_ref, q_ref, ck_ref,
                        o_ref, sel_ref, xb_ref, buf_ref, sem_ref, stage_ref, ksum_ref):
    t = pl.program_id(0) * pl.num_programs(1) + pl.program_id(1)
    step = _SelectStep(t, pt_ref, q_ref, ck_ref, sel_ref, buf_ref, sem_ref, stage_ref, ksum_ref)
    step.copies()
    _ffn_ln_body(x_ref, wg_ref, wu_ref, wd_ref, g_ref, b_ref, o_ref, xb_ref, side_work=step.block_sums)
    step.pick()


def _ffn_ln_select(x, wg, wu, wd, g, b, page_table, q_sample, cache_k, *, tm, tf):
    rows = x.shape[0]
    n_seq, n_pages = page_table.shape
    total_blocks = n_seq * n_pages // PAGES_PER_BLOCK
    grid = (rows // tm, D_FF // tf)
    assert n_pages % PAGES_PER_BLOCK == 0 and grid[0] * grid[1] * GROUP_PAGES >= n_seq * n_pages
    once = pl.Buffered(1)
    grid_spec = pltpu.PrefetchScalarGridSpec(
        num_scalar_prefetch=1,
        grid=grid,
        in_specs=[
            pl.BlockSpec((tm, D_MODEL), lambda i, j, pt: (i, 0)),
            pl.BlockSpec((D_MODEL, tf), lambda i, j, pt: (0, j)),
            pl.BlockSpec((D_MODEL, tf), lambda i, j, pt: (0, j)),
            pl.BlockSpec((tf, D_MODEL), lambda i, j, pt: (j, 0)),
            pl.BlockSpec((1, D_MODEL), lambda i, j, pt: (0, 0)),
            pl.BlockSpec((1, D_MODEL), lambda i, j, pt: (0, 0)),
            pl.BlockSpec((n_seq, N_HEADS, HEAD_DIM), lambda i, j, pt: (0, 0, 0)),
            pl.BlockSpec(memory_space=pl.ANY),
        ],
        out_specs=[
            pl.BlockSpec((tm, D_MODEL), lambda i, j, pt: (i, 0), pipeline_mode=once),
            pl.BlockSpec((n_seq, N_HEADS, LANES), lambda i, j, pt: (0, 0, 0)),
        ],
        scratch_shapes=[
            pltpu.VMEM((tm, D_MODEL), BF16),
            pltpu.VMEM((2, GROUP_PAGES * PAGE_SIZE, N_HEADS, HEAD_DIM), F32),
            pltpu.SemaphoreType.DMA((2,)),
            pltpu.VMEM((GROUP_BLOCKS, N_HEADS, HEAD_DIM), F32),
            pltpu.VMEM((total_blocks + GROUP_BLOCKS, N_HEADS, HEAD_DIM), F32),
        ],
    )
    return pl.pallas_call(
        _ffn_ln_select_body,
        grid_spec=grid_spec,
        out_shape=[jax.ShapeDtypeStruct((rows, D_MODEL), F32),
                   jax.ShapeDtypeStruct((n_seq, N_HEADS, LANES), jnp.int32)],
        compiler_params=pltpu.CompilerParams(
            dimension_semantics=("arbitrary", "arbitrary"), vmem_limit_bytes=VMEM_LIMIT),
        name="ffn1_prompt_select",
    )(page_table, x, wg, wu, wd, g, b, q_sample.reshape(n_seq, N_HEADS, HEAD_DIM), cache_k)


def _attn_sample_body(pt_ref, sel_ref, slope_ref, q_ref, kn_ref, vn_ref, ck_ref, cv_ref, o_ref,
                      kbuf_ref, vbuf_ref, sem_ref, *, past_len):
    b = pl.program_id(0)
    n_sel = MOBA_TOPK
    slot = b % 2

    def gathers(b, slot):
        out = []
        for h in range(N_HEADS):
            for j in range(n_sel):
                for pp in range(PAGES_PER_BLOCK):
                    page = pt_ref[b, sel_ref[b, h * MOBA_TOPK + j] * PAGES_PER_BLOCK + pp]
                    rows = pl.ds((j * PAGES_PER_BLOCK + pp) * PAGE_SIZE, PAGE_SIZE)
                    out.append(pltpu.make_async_copy(
                        ck_ref.at[page, :, 0, h, :], kbuf_ref.at[slot, h, rows, :], sem_ref.at[slot, 0]))
                    out.append(pltpu.make_async_copy(
                        cv_ref.at[page, :, 0, h, :], vbuf_ref.at[slot, h, rows, :], sem_ref.at[slot, 1]))
        return out

    @pl.when(b == 0)
    def _():
        for c in gathers(0, 0):
            c.start()

    @pl.when(b + 1 < pl.num_programs(0))
    def _():
        for c in gathers(b + 1, 1 - slot):
            c.start()

    for c in gathers(b, slot):
        c.wait()

    n_keys = n_sel * MOBA_BLOCK
    lane = lax.broadcasted_iota(jnp.int32, (1, n_keys), 1)
    which = lane // MOBA_BLOCK
    offs = lane % MOBA_BLOCK
    outs = []
    for h in range(N_HEADS):
        sl = slice(h * HEAD_DIM, (h + 1) * HEAD_DIM)
        slope = slope_ref[h]
        qh = q_ref[:, sl]
        kn = kn_ref[:, sl]
        vn = vn_ref[:, sl]
        q8 = jnp.broadcast_to(qh, (8, HEAD_DIM)).astype(BF16)
        s = _dot_nt(q8, kbuf_ref[slot, h].astype(BF16))[:1] * ATTN_SCALE
        blk = jnp.zeros((1, n_keys), jnp.int32)
        for j in range(n_sel):
            blk = jnp.where(which == j, sel_ref[b, h * MOBA_TOPK + j], blk)
        dist = (past_len - (blk * MOBA_BLOCK + offs)).astype(F32)
        s = s - slope * dist
        s_own = jnp.sum(qh * kn, axis=-1, keepdims=True) * ATTN_SCALE
        mx = jnp.maximum(jnp.max(s, axis=-1, keepdims=True), s_own)
        p = jnp.exp(s - mx)
        p_own = jnp.exp(s_own - mx)
        denom = jnp.sum(p, axis=-1, keepdims=True) + p_own
        p8 = jnp.broadcast_to(p, (8, n_keys)).astype(BF16)
        pv = _dot(p8, vbuf_ref[slot, h].astype(BF16))[:1]
        outs.append((pv + p_own * vn) / denom)
    o_ref[...] = jnp.concatenate(outs, axis=-1)


def _attn_sample(page_table, sel, slopes, q, k_new, v_new, cache_k, cache_v):
    n_seq, n_pages = page_table.shape
    per_seq = pl.BlockSpec((None, 1, WIDTH), lambda b, *_: (b, 0, 0))
    grid_spec = pltpu.PrefetchScalarGridSpec(
        num_scalar_prefetch=3,
        grid=(n_seq,),
        in_specs=[
            per_seq, per_seq, per_seq,
            pl.BlockSpec(memory_space=pl.ANY),
            pl.BlockSpec(memory_space=pl.ANY),
        ],
        out_specs=per_seq,
        scratch_shapes=[
            pltpu.VMEM((2, N_HEADS, MOBA_TOPK * MOBA_BLOCK, HEAD_DIM), F32),
            pltpu.VMEM((2, N_HEADS, MOBA_TOPK * MOBA_BLOCK, HEAD_DIM), F32),
            pltpu.SemaphoreType.DMA((2, 2)),
        ],
    )
    return pl.pallas_call(
        functools.partial(_attn_sample_body, past_len=n_pages * PAGE_SIZE),
        grid_spec=grid_spec,
        out_shape=jax.ShapeDtypeStruct((n_seq, 1, WIDTH), F32),
        compiler_params=pltpu.CompilerParams(
            dimension_semantics=("arbitrary",), vmem_limit_bytes=VMEM_LIMIT),
        name="attn_sample",
    )(page_table, sel, slopes, q.reshape(n_seq, 1, WIDTH), k_new.reshape(n_seq, 1, WIDTH),
      v_new.reshape(n_seq, 1, WIDTH), cache_k, cache_v).reshape(n_seq, WIDTH)


def kernel(x_prompt, x_sample, cache_k, cache_v, page_table, ffn1_w_gate, ffn1_w_up, ffn1_w_down, ln1_g, ln1_b, w_in, a_norm_g, a_norm_b, a_spatial_w, a_spatial_b, w_out, ln2_g, ln2_b, ffn2_w_gate, ffn2_w_up, ffn2_w_down, ln3_g, ln3_b):
    assert x_prompt.shape[0] == 1 and x_sample.shape[1] == 1 and w_in.shape[0] == DEPTH == 1
    t_len = x_prompt.shape[1]
    n_seq = x_sample.shape[0]
    xp = x_prompt.reshape(t_len, D_MODEL)
    xs = x_sample.reshape(n_seq, D_MODEL)
    slopes = jnp.exp2(-8.0 * jnp.arange(1, N_HEADS + 1, dtype=F32) / N_HEADS)

    row = lambda v: v.reshape(1, -1)
    ln1, ln2, ln3 = ((row(g[0]), row(b[0])) for g, b in ((ln1_g, ln1_b), (ln2_g, ln2_b), (ln3_g, ln3_b)))
    ag, ab = row(a_norm_g[0]), row(a_norm_b[0])
    ws, bs = a_spatial_w[0], a_spatial_b[0]
    bs_exp = jnp.repeat(bs.T, HEAD_DIM, axis=1)
    w00 = row(jnp.repeat(ws[:, 0, 0], HEAD_DIM))
    b0 = row(jnp.repeat(bs[:, 0], HEAD_DIM))

    hs, w1g, w1u, w1d = _ffn_ln_sample(xs, ffn1_w_gate[0], ffn1_w_up[0], ffn1_w_down[0], *ln1, tf=256)
    qs, ks, vs, gm_s, van_s, w_in_b = _proj_sample(hs, w_in[0], ag, ab, w00, b0)
    hp, sel = _ffn_ln_select(xp, w1g, w1u, w1d, *ln1, page_table, qs, cache_k, tm=1024, tf=256)
    qp, kp, vp, gm_p = _proj_prompt(hp, w_in_b, ag, ab, ws, bs_exp, tm=512)

    sel = sel[:, :, :MOBA_TOPK].reshape(n_seq, N_HEADS * MOBA_TOPK)
    attn_s = _attn_sample(page_table, sel, slopes, qs, ks, vs, cache_k, cache_v)
    attn_p = _attn_prompt(slopes, qp, kp, vp)

    hs, w_out_b = _out_ln_sample(jnp.concatenate([attn_s, gm_s], axis=-1), hs, w_out[0], *ln2, tk=512)
    hp = _out_ln(attn_p, gm_p, hp, w_out_b, *ln2, tm=512)

    yp, ys = _ffn_ln_pair(hp, hs, ffn2_w_gate[0], ffn2_w_up[0], ffn2_w_down[0], *ln3, tm=1024, tf=256)

    heads = (1, N_HEADS, HEAD_DIM)
    return (yp.reshape(1, t_len, D_MODEL), ys.reshape(n_seq, 1, D_MODEL),
            kp.reshape((1, t_len) + heads), vp.reshape((1, t_len) + heads),
            ks.reshape((n_seq, 1) + heads), vs.reshape((n_seq, 1) + heads),
            van_s.reshape((n_seq, 1) + heads))
```

```python
import functools

import jax
import jax.numpy as jnp
from jax import lax
from jax.experimental import pallas as pl
from jax.experimental.pallas import tpu as pltpu

D_MODEL = 2048
HEAD_DIM = 128
N_HEADS = 8
WIDTH = N_HEADS * HEAD_DIM
CHUNK = 128
MOBA_BLOCK = 256
MOBA_TOPK = 3
PAGE_SIZE = 128
PAGES_PER_BLOCK = MOBA_BLOCK // PAGE_SIZE
D_FF = 5632
LN_EPS = 1e-5
DEPTH = 1
DN_ALPHA = (2 * DEPTH) ** 0.25
ATTN_SCALE = HEAD_DIM ** -0.5
LOG2E = 1.4426950408889634

MASKED = -1e30
MASKED_BF16 = -(2.0 ** 100)

LANES = 128
VMEM_LIMIT = 61 * 1024 * 1024

BF16 = jnp.bfloat16
F32 = jnp.float32


def _layer_norm(x, g, b):
    mu = jnp.mean(x, axis=-1, keepdims=True)
    xc = x - mu
    var = jnp.mean(xc * xc, axis=-1, keepdims=True)
    return xc * lax.rsqrt(var + LN_EPS) * g + b


def _dot(a, b):
    return jnp.dot(a, b, preferred_element_type=F32)


def _dot_nt(a, b):
    return lax.dot_general(a, b, (((1,), (1,)), ((), ())), preferred_element_type=F32)


def _ffn_ln_body(x_ref, wg_ref, wu_ref, wd_ref, g_ref, b_ref, o_ref, xb_ref, side_work=None,
                 f32_weights=False):
    j = pl.program_id(1)
    weight = (lambda ref: ref[...].astype(BF16)) if f32_weights else (lambda ref: ref[...])

    @pl.when(j == 0)
    def _():
        xb_ref[...] = x_ref[...].astype(BF16)
        o_ref[...] = jnp.zeros_like(o_ref)

    side = side_work if side_work is not None else (lambda part: None)
    xb = xb_ref[...]
    side(0)
    gate = _dot(xb, weight(wg_ref))
    side(1)
    up = _dot(xb, weight(wu_ref))
    act = gate * jax.nn.sigmoid(gate) * up
    side(2)
    o_ref[...] += _dot(act.astype(BF16), weight(wd_ref))

    @pl.when(j == pl.num_programs(1) - 1)
    def _():
        y = DN_ALPHA * x_ref[...] + 0.5 * o_ref[...]
        o_ref[...] = _layer_norm(y, g_ref[...], b_ref[...])


def _ffn_ln_pair_body(x_ref, xs_ref, wg_ref, wu_ref, wd_ref, g_ref, b_ref, o_ref, os_ref, xb_ref, xsb_ref):
    last = pl.num_programs(0) - 1
    i = pl.program_id(0)

    @pl.when(i < last)
    def _():
        _ffn_ln_body(x_ref, wg_ref, wu_ref, wd_ref, g_ref, b_ref, o_ref, xb_ref, f32_weights=True)

    @pl.when(i == last)
    def _():
        _ffn_ln_body(xs_ref, wg_ref, wu_ref, wd_ref, g_ref, b_ref, os_ref, xsb_ref, f32_weights=True)


def _ffn_ln_pair(x, xs, wg, wu, wd, g, b, *, tm, tf):
    rows, rows_s = x.shape[0], xs.shape[0]
    n_tiles = rows // tm
    tile = lambda i, j: (jnp.minimum(i, n_tiles - 1), 0)
    const = lambda i, j: (0, 0)
    return pl.pallas_call(
        _ffn_ln_pair_body,
        grid=(n_tiles + 1, D_FF // tf),
        in_specs=[
            pl.BlockSpec((tm, D_MODEL), tile),
            pl.BlockSpec((rows_s, D_MODEL), const),
            pl.BlockSpec((D_MODEL, tf), lambda i, j: (0, j)),
            pl.BlockSpec((D_MODEL, tf), lambda i, j: (0, j)),
            pl.BlockSpec((tf, D_MODEL), lambda i, j: (j, 0)),
            pl.BlockSpec((1, D_MODEL), const),
            pl.BlockSpec((1, D_MODEL), const),
        ],
        out_specs=[pl.BlockSpec((tm, D_MODEL), tile), pl.BlockSpec((rows_s, D_MODEL), const)],
        out_shape=[jax.ShapeDtypeStruct((rows, D_MODEL), F32), jax.ShapeDtypeStruct((rows_s, D_MODEL), F32)],
        scratch_shapes=[pltpu.VMEM((tm, D_MODEL), BF16), pltpu.VMEM((rows_s, D_MODEL), BF16)],
        compiler_params=pltpu.CompilerParams(
            dimension_semantics=("arbitrary", "arbitrary"), vmem_limit_bytes=VMEM_LIMIT),
        name="ffn2",
    )(x, xs, wg, wu, wd, g, b)


def _split(x):
    hi = x.astype(BF16)
    return hi, (x - hi.astype(F32)).astype(BF16)


def _dot_split(x, w):
    x_hi, x_lo = _split(x)
    w_hi, w_lo = _split(w)
    n = x.shape[0]
    both = _dot(jnp.concatenate([x_hi, x_lo], axis=0), w_hi)
    return both[:n] + (both[n:] + _dot(x_hi, w_lo)), w_hi


def _dot_cast(x, w):
    w_hi = w.astype(BF16)
    return _dot(x.astype(BF16), w_hi), w_hi


def _ffn_ln_sample_body(x_ref, wg_ref, wu_ref, wd_ref, g_ref, b_ref,
                        o_ref, wgb_ref, wub_ref, wdb_ref, acc_ref):
    j = pl.program_id(0)
    dot = _dot_split

    @pl.when(j == 0)
    def _():
        acc_ref[...] = jnp.zeros_like(acc_ref)

    x = x_ref[...]
    gate, wgb_ref[...] = dot(x, wg_ref[...])
    up, wub_ref[...] = dot(x, wu_ref[...])
    down, wdb_ref[...] = dot(gate * jax.nn.sigmoid(gate) * up, wd_ref[...])
    acc_ref[...] += down

    @pl.when(j == pl.num_programs(0) - 1)
    def _():
        y = DN_ALPHA * x + 0.5 * acc_ref[...]
        o_ref[...] = _layer_norm(y, g_ref[...], b_ref[...])


def _ffn_ln_sample(x, wg, wu, wd, g, b, *, tf):
    rows = x.shape[0]
    whole = lambda j: (0, 0)
    cols = pl.BlockSpec((D_MODEL, tf), lambda j: (0, j))
    rws = pl.BlockSpec((tf, D_MODEL), lambda j: (j, 0))
    return pl.pallas_call(
        _ffn_ln_sample_body,
        grid=(D_FF // tf,),
        in_specs=[pl.BlockSpec((rows, D_MODEL), whole), cols, cols, rws,
                  pl.BlockSpec((1, D_MODEL), whole), pl.BlockSpec((1, D_MODEL), whole)],
        out_specs=[pl.BlockSpec((rows, D_MODEL), whole), cols, cols, rws],
        out_shape=[jax.ShapeDtypeStruct((rows, D_MODEL), F32),
                   jax.ShapeDtypeStruct(wg.shape, BF16),
                   jax.ShapeDtypeStruct(wu.shape, BF16),
                   jax.ShapeDtypeStruct(wd.shape, BF16)],
        scratch_shapes=[pltpu.VMEM((rows, D_MODEL), F32)],
        compiler_params=pltpu.CompilerParams(
            dimension_semantics=("arbitrary",), vmem_limit_bytes=VMEM_LIMIT),
        name="ffn1_sample",
    )(x, wg, wu, wd, g, b)


def _head_layer_norm(va, g, b):
    parts = []
    for h in range(N_HEADS):
        sl = slice(h * HEAD_DIM, (h + 1) * HEAD_DIM)
        parts.append(_layer_norm(va[:, sl], g[:, sl], b[:, sl]))
    return jnp.concatenate(parts, axis=-1)


def _proj_prompt_body(x_ref, w_ref, ag_ref, ab_ref, ws_ref, bs_ref, q_ref, k_ref, v_ref, gm_ref):
    xb = x_ref[...].astype(BF16)

    def project(group):
        return _dot(xb, w_ref[:, group * WIDTH:(group + 1) * WIDTH])

    u = project(3)
    van = _head_layer_norm(project(4), ag_ref[...], ab_ref[...]).astype(BF16)
    q_ref[...] = (project(0) * (ATTN_SCALE * LOG2E)).astype(BF16)
    k_ref[...] = project(1)
    v_ref[...] = project(2)

    row = lax.broadcasted_iota(jnp.int32, (CHUNK, CHUNK), 0)
    col = lax.broadcasted_iota(jnp.int32, (CHUNK, CHUNK), 1)
    causal = row >= col
    n_chunks = u.shape[0] // CHUNK
    for h in range(N_HEADS):
        sl = slice(h * HEAD_DIM, (h + 1) * HEAD_DIM)
        w_h = jnp.where(causal, ws_ref[h], 0.0).astype(BF16)
        for c in range(n_chunks):
            rs = slice(c * CHUNK, (c + 1) * CHUNK)
            mixed = _dot(w_h, van[rs, sl]) + bs_ref[:, sl]
            gm_ref[rs, sl] = (u[rs, sl] * mixed).astype(BF16)


def _proj_prompt(x, w_in, ag, ab, ws, bs_exp, *, tm):
    rows = x.shape[0]
    row_blk = lambda i: (i, 0)
    const2 = lambda i: (0, 0)
    return pl.pallas_call(
        _proj_prompt_body,
        grid=(rows // tm,),
        in_specs=[
            pl.BlockSpec((tm, D_MODEL), row_blk),
            pl.BlockSpec(w_in.shape, const2, pipeline_mode=pl.Buffered(1)),
            pl.BlockSpec((1, WIDTH), const2),
            pl.BlockSpec((1, WIDTH), const2),
            pl.BlockSpec((N_HEADS, CHUNK, CHUNK), lambda i: (0, 0, 0)),
            pl.BlockSpec((CHUNK, WIDTH), const2),
        ],
        out_specs=[
            pl.BlockSpec((tm, WIDTH), row_blk),
            pl.BlockSpec((tm, WIDTH), row_blk),
            pl.BlockSpec((tm, WIDTH), row_blk),
            pl.BlockSpec((tm, WIDTH), row_blk),
        ],
        out_shape=[
            jax.ShapeDtypeStruct((rows, WIDTH), BF16),
            jax.ShapeDtypeStruct((rows, WIDTH), F32),
            jax.ShapeDtypeStruct((rows, WIDTH), F32),
            jax.ShapeDtypeStruct((rows, WIDTH), BF16),
        ],
        compiler_params=pltpu.CompilerParams(
            dimension_semantics=("parallel",), vmem_limit_bytes=VMEM_LIMIT),
        name="proj_prompt",
    )(x, w_in, ag, ab, ws, bs_exp)


def _proj_sample_body(x_ref, w_ref, ag_ref, ab_ref, w00_ref, b0_ref,
                      q_ref, k_ref, v_ref, gm_ref, van_ref, wb_ref, u_ref):
    j = pl.program_id(0)
    p, w_hi = _dot_split(x_ref[...], w_ref[...])
    wb_ref[...] = w_hi

    @pl.when(j == 0)
    def _():
        q_ref[...] = p

    @pl.when(j == 1)
    def _():
        k_ref[...] = p

    @pl.when(j == 2)
    def _():
        v_ref[...] = p

    @pl.when(j == 3)
    def _():
        u_ref[...] = p

    @pl.when(j == 4)
    def _():
        van = _head_layer_norm(p, ag_ref[...], ab_ref[...])
        van_ref[...] = van
        gm_ref[...] = u_ref[...] * (w00_ref[...] * van + b0_ref[...])


def _proj_sample(x, w_in, ag, ab, w00, b0):
    rows = x.shape[0]
    whole = lambda j: (0, 0)
    w_cols = lambda j: (0, j)
    return pl.pallas_call(
        _proj_sample_body,
        grid=(5,),
        in_specs=[
            pl.BlockSpec((rows, D_MODEL), whole),
            pl.BlockSpec((D_MODEL, WIDTH), w_cols),
            pl.BlockSpec((1, WIDTH), whole),
            pl.BlockSpec((1, WIDTH), whole),
            pl.BlockSpec((1, WIDTH), whole),
            pl.BlockSpec((1, WIDTH), whole),
        ],
        out_specs=[pl.BlockSpec((rows, WIDTH), whole)] * 5 + [pl.BlockSpec((D_MODEL, WIDTH), w_cols)],
        out_shape=[
            jax.ShapeDtypeStruct((rows, WIDTH), F32),
            jax.ShapeDtypeStruct((rows, WIDTH), F32),
            jax.ShapeDtypeStruct((rows, WIDTH), F32),
            jax.ShapeDtypeStruct((rows, WIDTH), F32),
            jax.ShapeDtypeStruct((rows, WIDTH), F32),
            jax.ShapeDtypeStruct(w_in.shape, BF16),
        ],
        scratch_shapes=[pltpu.VMEM((rows, WIDTH), F32)],
        compiler_params=pltpu.CompilerParams(
            dimension_semantics=("arbitrary",), vmem_limit_bytes=VMEM_LIMIT),
        name="proj_sample",
    )(x, w_in, ag, ab, w00, b0)


def _out_ln_body(a_ref, m_ref, h_ref, wa_ref, wm_ref, g_ref, b_ref, o_ref):
    half = a_ref.shape[0] // 2
    mixes = []
    for rows in (slice(0, half), slice(half, 2 * half)):
        mixes.append(_dot(a_ref[rows, :], wa_ref[...]) + _dot(m_ref[rows, :], wm_ref[...]))
    for rows, mix in zip((slice(0, half), slice(half, 2 * half)), mixes):
        o_ref[rows, :] = _layer_norm(DN_ALPHA * h_ref[rows, :] + mix, g_ref[...], b_ref[...])


def _out_ln(attn, gm, h, w_out, g, b, *, tm):
    rows = h.shape[0]
    row_blk = lambda i: (i, 0)
    const = lambda i: (0, 0)
    return pl.pallas_call(
        _out_ln_body,
        grid=(rows // tm,),
        in_specs=[
            pl.BlockSpec((tm, WIDTH), row_blk),
            pl.BlockSpec((tm, WIDTH), row_blk),
            pl.BlockSpec((tm, D_MODEL), row_blk),
            pl.BlockSpec((WIDTH, D_MODEL), const),
            pl.BlockSpec((WIDTH, D_MODEL), lambda i: (1, 0)),
            pl.BlockSpec((1, D_MODEL), const),
            pl.BlockSpec((1, D_MODEL), const),
        ],
        out_specs=pl.BlockSpec((tm, D_MODEL), row_blk),
        out_shape=jax.ShapeDtypeStruct((rows, D_MODEL), F32),
        compiler_params=pltpu.CompilerParams(
            dimension_semantics=("parallel",), vmem_limit_bytes=VMEM_LIMIT),
        name="out_prompt",
    )(attn, gm, h, w_out, w_out, g, b)


def _out_ln_sample_body(x_ref, h_ref, w_ref, g_ref, b_ref, o_ref, wb_ref, acc_ref):
    j = pl.program_id(0)

    @pl.when(j == 0)
    def _():
        acc_ref[...] = jnp.zeros_like(acc_ref)

    part, wb_ref[...] = _dot_cast(x_ref[...], w_ref[...])
    acc_ref[...] += part

    @pl.when(j == pl.num_programs(0) - 1)
    def _():
        o_ref[...] = _layer_norm(DN_ALPHA * h_ref[...] + acc_ref[...], g_ref[...], b_ref[...])


def _out_ln_sample(mix, h, w_out, g, b, *, tk):
    rows = h.shape[0]
    whole = lambda j: (0, 0)
    slab = pl.BlockSpec((tk, D_MODEL), lambda j: (j, 0))
    return pl.pallas_call(
        _out_ln_sample_body,
        grid=(w_out.shape[0] // tk,),
        in_specs=[pl.BlockSpec((rows, tk), lambda j: (0, j)), pl.BlockSpec((rows, D_MODEL), whole), slab,
                  pl.BlockSpec((1, D_MODEL), whole), pl.BlockSpec((1, D_MODEL), whole)],
        out_specs=[pl.BlockSpec((rows, D_MODEL), whole), slab],
        out_shape=[jax.ShapeDtypeStruct((rows, D_MODEL), F32), jax.ShapeDtypeStruct(w_out.shape, BF16)],
        scratch_shapes=[pltpu.VMEM((rows, D_MODEL), F32)],
        compiler_params=pltpu.CompilerParams(
            dimension_semantics=("arbitrary",), vmem_limit_bytes=VMEM_LIMIT),
        name="out_sample",
    )(mix, h, w_out, g, b)


CHUNK_BLOCKS = 4
KEY_CHUNK = CHUNK_BLOCKS * MOBA_BLOCK
N_SEL_ROWS = 32
N_POS_ROWS = 16
SPLIT = 3
V_ROWS = HEAD_DIM + 16
STAGE_LOOKAHEAD = 3
N_STAGE_BUFS = 4


def _top_blocks_t(gate, n_valid):
    blk = lax.broadcasted_iota(jnp.int32, gate.shape, 0).astype(F32)
    valid = blk < n_valid
    g = jnp.where(valid, gate, -jnp.inf)
    sel = jnp.zeros(gate.shape, dtype=jnp.bool_)
    for _ in range(MOBA_TOPK):
        mx = jnp.max(g, axis=0, keepdims=True)
        first = jnp.min(jnp.where(g == mx, blk, float(gate.shape[0])), axis=0, keepdims=True)
        hit = blk == first
        sel = sel | (hit & valid)
        g = jnp.where(hit, -jnp.inf, g)
    return sel


def _attn_prompt_body(slope_ref, q_ref, k_ref, v_ref, o_ref,
                      ka_ref, vt_ref, km_ref, pos_ref, qa_ref, m_ref, acc_ref,
                      u_ref, p_ref, mx_ref, alpha_ref):
    h = pl.program_id(0)
    i = pl.program_id(1)
    n_blocks = k_ref.shape[0] // MOBA_BLOCK
    slope2 = slope_ref[h] * LOG2E

    @pl.when(i == 0)
    def _():
        km_ref[...] = jnp.zeros_like(km_ref)
        lane = lax.broadcasted_iota(jnp.int32, (MOBA_BLOCK, LANES), 1)
        offs = lax.broadcasted_iota(jnp.int32, (MOBA_BLOCK, LANES), 0).astype(F32)
        tail = lax.broadcasted_iota(jnp.int32, (V_ROWS - HEAD_DIM, MOBA_BLOCK), 0)
        ones_row = jnp.where(tail == 0, 1.0, 0.0).astype(BF16)

        def prep(n, carry):
            rows = pl.ds(pl.multiple_of(n * MOBA_BLOCK, MOBA_BLOCK), MOBA_BLOCK)
            kblk = k_ref[rows, :]
            in_chunk = lax.convert_element_type(n % CHUNK_BLOCKS, F32)
            aug = jnp.where(lane == n, 1.0, 0.0)
            aug = jnp.where((lane >= N_SEL_ROWS) & (lane < N_SEL_ROWS + SPLIT), in_chunk, aug)
            aug = jnp.where((lane >= N_SEL_ROWS + SPLIT) & (lane < N_SEL_ROWS + 2 * SPLIT), offs, aug)
            ka_ref[rows, :HEAD_DIM] = kblk.astype(BF16)
            ka_ref[rows, HEAD_DIM:] = aug.astype(BF16)
            vt_ref[n, :HEAD_DIM] = v_ref[rows, :].T.astype(BF16)
            vt_ref[n, HEAD_DIM:] = ones_row
            km_ref[pl.ds(n, 1), :] = jnp.mean(kblk, axis=0, keepdims=True)
            return carry

        lax.fori_loop(0, n_blocks, prep, 0)
        r = lax.broadcasted_iota(jnp.int32, pos_ref.shape, 0)
        rest = jnp.full(pos_ref.shape, slope2, F32)
        rows = jnp.zeros(pos_ref.shape, F32)
        for t in range(SPLIT):
            term = rest.astype(BF16).astype(F32)
            rest = rest - term
            rows = jnp.where(r == t, term * MOBA_BLOCK, jnp.where(r == t + SPLIT, term, rows))
        pos_ref[...] = rows.astype(BF16)

    qt = q_ref[...].astype(F32).T.astype(BF16)
    gate = _dot(km_ref[...].astype(BF16), qt)
    own = i * CHUNK_BLOCKS + lax.broadcasted_iota(jnp.int32, (1, KEY_CHUNK), 1) // MOBA_BLOCK
    sel = _top_blocks_t(gate, own.astype(F32))
    blk = lax.broadcasted_iota(jnp.int32, gate.shape, 0)
    qa_ref[:HEAD_DIM, :] = qt
    qa_ref[HEAD_DIM:HEAD_DIM + N_SEL_ROWS, :] = jnp.where(sel | (blk == own), 0.0, MASKED_BF16).astype(BF16)
    qa_ref[HEAD_DIM + N_SEL_ROWS:HEAD_DIM + N_SEL_ROWS + N_POS_ROWS, :] = jnp.concatenate(
        [pos_ref[...]] * CHUNK_BLOCKS, axis=1)
    qa_ref[HEAD_DIM + N_SEL_ROWS + N_POS_ROWS:, :] = jnp.zeros(
        (HEAD_DIM - N_SEL_ROWS - N_POS_ROWS, KEY_CHUNK), BF16)

    half = KEY_CHUNK // 2
    key = lax.broadcasted_iota(jnp.int32, (half, MOBA_BLOCK), 0)
    qry = lax.broadcasted_iota(jnp.int32, (half, MOBA_BLOCK), 1)

    def stage_logits(c, j, buf, diagonal):
        cols = slice(j * MOBA_BLOCK, (j + 1) * MOBA_BLOCK)
        mx = None
        for s in range(2):
            rows = pl.ds(pl.multiple_of(c * KEY_CHUNK + s * half, half), half)
            u = _dot(ka_ref[rows, :], qa_ref[:, cols])
            if diagonal:
                u = jnp.where(key + s * half <= qry + j * MOBA_BLOCK, u, MASKED)
            u_ref[buf, s * half:(s + 1) * half, :] = u
            hmax = jnp.max(u, axis=0, keepdims=True)
            mx = hmax if mx is None else jnp.maximum(mx, hmax)
        mx_ref[buf] = mx

    def stage_softmax(c, j, buf, diagonal):
        cols = slice(j * MOBA_BLOCK, (j + 1) * MOBA_BLOCK)
        blocks_ahead = (c - i) * CHUNK_BLOCKS - j
        off = slope2 * lax.convert_element_type(blocks_ahead * MOBA_BLOCK, F32)
        m_new = mx_ref[buf] + off
        if not diagonal:
            m_old = m_ref[:, cols]
            m_new = jnp.maximum(m_old, m_new)
            alpha_ref[buf] = jnp.exp2(m_old - m_new)
        p_ref[buf] = jnp.exp2((u_ref[buf] - (m_new - off)).astype(BF16))
        m_ref[:, cols] = m_new

    def stage_values(c, j, buf, diagonal):
        cols = slice(j * MOBA_BLOCK, (j + 1) * MOBA_BLOCK)
        parts = []
        for s in range(2):
            pv = None
            for n in range(s * CHUNK_BLOCKS // 2, (s + 1) * CHUNK_BLOCKS // 2):
                d = _dot(vt_ref[c * CHUNK_BLOCKS + n], p_ref[buf, n * MOBA_BLOCK:(n + 1) * MOBA_BLOCK, :])
                pv = d if pv is None else pv + d
            parts.append(pv)
        pv = parts[0] + parts[1]
        acc_ref[:, cols] = pv if diagonal else alpha_ref[buf] * acc_ref[:, cols] + pv

    def sweep(chunks, diagonal):
        stages = [(c, j) for c in chunks for j in range(CHUNK_BLOCKS)]
        for s, (c, j) in enumerate(stages[:STAGE_LOOKAHEAD]):
            stage_logits(c, j, s % N_STAGE_BUFS, diagonal)
        for s, (c, j) in enumerate(stages):
            if s + STAGE_LOOKAHEAD < len(stages):
                nc, nj = stages[s + STAGE_LOOKAHEAD]
                stage_logits(nc, nj, (s + STAGE_LOOKAHEAD) % N_STAGE_BUFS, diagonal)
            stage_softmax(c, j, s % N_STAGE_BUFS, diagonal)
            stage_values(c, j, s % N_STAGE_BUFS, diagonal)

    sweep([i], True)

    def past_pair(cc, carry):
        sweep([2 * cc, 2 * cc + 1], False)
        return carry

    lax.fori_loop(0, i // 2, past_pair, 0)

    @pl.when(i % 2 == 1)
    def _():
        sweep([i - 1], False)

    o_ref[...] = (acc_ref[:HEAD_DIM] / acc_ref[HEAD_DIM:HEAD_DIM + 1]).T.astype(o_ref.dtype)


def _attn_prompt(slopes, q, k, v):
    t_len = q.shape[0]
    n_blocks = t_len // MOBA_BLOCK
    assert n_blocks <= N_SEL_ROWS and t_len % KEY_CHUNK == 0
    grid_spec = pltpu.PrefetchScalarGridSpec(
        num_scalar_prefetch=1,
        grid=(N_HEADS, t_len // KEY_CHUNK),
        in_specs=[
            pl.BlockSpec((KEY_CHUNK, HEAD_DIM), lambda h, i, s: (i, h)),
            pl.BlockSpec((t_len, HEAD_DIM), lambda h, i, s: (0, h)),
            pl.BlockSpec((t_len, HEAD_DIM), lambda h, i, s: (0, h)),
        ],
        out_specs=pl.BlockSpec((KEY_CHUNK, HEAD_DIM), lambda h, i, s: (i, h)),
        scratch_shapes=[
            pltpu.VMEM((t_len, 2 * HEAD_DIM), BF16),
            pltpu.VMEM((n_blocks, V_ROWS, MOBA_BLOCK), BF16),
            pltpu.VMEM((N_SEL_ROWS, HEAD_DIM), F32),
            pltpu.VMEM((N_POS_ROWS, MOBA_BLOCK), BF16),
            pltpu.VMEM((2 * HEAD_DIM, KEY_CHUNK), BF16),
            pltpu.VMEM((1, KEY_CHUNK), F32),
            pltpu.VMEM((V_ROWS, KEY_CHUNK), F32),
            pltpu.VMEM((N_STAGE_BUFS, KEY_CHUNK, MOBA_BLOCK), F32),
            pltpu.VMEM((N_STAGE_BUFS, KEY_CHUNK, MOBA_BLOCK), BF16),
            pltpu.VMEM((N_STAGE_BUFS, 1, MOBA_BLOCK), F32),
            pltpu.VMEM((N_STAGE_BUFS, 1, MOBA_BLOCK), F32),
        ],
    )
    return pl.pallas_call(
        _attn_prompt_body,
        grid_spec=grid_spec,
        out_shape=jax.ShapeDtypeStruct((t_len, WIDTH), BF16),
        compiler_params=pltpu.CompilerParams(
            dimension_semantics=("parallel", "arbitrary"), vmem_limit_bytes=VMEM_LIMIT),
        name="attn_prompt",
    )(slopes, q, k, v)


GROUP_PAGES = 12
GROUP_BLOCKS = GROUP_PAGES // PAGES_PER_BLOCK


class _SelectStep:
    def __init__(self, t, pt_ref, q_ref, ck_ref, sel_ref, buf_ref, sem_ref, stage_ref, ksum_ref):
        self.t, self.pt_ref, self.q_ref, self.ck_ref = t, pt_ref, q_ref, ck_ref
        self.sel_ref, self.buf_ref, self.sem_ref = sel_ref, buf_ref, sem_ref
        self.stage_ref, self.ksum_ref = stage_ref, ksum_ref
        self.n_seq, self.n_pages = pt_ref.shape
        self.total_pages = self.n_seq * self.n_pages
        self.total_blocks = self.total_pages // PAGES_PER_BLOCK
        self.n_steps = -(-self.total_pages // GROUP_PAGES)
        self.active = t < self.n_steps
        self.slot = t % 2

    def _page_copy(self, flat, slot, p):
        flat = jnp.minimum(flat, self.total_pages - 1)
        page = self.pt_ref[flat // self.n_pages, flat % self.n_pages]
        return pltpu.make_async_copy(
            self.ck_ref.at[page, :, 0], self.buf_ref.at[slot, pl.ds(p * PAGE_SIZE, PAGE_SIZE)],
            self.sem_ref.at[slot])

    def _step_copies(self, t, slot):
        return [self._page_copy(t * GROUP_PAGES + p, slot, p) for p in range(GROUP_PAGES)]

    def copies(self):
        @pl.when(self.t == 0)
        def _():
            for c in self._step_copies(0, 0):
                c.start()

        @pl.when(self.t + 1 < self.n_steps)
        def _():
            for c in self._step_copies(self.t + 1, 1 - self.slot):
                c.start()

        @pl.when(self.active)
        def _():
            for c in self._step_copies(self.t, self.slot):
                c.wait()

    def block_sums(self, part):
        for n in range(part * GROUP_BLOCKS // 3, (part + 1) * GROUP_BLOCKS // 3):
            blk = self.buf_ref[self.slot, pl.ds(n * MOBA_BLOCK, MOBA_BLOCK)]
            self.stage_ref[n] = jnp.sum(blk, axis=0)

    def pick(self):
        n_full = self.n_pages // PAGES_PER_BLOCK
        for n in range(GROUP_BLOCKS):
            flat = self.t * GROUP_BLOCKS + n
            self.ksum_ref[jnp.where(flat < self.total_blocks, flat, self.total_blocks + n)] = self.stage_ref[n]

        def pick_seq(b, carry):
            ksum = self.ksum_ref[pl.ds(pl.multiple_of(b * n_full, n_full), n_full)]
            kmean = ksum / MOBA_BLOCK
            gate = jnp.sum(kmean * self.q_ref[b], axis=-1, keepdims=True)
            blk_id = lax.broadcasted_iota(jnp.int32, gate.shape, 0).astype(F32)
            lane = lax.broadcasted_iota(jnp.int32, (N_HEADS, LANES), 1)
            picks = jnp.zeros((N_HEADS, LANES), F32)
            for j in range(MOBA_TOPK):
                mx = jnp.max(gate, axis=0, keepdims=True)
                first = jnp.min(jnp.where(gate == mx, blk_id, float(n_full)), axis=0, keepdims=True)
                picks = jnp.where(lane == j, first[0], picks)
                gate = jnp.where(blk_id == first, -jnp.inf, gate)
            self.sel_ref[b] = picks.astype(jnp.int32)
            return carry

        @pl.when(self.t == self.n_steps - 1)
        def _():
            lax.fori_loop(0, self.n_seq, pick_seq, 0)


def _ffn_ln_select_body(pt_ref, x_ref, wg_ref, wu_ref, wd_ref, g_ref, b_ref, q_ref, ck_ref,
                        o_ref, sel_ref, xb_ref, buf_ref, sem_ref, stage_ref, ksum_ref):
    t = pl.program_id(0) * pl.num_programs(1) + pl.program_id(1)
    step = _SelectStep(t, pt_ref, q_ref, ck_ref, sel_ref, buf_ref, sem_ref, stage_ref, ksum_ref)
    step.copies()
    _ffn_ln_body(x_ref, wg_ref, wu_ref, wd_ref, g_ref, b_ref, o_ref, xb_ref, side_work=step.block_sums)
    step.pick()


def _ffn_ln_select(x, wg, wu, wd, g, b, page_table, q_sample, cache_k, *, tm, tf):
    rows = x.shape[0]
    n_seq, n_pages = page_table.shape
    total_blocks = n_seq * n_pages // PAGES_PER_BLOCK
    grid = (rows // tm, D_FF // tf)
    assert n_pages % PAGES_PER_BLOCK == 0 and grid[0] * grid[1] * GROUP_PAGES >= n_seq * n_pages
    once = pl.Buffered(1)
    grid_spec = pltpu.PrefetchScalarGridSpec(
        num_scalar_prefetch=1,
        grid=grid,
        in_specs=[
            pl.BlockSpec((tm, D_MODEL), lambda i, j, pt: (i, 0)),
            pl.BlockSpec((D_MODEL, tf), lambda i, j, pt: (0, j)),
            pl.BlockSpec((D_MODEL, tf), lambda i, j, pt: (0, j)),
            pl.BlockSpec((tf, D_MODEL), lambda i, j, pt: (j, 0)),
            pl.BlockSpec((1, D_MODEL), lambda i, j, pt: (0, 0)),
            pl.BlockSpec((1, D_MODEL), lambda i, j, pt: (0, 0)),
            pl.BlockSpec((n_seq, N_HEADS, HEAD_DIM), lambda i, j, pt: (0, 0, 0)),
            pl.BlockSpec(memory_space=pl.ANY),
        ],
        out_specs=[
            pl.BlockSpec((tm, D_MODEL), lambda i, j, pt: (i, 0), pipeline_mode=once),
            pl.BlockSpec((n_seq, N_HEADS, LANES), lambda i, j, pt: (0, 0, 0)),
        ],
        scratch_shapes=[
            pltpu.VMEM((tm, D_MODEL), BF16),
            pltpu.VMEM((2, GROUP_PAGES * PAGE_SIZE, N_HEADS, HEAD_DIM), F32),
            pltpu.SemaphoreType.DMA((2,)),
            pltpu.VMEM((GROUP_BLOCKS, N_HEADS, HEAD_DIM), F32),
            pltpu.VMEM((total_blocks + GROUP_BLOCKS, N_HEADS, HEAD_DIM), F32),
        ],
    )
    return pl.pallas_call(
        _ffn_ln_select_body,
        grid_spec=grid_spec,
        out_shape=[jax.ShapeDtypeStruct((rows, D_MODEL), F32),
                   jax.ShapeDtypeStruct((n_seq, N_HEADS, LANES), jnp.int32)],
        compiler_params=pltpu.CompilerParams(
            dimension_semantics=("arbitrary", "arbitrary"), vmem_limit_bytes=VMEM_LIMIT),
        name="ffn1_prompt_select",
    )(page_table, x, wg, wu, wd, g, b, q_sample.reshape(n_seq, N_HEADS, HEAD_DIM), cache_k)


def _attn_sample_body(pt_ref, sel_ref, slope_ref, q_ref, kn_ref, vn_ref, ck_ref, cv_ref, o_ref,
                      kbuf_ref, vbuf_ref, sem_ref, *, past_len):
    b = pl.program_id(0)
    n_sel = MOBA_TOPK
    slot = b % 2

    def gathers(b, slot):
        out = []
        for h in range(N_HEADS):
            for j in range(n_sel):
                for pp in range(PAGES_PER_BLOCK):
                    page = pt_ref[b, sel_ref[b, h * MOBA_TOPK + j] * PAGES_PER_BLOCK + pp]
                    rows = pl.ds((j * PAGES_PER_BLOCK + pp) * PAGE_SIZE, PAGE_SIZE)
                    out.append(pltpu.make_async_copy(
                        ck_ref.at[page, :, 0, h, :], kbuf_ref.at[slot, h, rows, :], sem_ref.at[slot, 0]))
                    out.append(pltpu.make_async_copy(
                        cv_ref.at[page, :, 0, h, :], vbuf_ref.at[slot, h, rows, :], sem_ref.at[slot, 1]))
        return out

    @pl.when(b == 0)
    def _():
        for c in gathers(0, 0):
            c.start()

    @pl.when(b + 1 < pl.num_programs(0))
    def _():
        for c in gathers(b + 1, 1 - slot):
            c.start()

    for c in gathers(b, slot):
        c.wait()

    n_keys = n_sel * MOBA_BLOCK
    lane = lax.broadcasted_iota(jnp.int32, (1, n_keys), 1)
    which = lane // MOBA_BLOCK
    offs = lane % MOBA_BLOCK
    outs = []
    for h in range(N_HEADS):
        sl = slice(h * HEAD_DIM, (h + 1) * HEAD_DIM)
        slope = slope_ref[h]
        qh = q_ref[:, sl]
        kn = kn_ref[:, sl]
        vn = vn_ref[:, sl]
        q8 = jnp.broadcast_to(qh, (8, HEAD_DIM)).astype(BF16)
        s = _dot_nt(q8, kbuf_ref[slot, h].astype(BF16))[:1] * ATTN_SCALE
        blk = jnp.zeros((1, n_keys), jnp.int32)
        for j in range(n_sel):
            blk = jnp.where(which == j, sel_ref[b, h * MOBA_TOPK + j], blk)
        dist = (past_len - (blk * MOBA_BLOCK + offs)).astype(F32)
        s = s - slope * dist
        s_own = jnp.sum(qh * kn, axis=-1, keepdims=True) * ATTN_SCALE
        mx = jnp.maximum(jnp.max(s, axis=-1, keepdims=True), s_own)
        p = jnp.exp(s - mx)
        p_own = jnp.exp(s_own - mx)
        denom = jnp.sum(p, axis=-1, keepdims=True) + p_own
        p8 = jnp.broadcast_to(p, (8, n_keys)).astype(BF16)
        pv = _dot(p8, vbuf_ref[slot, h].astype(BF16))[:1]
        outs.append((pv + p_own * vn) / denom)
    o_ref[...] = jnp.concatenate(outs, axis=-1)


def _attn_sample(page_table, sel, slopes, q, k_new, v_new, cache_k, cache_v):
    n_seq, n_pages = page_table.shape
    per_seq = pl.BlockSpec((None, 1, WIDTH), lambda b, *_: (b, 0, 0))
    grid_spec = pltpu.PrefetchScalarGridSpec(
        num_scalar_prefetch=3,
        grid=(n_seq,),
        in_specs=[
            per_seq, per_seq, per_seq,
            pl.BlockSpec(memory_space=pl.ANY),
            pl.BlockSpec(memory_space=pl.ANY),
        ],
        out_specs=per_seq,
        scratch_shapes=[
            pltpu.VMEM((2, N_HEADS, MOBA_TOPK * MOBA_BLOCK, HEAD_DIM), F32),
            pltpu.VMEM((2, N_HEADS, MOBA_TOPK * MOBA_BLOCK, HEAD_DIM), F32),
            pltpu.SemaphoreType.DMA((2, 2)),
        ],
    )
    return pl.pallas_call(
        functools.partial(_attn_sample_body, past_len=n_pages * PAGE_SIZE),
        grid_spec=grid_spec,
        out_shape=jax.ShapeDtypeStruct((n_seq, 1, WIDTH), F32),
        compiler_params=pltpu.CompilerParams(
            dimension_semantics=("arbitrary",), vmem_limit_bytes=VMEM_LIMIT),
        name="attn_sample",
    )(page_table, sel, slopes, q.reshape(n_seq, 1, WIDTH), k_new.reshape(n_seq, 1, WIDTH),
      v_new.reshape(n_seq, 1, WIDTH), cache_k, cache_v).reshape(n_seq, WIDTH)


def kernel(x_prompt, x_sample, cache_k, cache_v, page_table, ffn1_w_gate, ffn1_w_up, ffn1_w_down, ln1_g, ln1_b, w_in, a_norm_g, a_norm_b, a_spatial_w, a_spatial_b, w_out, ln2_g, ln2_b, ffn2_w_gate, ffn2_w_up, ffn2_w_down, ln3_g, ln3_b):
    assert x_prompt.shape[0] == 1 and x_sample.shape[1] == 1 and w_in.shape[0] == DEPTH == 1
    t_len = x_prompt.shape[1]
    n_seq = x_sample.shape[0]
    xp = x_prompt.reshape(t_len, D_MODEL)
    xs = x_sample.reshape(n_seq, D_MODEL)
    slopes = jnp.exp2(-8.0 * jnp.arange(1, N_HEADS + 1, dtype=F32) / N_HEADS)

    row = lambda v: v.reshape(1, -1)
    ln1, ln2, ln3 = ((row(g[0]), row(b[0])) for g, b in ((ln1_g, ln1_b), (ln2_g, ln2_b), (ln3_g, ln3_b)))
    ag, ab = row(a_norm_g[0]), row(a_norm_b[0])
    ws, bs = a_spatial_w[0], a_spatial_b[0]
    bs_exp = jnp.repeat(bs.T, HEAD_DIM, axis=1)
    w00 = row(jnp.repeat(ws[:, 0, 0], HEAD_DIM))
    b0 = row(jnp.repeat(bs[:, 0], HEAD_DIM))

    hs, w1g, w1u, w1d = _ffn_ln_sample(xs, ffn1_w_gate[0], ffn1_w_up[0], ffn1_w_down[0], *ln1, tf=256)
    qs, ks, vs, gm_s, van_s, w_in_b = _proj_sample(hs, w_in[0], ag, ab, w00, b0)
    hp, sel = _ffn_ln_select(xp, w1g, w1u, w1d, *ln1, page_table, qs, cache_k, tm=1024, tf=256)
    qp, kp, vp, gm_p = _proj_prompt(hp, w_in_b, ag, ab, ws, bs_exp, tm=512)

    sel = sel[:, :, :MOBA_TOPK].reshape(n_seq, N_HEADS * MOBA_TOPK)
    attn_s = _attn_sample(page_table, sel, slopes, qs, ks, vs, cache_k, cache_v)
    attn_p = _attn_prompt(slopes, qp, kp, vp)

    hs, w_out_b = _out_ln_sample(jnp.concatenate([attn_s, gm_s], axis=-1), hs, w_out[0], *ln2, tk=512)
    hp = _out_ln(attn_p, gm_p, hp, w_out_b, *ln2, tm=512)

    yp, ys = _ffn_ln_pair(hp, hs, ffn2_w_gate[0], ffn2_w_up[0], ffn2_w_down[0], *ln3, tm=1024, tf=256)

    heads = (1, N_HEADS, HEAD_DIM)
    return (yp.reshape(1, t_len, D_MODEL), ys.reshape(n_seq, 1, D_MODEL),
            kp.reshape((1, t_len) + heads), vp.reshape((1, t_len) + heads),
            ks.reshape((n_seq, 1) + heads), vs.reshape((n_seq, 1) + heads),
            van_s.reshape((n_seq, 1) + heads))
```

```python
import functools

import jax
import jax.numpy as jnp
from jax import lax
from jax.experimental import pallas as pl
from jax.experimental.pallas import tpu as pltpu

D_MODEL = 2048
HEAD_DIM = 128
N_HEADS = 8
WIDTH = N_HEADS * HEAD_DIM
CHUNK = 128
MOBA_BLOCK = 256
MOBA_TOPK = 3
PAGE_SIZE = 128
PAGES_PER_BLOCK = MOBA_BLOCK // PAGE_SIZE
D_FF = 5632
LN_EPS = 1e-5
DEPTH = 1
DN_ALPHA = (2 * DEPTH) ** 0.25
ATTN_SCALE = HEAD_DIM ** -0.5
LOG2E = 1.4426950408889634

MASKED = -1e30
MASKED_BF16 = -(2.0 ** 100)

LANES = 128
VMEM_LIMIT = 61 * 1024 * 1024

BF16 = jnp.bfloat16
F32 = jnp.float32


def _layer_norm(x, g, b):
    mu = jnp.mean(x, axis=-1, keepdims=True)
    xc = x - mu
    var = jnp.mean(xc * xc, axis=-1, keepdims=True)
    return xc * lax.rsqrt(var + LN_EPS) * g + b


def _dot(a, b):
    return jnp.dot(a, b, preferred_element_type=F32)


def _dot_nt(a, b):
    return lax.dot_general(a, b, (((1,), (1,)), ((), ())), preferred_element_type=F32)


def _ffn_ln_body(x_ref, wg_ref, wu_ref, wd_ref, g_ref, b_ref, o_ref, xb_ref, side_work=None,
                 f32_weights=False):
    j = pl.program_id(1)
    weight = (lambda ref: ref[...].astype(BF16)) if f32_weights else (lambda ref: ref[...])

    @pl.when(j == 0)
    def _():
        xb_ref[...] = x_ref[...].astype(BF16)
        o_ref[...] = jnp.zeros_like(o_ref)

    side = side_work if side_work is not None else (lambda part: None)
    xb = xb_ref[...]
    side(0)
    gate = _dot(xb, weight(wg_ref))
    side(1)
    up = _dot(xb, weight(wu_ref))
    act = gate * jax.nn.sigmoid(gate) * up
    side(2)
    o_ref[...] += _dot(act.astype(BF16), weight(wd_ref))

    @pl.when(j == pl.num_programs(1) - 1)
    def _():
        y = DN_ALPHA * x_ref[...] + 0.5 * o_ref[...]
        o_ref[...] = _layer_norm(y, g_ref[...], b_ref[...])


def _ffn_ln_pair_body(x_ref, xs_ref, wg_ref, wu_ref, wd_ref, g_ref, b_ref, o_ref, os_ref, xb_ref, xsb_ref):
    last = pl.num_programs(0) - 1
    i = pl.program_id(0)

    @pl.when(i < last)
    def _():
        _ffn_ln_body(x_ref, wg_ref, wu_ref, wd_ref, g_ref, b_ref, o_ref, xb_ref, f32_weights=True)

    @pl.when(i == last)
    def _():
        _ffn_ln_body(xs_ref, wg_ref, wu_ref, wd_ref, g_ref, b_ref, os_ref, xsb_ref, f32_weights=True)


def _ffn_ln_pair(x, xs, wg, wu, wd, g, b, *, tm, tf):
    rows, rows_s = x.shape[0], xs.shape[0]
    n_tiles = rows // tm
    tile = lambda i, j: (jnp.minimum(i, n_tiles - 1), 0)
    const = lambda i, j: (0, 0)
    return pl.pallas_call(
        _ffn_ln_pair_body,
        grid=(n_tiles + 1, D_FF // tf),
        in_specs=[
            pl.BlockSpec((tm, D_MODEL), tile),
            pl.BlockSpec((rows_s, D_MODEL), const),
            pl.BlockSpec((D_MODEL, tf), lambda i, j: (0, j)),
            pl.BlockSpec((D_MODEL, tf), lambda i, j: (0, j)),
            pl.BlockSpec((tf, D_MODEL), lambda i, j: (j, 0)),
            pl.BlockSpec((1, D_MODEL), const),
            pl.BlockSpec((1, D_MODEL), const),
        ],
        out_specs=[pl.BlockSpec((tm, D_MODEL), tile), pl.BlockSpec((rows_s, D_MODEL), const)],
        out_shape=[jax.ShapeDtypeStruct((rows, D_MODEL), F32), jax.ShapeDtypeStruct((rows_s, D_MODEL), F32)],
        scratch_shapes=[pltpu.VMEM((tm, D_MODEL), BF16), pltpu.VMEM((rows_s, D_MODEL), BF16)],
        compiler_params=pltpu.CompilerParams(
            dimension_semantics=("arbitrary", "arbitrary"), vmem_limit_bytes=VMEM_LIMIT),
        name="ffn2",
    )(x, xs, wg, wu, wd, g, b)


def _split(x):
    hi = x.astype(BF16)
    return hi, (x - hi.astype(F32)).astype(BF16)


def _dot_split(x, w):
    x_hi, x_lo = _split(x)
    w_hi, w_lo = _split(w)
    n = x.shape[0]
    both = _dot(jnp.concatenate([x_hi, x_lo], axis=0), w_hi)
    return both[:n] + (both[n:] + _dot(x_hi, w_lo)), w_hi


def _dot_cast(x, w):
    w_hi = w.astype(BF16)
    return _dot(x.astype(BF16), w_hi), w_hi


def _ffn_ln_sample_body(x_ref, wg_ref, wu_ref, wd_ref, g_ref, b_ref,
                        o_ref, wgb_ref, wub_ref, wdb_ref, acc_ref):
    j = pl.program_id(0)
    dot = _dot_split

    @pl.when(j == 0)
    def _():
        acc_ref[...] = jnp.zeros_like(acc_ref)

    x = x_ref[...]
    gate, wgb_ref[...] = dot(x, wg_ref[...])
    up, wub_ref[...] = dot(x, wu_ref[...])
    down, wdb_ref[...] = dot(gate * jax.nn.sigmoid(gate) * up, wd_ref[...])
    acc_ref[...] += down

    @pl.when(j == pl.num_programs(0) - 1)
    def _():
        y = DN_ALPHA * x + 0.5 * acc_ref[...]
        o_ref[...] = _layer_norm(y, g_ref[...], b_ref[...])


def _ffn_ln_sample(x, wg, wu, wd, g, b, *, tf):
    rows = x.shape[0]
    whole = lambda j: (0, 0)
    cols = pl.BlockSpec((D_MODEL, tf), lambda j: (0, j))
    rws = pl.BlockSpec((tf, D_MODEL), lambda j: (j, 0))
    return pl.pallas_call(
        _ffn_ln_sample_body,
        grid=(D_FF // tf,),
        in_specs=[pl.BlockSpec((rows, D_MODEL), whole), cols, cols, rws,
                  pl.BlockSpec((1, D_MODEL), whole), pl.BlockSpec((1, D_MODEL), whole)],
        out_specs=[pl.BlockSpec((rows, D_MODEL), whole), cols, cols, rws],
        out_shape=[jax.ShapeDtypeStruct((rows, D_MODEL), F32),
                   jax.ShapeDtypeStruct(wg.shape, BF16),
                   jax.ShapeDtypeStruct(wu.shape, BF16),
                   jax.ShapeDtypeStruct(wd.shape, BF16)],
        scratch_shapes=[pltpu.VMEM((rows, D_MODEL), F32)],
        compiler_params=pltpu.CompilerParams(
            dimension_semantics=("arbitrary",), vmem_limit_bytes=VMEM_LIMIT),
        name="ffn1_sample",
    )(x, wg, wu, wd, g, b)


def _head_layer_norm(va, g, b):
    parts = []
    for h in range(N_HEADS):
        sl = slice(h * HEAD_DIM, (h + 1) * HEAD_DIM)
        parts.append(_layer_norm(va[:, sl], g[:, sl], b[:, sl]))
    return jnp.concatenate(parts, axis=-1)


def _proj_prompt_body(x_ref, w_ref, ag_ref, ab_ref, ws_ref, bs_ref, q_ref, k_ref, v_ref, gm_ref):
    xb = x_ref[...].astype(BF16)

    def project(group):
        return _dot(xb, w_ref[:, group * WIDTH:(group + 1) * WIDTH])

    u = project(3)
    van = _head_layer_norm(project(4), ag_ref[...], ab_ref[...]).astype(BF16)
    q_ref[...] = (project(0) * (ATTN_SCALE * LOG2E)).astype(BF16)
    k_ref[...] = project(1)
    v_ref[...] = project(2)

    row = lax.broadcasted_iota(jnp.int32, (CHUNK, CHUNK), 0)
    col = lax.broadcasted_iota(jnp.int32, (CHUNK, CHUNK), 1)
    causal = row >= col
    n_chunks = u.shape[0] // CHUNK
    for h in range(N_HEADS):
        sl = slice(h * HEAD_DIM, (h + 1) * HEAD_DIM)
        w_h = jnp.where(causal, ws_ref[h], 0.0).astype(BF16)
        for c in range(n_chunks):
            rs = slice(c * CHUNK, (c + 1) * CHUNK)
            mixed = _dot(w_h, van[rs, sl]) + bs_ref[:, sl]
            gm_ref[rs, sl] = (u[rs, sl] * mixed).astype(BF16)


def _proj_prompt(x, w_in, ag, ab, ws, bs_exp, *, tm):
    rows = x.shape[0]
    row_blk = lambda i: (i, 0)
    const2 = lambda i: (0, 0)
    return pl.pallas_call(
        _proj_prompt_body,
        grid=(rows // tm,),
        in_specs=[
            pl.BlockSpec((tm, D_MODEL), row_blk),
            pl.BlockSpec(w_in.shape, const2, pipeline_mode=pl.Buffered(1)),
            pl.BlockSpec((1, WIDTH), const2),
            pl.BlockSpec((1, WIDTH), const2),
            pl.BlockSpec((N_HEADS, CHUNK, CHUNK), lambda i: (0, 0, 0)),
            pl.BlockSpec((CHUNK, WIDTH), const2),
        ],
        out_specs=[
            pl.BlockSpec((tm, WIDTH), row_blk),
            pl.BlockSpec((tm, WIDTH), row_blk),
            pl.BlockSpec((tm, WIDTH), row_blk),
            pl.BlockSpec((tm, WIDTH), row_blk),
        ],
        out_shape=[
            jax.ShapeDtypeStruct((rows, WIDTH), BF16),
            jax.ShapeDtypeStruct((rows, WIDTH), F32),
            jax.ShapeDtypeStruct((rows, WIDTH), F32),
            jax.ShapeDtypeStruct((rows, WIDTH), BF16),
        ],
        compiler_params=pltpu.CompilerParams(
            dimension_semantics=("parallel",), vmem_limit_bytes=VMEM_LIMIT),
        name="proj_prompt",
    )(x, w_in, ag, ab, ws, bs_exp)


def _proj_sample_body(x_ref, w_ref, ag_ref, ab_ref, w00_ref, b0_ref,
                      q_ref, k_ref, v_ref, gm_ref, van_ref, wb_ref, u_ref):
    j = pl.program_id(0)
    p, w_hi = _dot_split(x_ref[...], w_ref[...])
    wb_ref[...] = w_hi

    @pl.when(j == 0)
    def _():
        q_ref[...] = p

    @pl.when(j == 1)
    def _():
        k_ref[...] = p

    @pl.when(j == 2)
    def _():
        v_ref[...] = p

    @pl.when(j == 3)
    def _():
        u_ref[...] = p

    @pl.when(j == 4)
    def _():
        van = _head_layer_norm(p, ag_ref[...], ab_ref[...])
        van_ref[...] = van
        gm_ref[...] = u_ref[...] * (w00_ref[...] * van + b0_ref[...])


def _proj_sample(x, w_in, ag, ab, w00, b0):
    rows = x.shape[0]
    whole = lambda j: (0, 0)
    w_cols = lambda j: (0, j)
    return pl.pallas_call(
        _proj_sample_body,
        grid=(5,),
        in_specs=[
            pl.BlockSpec((rows, D_MODEL), whole),
            pl.BlockSpec((D_MODEL, WIDTH), w_cols),
            pl.BlockSpec((1, WIDTH), whole),
            pl.BlockSpec((1, WIDTH), whole),
            pl.BlockSpec((1, WIDTH), whole),
            pl.BlockSpec((1, WIDTH), whole),
        ],
        out_specs=[pl.BlockSpec((rows, WIDTH), whole)] * 5 + [pl.BlockSpec((D_MODEL, WIDTH), w_cols)],
        out_shape=[
            jax.ShapeDtypeStruct((rows, WIDTH), F32),
            jax.ShapeDtypeStruct((rows, WIDTH), F32),
            jax.ShapeDtypeStruct((rows, WIDTH), F32),
            jax.ShapeDtypeStruct((rows, WIDTH), F32),
            jax.ShapeDtypeStruct((rows, WIDTH), F32),
            jax.ShapeDtypeStruct(w_in.shape, BF16),
        ],
        scratch_shapes=[pltpu.VMEM((rows, WIDTH), F32)],
        compiler_params=pltpu.CompilerParams(
            dimension_semantics=("arbitrary",), vmem_limit_bytes=VMEM_LIMIT),
        name="proj_sample",
    )(x, w_in, ag, ab, w00, b0)


def _out_ln_body(a_ref, m_ref, h_ref, wa_ref, wm_ref, g_ref, b_ref, o_ref):
    mix = _dot(a_ref[...], wa_ref[...]) + _dot(m_ref[...], wm_ref[...])
    o_ref[...] = _layer_norm(DN_ALPHA * h_ref[...] + mix, g_ref[...], b_ref[...])


def _out_ln(attn, gm, h, w_out, g, b, *, tm):
    rows = h.shape[0]
    row_blk = lambda i: (i, 0)
    const = lambda i: (0, 0)
    return pl.pallas_call(
        _out_ln_body,
        grid=(rows // tm,),
        in_specs=[
            pl.BlockSpec((tm, WIDTH), row_blk),
            pl.BlockSpec((tm, WIDTH), row_blk),
            pl.BlockSpec((tm, D_MODEL), row_blk),
            pl.BlockSpec((WIDTH, D_MODEL), const),
            pl.BlockSpec((WIDTH, D_MODEL), lambda i: (1, 0)),
            pl.BlockSpec((1, D_MODEL), const),
            pl.BlockSpec((1, D_MODEL), const),
        ],
        out_specs=pl.BlockSpec((tm, D_MODEL), row_blk),
        out_shape=jax.ShapeDtypeStruct((rows, D_MODEL), F32),
        compiler_params=pltpu.CompilerParams(
            dimension_semantics=("parallel",), vmem_limit_bytes=VMEM_LIMIT),
        name="out_prompt",
    )(attn, gm, h, w_out, w_out, g, b)


def _out_ln_sample_body(x_ref, h_ref, w_ref, g_ref, b_ref, o_ref, wb_ref, acc_ref):
    j = pl.program_id(0)

    @pl.when(j == 0)
    def _():
        acc_ref[...] = jnp.zeros_like(acc_ref)

    part, wb_ref[...] = _dot_cast(x_ref[...], w_ref[...])
    acc_ref[...] += part

    @pl.when(j == pl.num_programs(0) - 1)
    def _():
        o_ref[...] = _layer_norm(DN_ALPHA * h_ref[...] + acc_ref[...], g_ref[...], b_ref[...])


def _out_ln_sample(mix, h, w_out, g, b, *, tk):
    rows = h.shape[0]
    whole = lambda j: (0, 0)
    slab = pl.BlockSpec((tk, D_MODEL), lambda j: (j, 0))
    return pl.pallas_call(
        _out_ln_sample_body,
        grid=(w_out.shape[0] // tk,),
        in_specs=[pl.BlockSpec((rows, tk), lambda j: (0, j)), pl.BlockSpec((rows, D_MODEL), whole), slab,
                  pl.BlockSpec((1, D_MODEL), whole), pl.BlockSpec((1, D_MODEL), whole)],
        out_specs=[pl.BlockSpec((rows, D_MODEL), whole), slab],
        out_shape=[jax.ShapeDtypeStruct((rows, D_MODEL), F32), jax.ShapeDtypeStruct(w_out.shape, BF16)],
        scratch_shapes=[pltpu.VMEM((rows, D_MODEL), F32)],
        compiler_params=pltpu.CompilerParams(
            dimension_semantics=("arbitrary",), vmem_limit_bytes=VMEM_LIMIT),
        name="out_sample",
    )(mix, h, w_out, g, b)


CHUNK_BLOCKS = 4
KEY_CHUNK = CHUNK_BLOCKS * MOBA_BLOCK
N_SEL_ROWS = 32
N_POS_ROWS = 16
SPLIT = 3
V_ROWS = HEAD_DIM + 16
STAGE_LOOKAHEAD = 3
N_STAGE_BUFS = 4


def _top_blocks_t(gate, n_valid):
    blk = lax.broadcasted_iota(jnp.int32, gate.shape, 0).astype(F32)
    valid = blk < n_valid
    g = jnp.where(valid, gate, -jnp.inf)
    sel = jnp.zeros(gate.shape, dtype=jnp.bool_)
    for _ in range(MOBA_TOPK):
        mx = jnp.max(g, axis=0, keepdims=True)
        first = jnp.min(jnp.where(g == mx, blk, float(gate.shape[0])), axis=0, keepdims=True)
        hit = blk == first
        sel = sel | (hit & valid)
        g = jnp.where(hit, -jnp.inf, g)
    return sel


def _attn_prompt_body(slope_ref, q_ref, k_ref, v_ref, o_ref,
                      ka_ref, vt_ref, km_ref, pos_ref, qa_ref, m_ref, acc_ref,
                      u_ref, p_ref, mx_ref, alpha_ref):
    h = pl.program_id(0)
    i = pl.program_id(1)
    n_blocks = k_ref.shape[0] // MOBA_BLOCK
    slope2 = slope_ref[h] * LOG2E

    @pl.when(i == 0)
    def _():
        km_ref[...] = jnp.zeros_like(km_ref)
        lane = lax.broadcasted_iota(jnp.int32, (MOBA_BLOCK, LANES), 1)
        offs = lax.broadcasted_iota(jnp.int32, (MOBA_BLOCK, LANES), 0).astype(F32)
        tail = lax.broadcasted_iota(jnp.int32, (V_ROWS - HEAD_DIM, MOBA_BLOCK), 0)
        ones_row = jnp.where(tail == 0, 1.0, 0.0).astype(BF16)

        def prep(n, carry):
            rows = pl.ds(pl.multiple_of(n * MOBA_BLOCK, MOBA_BLOCK), MOBA_BLOCK)
            kblk = k_ref[rows, :]
            in_chunk = lax.convert_element_type(n % CHUNK_BLOCKS, F32)
            aug = jnp.where(lane == n, 1.0, 0.0)
            aug = jnp.where((lane >= N_SEL_ROWS) & (lane < N_SEL_ROWS + SPLIT), in_chunk, aug)
            aug = jnp.where((lane >= N_SEL_ROWS + SPLIT) & (lane < N_SEL_ROWS + 2 * SPLIT), offs, aug)
            ka_ref[rows, :HEAD_DIM] = kblk.astype(BF16)
            ka_ref[rows, HEAD_DIM:] = aug.astype(BF16)
            vt_ref[n, :HEAD_DIM] = v_ref[rows, :].T.astype(BF16)
            vt_ref[n, HEAD_DIM:] = ones_row
            km_ref[pl.ds(n, 1), :] = jnp.mean(kblk, axis=0, keepdims=True)
            return carry

        lax.fori_loop(0, n_blocks, prep, 0)
        r = lax.broadcasted_iota(jnp.int32, pos_ref.shape, 0)
        rest = jnp.full(pos_ref.shape, slope2, F32)
        rows = jnp.zeros(pos_ref.shape, F32)
        for t in range(SPLIT):
            term = rest.astype(BF16).astype(F32)
            rest = rest - term
            rows = jnp.where(r == t, term * MOBA_BLOCK, jnp.where(r == t + SPLIT, term, rows))
        pos_ref[...] = rows.astype(BF16)

    qt = q_ref[...].astype(F32).T.astype(BF16)
    gate = _dot(km_ref[...].astype(BF16), qt)
    own = i * CHUNK_BLOCKS + lax.broadcasted_iota(jnp.int32, (1, KEY_CHUNK), 1) // MOBA_BLOCK
    sel = _top_blocks_t(gate, own.astype(F32))
    blk = lax.broadcasted_iota(jnp.int32, gate.shape, 0)
    qa_ref[:HEAD_DIM, :] = qt
    qa_ref[HEAD_DIM:HEAD_DIM + N_SEL_ROWS, :] = jnp.where(sel | (blk == own), 0.0, MASKED_BF16).astype(BF16)
    qa_ref[HEAD_DIM + N_SEL_ROWS:HEAD_DIM + N_SEL_ROWS + N_POS_ROWS, :] = jnp.concatenate(
        [pos_ref[...]] * CHUNK_BLOCKS, axis=1)
    qa_ref[HEAD_DIM + N_SEL_ROWS + N_POS_ROWS:, :] = jnp.zeros(
        (HEAD_DIM - N_SEL_ROWS - N_POS_ROWS, KEY_CHUNK), BF16)

    half = KEY_CHUNK // 2
    key = lax.broadcasted_iota(jnp.int32, (half, MOBA_BLOCK), 0)
    qry = lax.broadcasted_iota(jnp.int32, (half, MOBA_BLOCK), 1)

    def stage_logits(c, j, buf, diagonal):
        cols = slice(j * MOBA_BLOCK, (j + 1) * MOBA_BLOCK)
        mx = None
        for s in range(2):
            rows = pl.ds(pl.multiple_of(c * KEY_CHUNK + s * half, half), half)
            u = _dot(ka_ref[rows, :], qa_ref[:, cols])
            if diagonal:
                u = jnp.where(key + s * half <= qry + j * MOBA_BLOCK, u, MASKED)
            u_ref[buf, s * half:(s + 1) * half, :] = u
            hmax = jnp.max(u, axis=0, keepdims=True)
            mx = hmax if mx is None else jnp.maximum(mx, hmax)
        mx_ref[buf] = mx

    def stage_softmax(c, j, buf, diagonal):
        cols = slice(j * MOBA_BLOCK, (j + 1) * MOBA_BLOCK)
        blocks_ahead = (c - i) * CHUNK_BLOCKS - j
        off = slope2 * lax.convert_element_type(blocks_ahead * MOBA_BLOCK, F32)
        m_new = mx_ref[buf] + off
        if not diagonal:
            m_old = m_ref[:, cols]
            m_new = jnp.maximum(m_old, m_new)
            alpha_ref[buf] = jnp.exp2(m_old - m_new)
        p_ref[buf] = jnp.exp2((u_ref[buf] - (m_new - off)).astype(BF16))
        m_ref[:, cols] = m_new

    def stage_values(c, j, buf, diagonal):
        cols = slice(j * MOBA_BLOCK, (j + 1) * MOBA_BLOCK)
        parts = []
        for s in range(2):
            pv = None
            for n in range(s * CHUNK_BLOCKS // 2, (s + 1) * CHUNK_BLOCKS // 2):
                d = _dot(vt_ref[c * CHUNK_BLOCKS + n], p_ref[buf, n * MOBA_BLOCK:(n + 1) * MOBA_BLOCK, :])
                pv = d if pv is None else pv + d
            parts.append(pv)
        pv = parts[0] + parts[1]
        acc_ref[:, cols] = pv if diagonal else alpha_ref[buf] * acc_ref[:, cols] + pv

    def sweep(chunks, diagonal):
        stages = [(c, j) for c in chunks for j in range(CHUNK_BLOCKS)]
        for s, (c, j) in enumerate(stages[:STAGE_LOOKAHEAD]):
            stage_logits(c, j, s % N_STAGE_BUFS, diagonal)
        for s, (c, j) in enumerate(stages):
            if s + STAGE_LOOKAHEAD < len(stages):
                nc, nj = stages[s + STAGE_LOOKAHEAD]
                stage_logits(nc, nj, (s + STAGE_LOOKAHEAD) % N_STAGE_BUFS, diagonal)
            stage_softmax(c, j, s % N_STAGE_BUFS, diagonal)
            stage_values(c, j, s % N_STAGE_BUFS, diagonal)

    sweep([i], True)

    def past_pair(cc, carry):
        sweep([2 * cc, 2 * cc + 1], False)
        return carry

    lax.fori_loop(0, i // 2, past_pair, 0)

    @pl.when(i % 2 == 1)
    def _():
        sweep([i - 1], False)

    o_ref[...] = (acc_ref[:HEAD_DIM] / acc_ref[HEAD_DIM:HEAD_DIM + 1]).T.astype(o_ref.dtype)


def _attn_prompt(slopes, q, k, v):
    t_len = q.shape[0]
    n_blocks = t_len // MOBA_BLOCK
    assert n_blocks <= N_SEL_ROWS and t_len % KEY_CHUNK == 0
    grid_spec = pltpu.PrefetchScalarGridSpec(
        num_scalar_prefetch=1,
        grid=(N_HEADS, t_len // KEY_CHUNK),
        in_specs=[
            pl.BlockSpec((KEY_CHUNK, HEAD_DIM), lambda h, i, s: (i, h)),
            pl.BlockSpec((t_len, HEAD_DIM), lambda h, i, s: (0, h)),
            pl.BlockSpec((t_len, HEAD_DIM), lambda h, i, s: (0, h)),
        ],
        out_specs=pl.BlockSpec((KEY_CHUNK, HEAD_DIM), lambda h, i, s: (i, h)),
        scratch_shapes=[
            pltpu.VMEM((t_len, 2 * HEAD_DIM), BF16),
            pltpu.VMEM((n_blocks, V_ROWS, MOBA_BLOCK), BF16),
            pltpu.VMEM((N_SEL_ROWS, HEAD_DIM), F32),
            pltpu.VMEM((N_POS_ROWS, MOBA_BLOCK), BF16),
            pltpu.VMEM((2 * HEAD_DIM, KEY_CHUNK), BF16),
            pltpu.VMEM((1, KEY_CHUNK), F32),
            pltpu.VMEM((V_ROWS, KEY_CHUNK), F32),
            pltpu.VMEM((N_STAGE_BUFS, KEY_CHUNK, MOBA_BLOCK), F32),
            pltpu.VMEM((N_STAGE_BUFS, KEY_CHUNK, MOBA_BLOCK), BF16),
            pltpu.VMEM((N_STAGE_BUFS, 1, MOBA_BLOCK), F32),
            pltpu.VMEM((N_STAGE_BUFS, 1, MOBA_BLOCK), F32),
        ],
    )
    return pl.pallas_call(
        _attn_prompt_body,
        grid_spec=grid_spec,
        out_shape=jax.ShapeDtypeStruct((t_len, WIDTH), BF16),
        compiler_params=pltpu.CompilerParams(
            dimension_semantics=("parallel", "arbitrary"), vmem_limit_bytes=VMEM_LIMIT),
        name="attn_prompt",
    )(slopes, q, k, v)


GROUP_PAGES = 12
GROUP_BLOCKS = GROUP_PAGES // PAGES_PER_BLOCK


class _SelectStep:
    def __init__(self, t, pt_ref, q_ref, ck_ref, sel_ref, buf_ref, sem_ref, stage_ref, ksum_ref):
        self.t, self.pt_ref, self.q_ref, self.ck_ref = t, pt_ref, q_ref, ck_ref
        self.sel_ref, self.buf_ref, self.sem_ref = sel_ref, buf_ref, sem_ref
        self.stage_ref, self.ksum_ref = stage_ref, ksum_ref
        self.n_seq, self.n_pages = pt_ref.shape
        self.total_pages = self.n_seq * self.n_pages
        self.total_blocks = self.total_pages // PAGES_PER_BLOCK
        self.n_steps = -(-self.total_pages // GROUP_PAGES)
        self.active = t < self.n_steps
        self.slot = t % 2

    def _page_copy(self, flat, slot, p):
        flat = jnp.minimum(flat, self.total_pages - 1)
        page = self.pt_ref[flat // self.n_pages, flat % self.n_pages]
        return pltpu.make_async_copy(
            self.ck_ref.at[page, :, 0], self.buf_ref.at[slot, pl.ds(p * PAGE_SIZE, PAGE_SIZE)],
            self.sem_ref.at[slot])

    def _step_copies(self, t, slot):
        return [self._page_copy(t * GROUP_PAGES + p, slot, p) for p in range(GROUP_PAGES)]

    def copies(self):
        @pl.when(self.t == 0)
        def _():
            for c in self._step_copies(0, 0):
                c.start()

        @pl.when(self.t + 1 < self.n_steps)
        def _():
            for c in self._step_copies(self.t + 1, 1 - self.slot):
                c.start()

        @pl.when(self.active)
        def _():
            for c in self._step_copies(self.t, self.slot):
                c.wait()

    def block_sums(self, part):
        for n in range(part * GROUP_BLOCKS // 3, (part + 1) * GROUP_BLOCKS // 3):
            blk = self.buf_ref[self.slot, pl.ds(n * MOBA_BLOCK, MOBA_BLOCK)]
            self.stage_ref[n] = jnp.sum(blk, axis=0)

    def pick(self):
        n_full = self.n_pages // PAGES_PER_BLOCK
        for n in range(GROUP_BLOCKS):
            flat = self.t * GROUP_BLOCKS + n
            self.ksum_ref[jnp.where(flat < self.total_blocks, flat, self.total_blocks + n)] = self.stage_ref[n]

        def pick_seq(b, carry):
            ksum = self.ksum_ref[pl.ds(pl.multiple_of(b * n_full, n_full), n_full)]
            kmean = ksum / MOBA_BLOCK
            gate = jnp.sum(kmean * self.q_ref[b], axis=-1, keepdims=True)
            blk_id = lax.broadcasted_iota(jnp.int32, gate.shape, 0).astype(F32)
            lane = lax.broadcasted_iota(jnp.int32, (N_HEADS, LANES), 1)
            picks = jnp.zeros((N_HEADS, LANES), F32)
            for j in range(MOBA_TOPK):
                mx = jnp.max(gate, axis=0, keepdims=True)
                first = jnp.min(jnp.where(gate == mx, blk_id, float(n_full)), axis=0, keepdims=True)
                picks = jnp.where(lane == j, first[0], picks)
                gate = jnp.where(blk_id == first, -jnp.inf, gate)
            self.sel_ref[b] = picks.astype(jnp.int32)
            return carry

        @pl.when(self.t == self.n_steps - 1)
        def _():
            lax.fori_loop(0, self.n_seq, pick_seq, 0)


def _ffn_ln_select_body(pt_ref, x_ref, wg_ref, wu_ref, wd_ref, g_ref, b_ref, q_ref, ck_ref,
                        o_ref, sel_ref, xb_ref, buf_ref, sem_ref, stage_ref, ksum_ref):
    t = pl.program_id(0) * pl.num_programs(1) + pl.program_id(1)
    step = _SelectStep(t, pt_ref, q_ref, ck_ref, sel_ref, buf_ref, sem_ref, stage_ref, ksum_ref)
    step.copies()
    _ffn_ln_body(x_ref, wg_ref, wu_ref, wd_ref, g_ref, b_ref, o_ref, xb_ref, side_work=step.block_sums)
    step.pick()


def _ffn_ln_select(x, wg, wu, wd, g, b, page_table, q_sample, cache_k, *, tm, tf):
    rows = x.shape[0]
    n_seq, n_pages = page_table.shape
    total_blocks = n_seq * n_pages // PAGES_PER_BLOCK
    grid = (rows // tm, D_FF // tf)
    assert n_pages % PAGES_PER_BLOCK == 0 and grid[0] * grid[1] * GROUP_PAGES >= n_seq * n_pages
    once = pl.Buffered(1)
    grid_spec = pltpu.PrefetchScalarGridSpec(
        num_scalar_prefetch=1,
        grid=grid,
        in_specs=[
            pl.BlockSpec((tm, D_MODEL), lambda i, j, pt: (i, 0)),
            pl.BlockSpec((D_MODEL, tf), lambda i, j, pt: (0, j)),
            pl.BlockSpec((D_MODEL, tf), lambda i, j, pt: (0, j)),
            pl.BlockSpec((tf, D_MODEL), lambda i, j, pt: (j, 0)),
            pl.BlockSpec((1, D_MODEL), lambda i, j, pt: (0, 0)),
            pl.BlockSpec((1, D_MODEL), lambda i, j, pt: (0, 0)),
            pl.BlockSpec((n_seq, N_HEADS, HEAD_DIM), lambda i, j, pt: (0, 0, 0)),
            pl.BlockSpec(memory_space=pl.ANY),
        ],
        out_specs=[
            pl.BlockSpec((tm, D_MODEL), lambda i, j, pt: (i, 0), pipeline_mode=once),
            pl.BlockSpec((n_seq, N_HEADS, LANES), lambda i, j, pt: (0, 0, 0)),
        ],
        scratch_shapes=[
            pltpu.VMEM((tm, D_MODEL), BF16),
            pltpu.VMEM((2, GROUP_PAGES * PAGE_SIZE, N_HEADS, HEAD_DIM), F32),
            pltpu.SemaphoreType.DMA((2,)),
            pltpu.VMEM((GROUP_BLOCKS, N_HEADS, HEAD_DIM), F32),
            pltpu.VMEM((total_blocks + GROUP_BLOCKS, N_HEADS, HEAD_DIM), F32),
        ],
    )
    return pl.pallas_call(
        _ffn_ln_select_body,
        grid_spec=grid_spec,
        out_shape=[jax.ShapeDtypeStruct((rows, D_MODEL), F32),
                   jax.ShapeDtypeStruct((n_seq, N_HEADS, LANES), jnp.int32)],
        compiler_params=pltpu.CompilerParams(
            dimension_semantics=("arbitrary", "arbitrary"), vmem_limit_bytes=VMEM_LIMIT),
        name="ffn1_prompt_select",
    )(page_table, x, wg, wu, wd, g, b, q_sample.reshape(n_seq, N_HEADS, HEAD_DIM), cache_k)


def _attn_sample_body(pt_ref, sel_ref, slope_ref, q_ref, kn_ref, vn_ref, ck_ref, cv_ref, o_ref,
                      kbuf_ref, vbuf_ref, sem_ref, *, past_len):
    b = pl.program_id(0)
    n_sel = MOBA_TOPK
    slot = b % 2

    def gathers(b, slot):
        out = []
        for h in range(N_HEADS):
            for j in range(n_sel):
                for pp in range(PAGES_PER_BLOCK):
                    page = pt_ref[b, sel_ref[b, h * MOBA_TOPK + j] * PAGES_PER_BLOCK + pp]
                    rows = pl.ds((j * PAGES_PER_BLOCK + pp) * PAGE_SIZE, PAGE_SIZE)
                    out.append(pltpu.make_async_copy(
                        ck_ref.at[page, :, 0, h, :], kbuf_ref.at[slot, h, rows, :], sem_ref.at[slot, 0]))
                    out.append(pltpu.make_async_copy(
                        cv_ref.at[page, :, 0, h, :], vbuf_ref.at[slot, h, rows, :], sem_ref.at[slot, 1]))
        return out

    @pl.when(b == 0)
    def _():
        for c in gathers(0, 0):
            c.start()

    @pl.when(b + 1 < pl.num_programs(0))
    def _():
        for c in gathers(b + 1, 1 - slot):
            c.start()

    for c in gathers(b, slot):
        c.wait()

    n_keys = n_sel * MOBA_BLOCK
    lane = lax.broadcasted_iota(jnp.int32, (1, n_keys), 1)
    which = lane // MOBA_BLOCK
    offs = lane % MOBA_BLOCK
    outs = []
    for h in range(N_HEADS):
        sl = slice(h * HEAD_DIM, (h + 1) * HEAD_DIM)
        slope = slope_ref[h]
        qh = q_ref[:, sl]
        kn = kn_ref[:, sl]
        vn = vn_ref[:, sl]
        q8 = jnp.broadcast_to(qh, (8, HEAD_DIM)).astype(BF16)
        s = _dot_nt(q8, kbuf_ref[slot, h].astype(BF16))[:1] * ATTN_SCALE
        blk = jnp.zeros((1, n_keys), jnp.int32)
        for j in range(n_sel):
            blk = jnp.where(which == j, sel_ref[b, h * MOBA_TOPK + j], blk)
        dist = (past_len - (blk * MOBA_BLOCK + offs)).astype(F32)
        s = s - slope * dist
        s_own = jnp.sum(qh * kn, axis=-1, keepdims=True) * ATTN_SCALE
        mx = jnp.maximum(jnp.max(s, axis=-1, keepdims=True), s_own)
        p = jnp.exp(s - mx)
        p_own = jnp.exp(s_own - mx)
        denom = jnp.sum(p, axis=-1, keepdims=True) + p_own
        p8 = jnp.broadcast_to(p, (8, n_keys)).astype(BF16)
        pv = _dot(p8, vbuf_ref[slot, h].astype(BF16))[:1]
        outs.append((pv + p_own * vn) / denom)
    o_ref[...] = jnp.concatenate(outs, axis=-1)


def _attn_sample(page_table, sel, slopes, q, k_new, v_new, cache_k, cache_v):
    n_seq, n_pages = page_table.shape
    per_seq = pl.BlockSpec((None, 1, WIDTH), lambda b, *_: (b, 0, 0))
    grid_spec = pltpu.PrefetchScalarGridSpec(
        num_scalar_prefetch=3,
        grid=(n_seq,),
        in_specs=[
            per_seq, per_seq, per_seq,
            pl.BlockSpec(memory_space=pl.ANY),
            pl.BlockSpec(memory_space=pl.ANY),
        ],
        out_specs=per_seq,
        scratch_shapes=[
            pltpu.VMEM((2, N_HEADS, MOBA_TOPK * MOBA_BLOCK, HEAD_DIM), F32),
            pltpu.VMEM((2, N_HEADS, MOBA_TOPK * MOBA_BLOCK, HEAD_DIM), F32),
            pltpu.SemaphoreType.DMA((2, 2)),
        ],
    )
    return pl.pallas_call(
        functools.partial(_attn_sample_body, past_len=n_pages * PAGE_SIZE),
        grid_spec=grid_spec,
        out_shape=jax.ShapeDtypeStruct((n_seq, 1, WIDTH), F32),
        compiler_params=pltpu.CompilerParams(
            dimension_semantics=("arbitrary",), vmem_limit_bytes=VMEM_LIMIT),
        name="attn_sample",
    )(page_table, sel, slopes, q.reshape(n_seq, 1, WIDTH), k_new.reshape(n_seq, 1, WIDTH),
      v_new.reshape(n_seq, 1, WIDTH), cache_k, cache_v).reshape(n_seq, WIDTH)


def kernel(x_prompt, x_sample, cache_k, cache_v, page_table, ffn1_w_gate, ffn1_w_up, ffn1_w_down, ln1_g, ln1_b, w_in, a_norm_g, a_norm_b, a_spatial_w, a_spatial_b, w_out, ln2_g, ln2_b, ffn2_w_gate, ffn2_w_up, ffn2_w_down, ln3_g, ln3_b):
    assert x_prompt.shape[0] == 1 and x_sample.shape[1] == 1 and w_in.shape[0] == DEPTH == 1
    t_len = x_prompt.shape[1]
    n_seq = x_sample.shape[0]
    xp = x_prompt.reshape(t_len, D_MODEL)
    xs = x_sample.reshape(n_seq, D_MODEL)
    slopes = jnp.exp2(-8.0 * jnp.arange(1, N_HEADS + 1, dtype=F32) / N_HEADS)

    row = lambda v: v.reshape(1, -1)
    ln1, ln2, ln3 = ((row(g[0]), row(b[0])) for g, b in ((ln1_g, ln1_b), (ln2_g, ln2_b), (ln3_g, ln3_b)))
    ag, ab = row(a_norm_g[0]), row(a_norm_b[0])
    ws, bs = a_spatial_w[0], a_spatial_b[0]
    bs_exp = jnp.repeat(bs.T, HEAD_DIM, axis=1)
    w00 = row(jnp.repeat(ws[:, 0, 0], HEAD_DIM))
    b0 = row(jnp.repeat(bs[:, 0], HEAD_DIM))

    hs, w1g, w1u, w1d = _ffn_ln_sample(xs, ffn1_w_gate[0], ffn1_w_up[0], ffn1_w_down[0], *ln1, tf=512)
    qs, ks, vs, gm_s, van_s, w_in_b = _proj_sample(hs, w_in[0], ag, ab, w00, b0)
    hp, sel = _ffn_ln_select(xp, w1g, w1u, w1d, *ln1, page_table, qs, cache_k, tm=1024, tf=256)
    qp, kp, vp, gm_p = _proj_prompt(hp, w_in_b, ag, ab, ws, bs_exp, tm=512)

    sel = sel[:, :, :MOBA_TOPK].reshape(n_seq, N_HEADS * MOBA_TOPK)
    attn_s = _attn_sample(page_table, sel, slopes, qs, ks, vs, cache_k, cache_v)
    attn_p = _attn_prompt(slopes, qp, kp, vp)

    hs, w_out_b = _out_ln_sample(jnp.concatenate([attn_s, gm_s], axis=-1), hs, w_out[0], *ln2, tk=512)
    hp = _out_ln(attn_p, gm_p, hp, w_out_b, *ln2, tm=512)

    yp, ys = _ffn_ln_pair(hp, hs, ffn2_w_gate[0], ffn2_w_up[0], ffn2_w_down[0], *ln3, tm=1024, tf=256)

    heads = (1, N_HEADS, HEAD_DIM)
    return (yp.reshape(1, t_len, D_MODEL), ys.reshape(n_seq, 1, D_MODEL),
            kp.reshape((1, t_len) + heads), vp.reshape((1, t_len) + heads),
            ks.reshape((n_seq, 1) + heads), vs.reshape((n_seq, 1) + heads),
            van_s.reshape((n_seq, 1) + heads))
```

```python
import functools

import jax
import jax.numpy as jnp
from jax import lax
from jax.experimental import pallas as pl
from jax.experimental.pallas import tpu as pltpu

D_MODEL = 2048
HEAD_DIM = 128
N_HEADS = 8
WIDTH = N_HEADS * HEAD_DIM
CHUNK = 128
MOBA_BLOCK = 256
MOBA_TOPK = 3
PAGE_SIZE = 128
PAGES_PER_BLOCK = MOBA_BLOCK // PAGE_SIZE
D_FF = 5632
LN_EPS = 1e-5
DEPTH = 1
DN_ALPHA = (2 * DEPTH) ** 0.25
ATTN_SCALE = HEAD_DIM ** -0.5
LOG2E = 1.4426950408889634

MASKED = -1e30
MASKED_BF16 = -(2.0 ** 100)

LANES = 128
VMEM_LIMIT = 61 * 1024 * 1024

BF16 = jnp.bfloat16
F32 = jnp.float32


def _layer_norm(x, g, b):
    mu = jnp.mean(x, axis=-1, keepdims=True)
    xc = x - mu
    var = jnp.mean(xc * xc, axis=-1, keepdims=True)
    return xc * lax.rsqrt(var + LN_EPS) * g + b


def _dot(a, b):
    return jnp.dot(a, b, preferred_element_type=F32)


def _dot_nt(a, b):
    return lax.dot_general(a, b, (((1,), (1,)), ((), ())), preferred_element_type=F32)


def _ffn_ln_body(x_ref, wg_ref, wu_ref, wd_ref, g_ref, b_ref, o_ref, xb_ref, side_work=None,
                 f32_weights=False):
    j = pl.program_id(1)
    weight = (lambda ref: ref[...].astype(BF16)) if f32_weights else (lambda ref: ref[...])

    @pl.when(j == 0)
    def _():
        xb_ref[...] = x_ref[...].astype(BF16)
        o_ref[...] = jnp.zeros_like(o_ref)

    side = side_work if side_work is not None else (lambda part: None)
    xb = xb_ref[...]
    side(0)
    gate = _dot(xb, weight(wg_ref))
    side(1)
    up = _dot(xb, weight(wu_ref))
    act = gate * jax.nn.sigmoid(gate) * up
    side(2)
    o_ref[...] += _dot(act.astype(BF16), weight(wd_ref))

    @pl.when(j == pl.num_programs(1) - 1)
    def _():
        y = DN_ALPHA * x_ref[...] + 0.5 * o_ref[...]
        o_ref[...] = _layer_norm(y, g_ref[...], b_ref[...])


def _ffn_ln_pair_body(x_ref, xs_ref, wg_ref, wu_ref, wd_ref, g_ref, b_ref, o_ref, os_ref, xb_ref, xsb_ref):
    _ffn_ln_body(x_ref, wg_ref, wu_ref, wd_ref, g_ref, b_ref, o_ref, xb_ref, f32_weights=True)

    @pl.when(pl.program_id(0) == pl.num_programs(0) - 1)
    def _():
        _ffn_ln_body(xs_ref, wg_ref, wu_ref, wd_ref, g_ref, b_ref, os_ref, xsb_ref, f32_weights=True)


def _ffn_ln_pair(x, xs, wg, wu, wd, g, b, *, tm, tf):
    rows, rows_s = x.shape[0], xs.shape[0]
    n_tiles = rows // tm
    tile = lambda i, j: (i, 0)
    const = lambda i, j: (0, 0)
    return pl.pallas_call(
        _ffn_ln_pair_body,
        grid=(n_tiles, D_FF // tf),
        in_specs=[
            pl.BlockSpec((tm, D_MODEL), tile),
            pl.BlockSpec((rows_s, D_MODEL), const),
            pl.BlockSpec((D_MODEL, tf), lambda i, j: (0, j)),
            pl.BlockSpec((D_MODEL, tf), lambda i, j: (0, j)),
            pl.BlockSpec((tf, D_MODEL), lambda i, j: (j, 0)),
            pl.BlockSpec((1, D_MODEL), const),
            pl.BlockSpec((1, D_MODEL), const),
        ],
        out_specs=[pl.BlockSpec((tm, D_MODEL), tile), pl.BlockSpec((rows_s, D_MODEL), const)],
        out_shape=[jax.ShapeDtypeStruct((rows, D_MODEL), F32), jax.ShapeDtypeStruct((rows_s, D_MODEL), F32)],
        scratch_shapes=[pltpu.VMEM((tm, D_MODEL), BF16), pltpu.VMEM((rows_s, D_MODEL), BF16)],
        compiler_params=pltpu.CompilerParams(
            dimension_semantics=("arbitrary", "arbitrary"), vmem_limit_bytes=VMEM_LIMIT),
        name="ffn2",
    )(x, xs, wg, wu, wd, g, b)


def _split(x):
    hi = x.astype(BF16)
    return hi, (x - hi.astype(F32)).astype(BF16)


def _dot_split(x, w):
    x_hi, x_lo = _split(x)
    w_hi, w_lo = _split(w)
    n = x.shape[0]
    both = _dot(jnp.concatenate([x_hi, x_lo], axis=0), w_hi)
    return both[:n] + (both[n:] + _dot(x_hi, w_lo)), w_hi


def _dot_cast(x, w):
    w_hi = w.astype(BF16)
    return _dot(x.astype(BF16), w_hi), w_hi


def _ffn_ln_sample_body(x_ref, wg_ref, wu_ref, wd_ref, g_ref, b_ref,
                        o_ref, wgb_ref, wub_ref, wdb_ref, acc_ref):
    j = pl.program_id(0)
    dot = _dot_split

    @pl.when(j == 0)
    def _():
        acc_ref[...] = jnp.zeros_like(acc_ref)

    x = x_ref[...]
    gate, wgb_ref[...] = dot(x, wg_ref[...])
    up, wub_ref[...] = dot(x, wu_ref[...])
    down, wdb_ref[...] = dot(gate * jax.nn.sigmoid(gate) * up, wd_ref[...])
    acc_ref[...] += down

    @pl.when(j == pl.num_programs(0) - 1)
    def _():
        y = DN_ALPHA * x + 0.5 * acc_ref[...]
        o_ref[...] = _layer_norm(y, g_ref[...], b_ref[...])


def _ffn_ln_sample(x, wg, wu, wd, g, b, *, tf):
    rows = x.shape[0]
    whole = lambda j: (0, 0)
    cols = pl.BlockSpec((D_MODEL, tf), lambda j: (0, j))
    rws = pl.BlockSpec((tf, D_MODEL), lambda j: (j, 0))
    return pl.pallas_call(
        _ffn_ln_sample_body,
        grid=(D_FF // tf,),
        in_specs=[pl.BlockSpec((rows, D_MODEL), whole), cols, cols, rws,
                  pl.BlockSpec((1, D_MODEL), whole), pl.BlockSpec((1, D_MODEL), whole)],
        out_specs=[pl.BlockSpec((rows, D_MODEL), whole), cols, cols, rws],
        out_shape=[jax.ShapeDtypeStruct((rows, D_MODEL), F32),
                   jax.ShapeDtypeStruct(wg.shape, BF16),
                   jax.ShapeDtypeStruct(wu.shape, BF16),
                   jax.ShapeDtypeStruct(wd.shape, BF16)],
        scratch_shapes=[pltpu.VMEM((rows, D_MODEL), F32)],
        compiler_params=pltpu.CompilerParams(
            dimension_semantics=("arbitrary",), vmem_limit_bytes=VMEM_LIMIT),
        name="ffn1_sample",
    )(x, wg, wu, wd, g, b)


def _head_layer_norm(va, g, b):
    parts = []
    for h in range(N_HEADS):
        sl = slice(h * HEAD_DIM, (h + 1) * HEAD_DIM)
        parts.append(_layer_norm(va[:, sl], g[:, sl], b[:, sl]))
    return jnp.concatenate(parts, axis=-1)


def _proj_prompt_body(x_ref, w_ref, ag_ref, ab_ref, ws_ref, bs_ref, q_ref, k_ref, v_ref, gm_ref):
    xb = x_ref[...].astype(BF16)

    def project(group):
        return _dot(xb, w_ref[:, group * WIDTH:(group + 1) * WIDTH])

    u = project(3)
    van = _head_layer_norm(project(4), ag_ref[...], ab_ref[...]).astype(BF16)
    q_ref[...] = (project(0) * (ATTN_SCALE * LOG2E)).astype(BF16)
    k_ref[...] = project(1)
    v_ref[...] = project(2)

    row = lax.broadcasted_iota(jnp.int32, (CHUNK, CHUNK), 0)
    col = lax.broadcasted_iota(jnp.int32, (CHUNK, CHUNK), 1)
    causal = row >= col
    n_chunks = u.shape[0] // CHUNK
    for h in range(N_HEADS):
        sl = slice(h * HEAD_DIM, (h + 1) * HEAD_DIM)
        w_h = jnp.where(causal, ws_ref[h], 0.0).astype(BF16)
        for c in range(n_chunks):
            rs = slice(c * CHUNK, (c + 1) * CHUNK)
            mixed = _dot(w_h, van[rs, sl]) + bs_ref[:, sl]
            gm_ref[rs, sl] = (u[rs, sl] * mixed).astype(BF16)


def _proj_prompt(x, w_in, ag, ab, ws, bs_exp, *, tm):
    rows = x.shape[0]
    row_blk = lambda i: (i, 0)
    const2 = lambda i: (0, 0)
    return pl.pallas_call(
        _proj_prompt_body,
        grid=(rows // tm,),
        in_specs=[
            pl.BlockSpec((tm, D_MODEL), row_blk),
            pl.BlockSpec(w_in.shape, const2, pipeline_mode=pl.Buffered(1)),
            pl.BlockSpec((1, WIDTH), const2),
            pl.BlockSpec((1, WIDTH), const2),
            pl.BlockSpec((N_HEADS, CHUNK, CHUNK), lambda i: (0, 0, 0)),
            pl.BlockSpec((CHUNK, WIDTH), const2),
        ],
        out_specs=[
            pl.BlockSpec((tm, WIDTH), row_blk),
            pl.BlockSpec((tm, WIDTH), row_blk),
            pl.BlockSpec((tm, WIDTH), row_blk),
            pl.BlockSpec((tm, WIDTH), row_blk),
        ],
        out_shape=[
            jax.ShapeDtypeStruct((rows, WIDTH), BF16),
            jax.ShapeDtypeStruct((rows, WIDTH), F32),
            jax.ShapeDtypeStruct((rows, WIDTH), F32),
            jax.ShapeDtypeStruct((rows, WIDTH), BF16),
        ],
        compiler_params=pltpu.CompilerParams(
            dimension_semantics=("parallel",), vmem_limit_bytes=VMEM_LIMIT),
        name="proj_prompt",
    )(x, w_in, ag, ab, ws, bs_exp)


def _proj_sample_body(x_ref, w_ref, ag_ref, ab_ref, w00_ref, b0_ref,
                      q_ref, k_ref, v_ref, gm_ref, van_ref, wb_ref, u_ref):
    j = pl.program_id(0)
    p, w_hi = _dot_split(x_ref[...], w_ref[...])
    wb_ref[...] = w_hi

    @pl.when(j == 0)
    def _():
        q_ref[...] = p

    @pl.when(j == 1)
    def _():
        k_ref[...] = p

    @pl.when(j == 2)
    def _():
        v_ref[...] = p

    @pl.when(j == 3)
    def _():
        u_ref[...] = p

    @pl.when(j == 4)
    def _():
        van = _head_layer_norm(p, ag_ref[...], ab_ref[...])
        van_ref[...] = van
        gm_ref[...] = u_ref[...] * (w00_ref[...] * van + b0_ref[...])


def _proj_sample(x, w_in, ag, ab, w00, b0):
    rows = x.shape[0]
    whole = lambda j: (0, 0)
    w_cols = lambda j: (0, j)
    return pl.pallas_call(
        _proj_sample_body,
        grid=(5,),
        in_specs=[
            pl.BlockSpec((rows, D_MODEL), whole),
            pl.BlockSpec((D_MODEL, WIDTH), w_cols),
            pl.BlockSpec((1, WIDTH), whole),
            pl.BlockSpec((1, WIDTH), whole),
            pl.BlockSpec((1, WIDTH), whole),
            pl.BlockSpec((1, WIDTH), whole),
        ],
        out_specs=[pl.BlockSpec((rows, WIDTH), whole)] * 5 + [pl.BlockSpec((D_MODEL, WIDTH), w_cols)],
        out_shape=[
            jax.ShapeDtypeStruct((rows, WIDTH), F32),
            jax.ShapeDtypeStruct((rows, WIDTH), F32),
            jax.ShapeDtypeStruct((rows, WIDTH), F32),
            jax.ShapeDtypeStruct((rows, WIDTH), F32),
            jax.ShapeDtypeStruct((rows, WIDTH), F32),
            jax.ShapeDtypeStruct(w_in.shape, BF16),
        ],
        scratch_shapes=[pltpu.VMEM((rows, WIDTH), F32)],
        compiler_params=pltpu.CompilerParams(
            dimension_semantics=("arbitrary",), vmem_limit_bytes=VMEM_LIMIT),
        name="proj_sample",
    )(x, w_in, ag, ab, w00, b0)


def _out_ln_body(a_ref, m_ref, h_ref, wa_ref, wm_ref, g_ref, b_ref, o_ref):
    mix = _dot(a_ref[...], wa_ref[...]) + _dot(m_ref[...], wm_ref[...])
    o_ref[...] = _layer_norm(DN_ALPHA * h_ref[...] + mix, g_ref[...], b_ref[...])


def _out_ln(attn, gm, h, w_out, g, b, *, tm):
    rows = h.shape[0]
    row_blk = lambda i: (i, 0)
    const = lambda i: (0, 0)
    return pl.pallas_call(
        _out_ln_body,
        grid=(rows // tm,),
        in_specs=[
            pl.BlockSpec((tm, WIDTH), row_blk),
            pl.BlockSpec((tm, WIDTH), row_blk),
            pl.BlockSpec((tm, D_MODEL), row_blk),
            pl.BlockSpec((WIDTH, D_MODEL), const),
            pl.BlockSpec((WIDTH, D_MODEL), lambda i: (1, 0)),
            pl.BlockSpec((1, D_MODEL), const),
            pl.BlockSpec((1, D_MODEL), const),
        ],
        out_specs=pl.BlockSpec((tm, D_MODEL), row_blk),
        out_shape=jax.ShapeDtypeStruct((rows, D_MODEL), F32),
        compiler_params=pltpu.CompilerParams(
            dimension_semantics=("parallel",), vmem_limit_bytes=VMEM_LIMIT),
        name="out_prompt",
    )(attn, gm, h, w_out, w_out, g, b)


def _out_ln_sample_body(x_ref, h_ref, w_ref, g_ref, b_ref, o_ref, wb_ref, acc_ref):
    j = pl.program_id(0)

    @pl.when(j == 0)
    def _():
        acc_ref[...] = jnp.zeros_like(acc_ref)

    part, wb_ref[...] = _dot_cast(x_ref[...], w_ref[...])
    acc_ref[...] += part

    @pl.when(j == pl.num_programs(0) - 1)
    def _():
        o_ref[...] = _layer_norm(DN_ALPHA * h_ref[...] + acc_ref[...], g_ref[...], b_ref[...])


def _out_ln_sample(mix, h, w_out, g, b, *, tk):
    rows = h.shape[0]
    whole = lambda j: (0, 0)
    slab = pl.BlockSpec((tk, D_MODEL), lambda j: (j, 0))
    return pl.pallas_call(
        _out_ln_sample_body,
        grid=(w_out.shape[0] // tk,),
        in_specs=[pl.BlockSpec((rows, tk), lambda j: (0, j)), pl.BlockSpec((rows, D_MODEL), whole), slab,
                  pl.BlockSpec((1, D_MODEL), whole), pl.BlockSpec((1, D_MODEL), whole)],
        out_specs=[pl.BlockSpec((rows, D_MODEL), whole), slab],
        out_shape=[jax.ShapeDtypeStruct((rows, D_MODEL), F32), jax.ShapeDtypeStruct(w_out.shape, BF16)],
        scratch_shapes=[pltpu.VMEM((rows, D_MODEL), F32)],
        compiler_params=pltpu.CompilerParams(
            dimension_semantics=("arbitrary",), vmem_limit_bytes=VMEM_LIMIT),
        name="out_sample",
    )(mix, h, w_out, g, b)


CHUNK_BLOCKS = 4
KEY_CHUNK = CHUNK_BLOCKS * MOBA_BLOCK
N_SEL_ROWS = 32
N_POS_ROWS = 16
SPLIT = 3
V_ROWS = HEAD_DIM + 16
STAGE_LOOKAHEAD = 3
N_STAGE_BUFS = 4


def _top_blocks_t(gate, n_valid):
    blk = lax.broadcasted_iota(jnp.int32, gate.shape, 0).astype(F32)
    valid = blk < n_valid
    g = jnp.where(valid, gate, -jnp.inf)
    sel = jnp.zeros(gate.shape, dtype=jnp.bool_)
    for _ in range(MOBA_TOPK):
        mx = jnp.max(g, axis=0, keepdims=True)
        first = jnp.min(jnp.where(g == mx, blk, float(gate.shape[0])), axis=0, keepdims=True)
        hit = blk == first
        sel = sel | (hit & valid)
        g = jnp.where(hit, -jnp.inf, g)
    return sel


def _attn_prompt_body(slope_ref, q_ref, k_ref, v_ref, o_ref,
                      ka_ref, vt_ref, km_ref, pos_ref, qa_ref, m_ref, acc_ref,
                      u_ref, p_ref, mx_ref, alpha_ref):
    h = pl.program_id(0)
    i = pl.program_id(1)
    n_blocks = k_ref.shape[0] // MOBA_BLOCK
    slope2 = slope_ref[h] * LOG2E

    @pl.when(i == 0)
    def _():
        km_ref[...] = jnp.zeros_like(km_ref)
        lane = lax.broadcasted_iota(jnp.int32, (MOBA_BLOCK, LANES), 1)
        offs = lax.broadcasted_iota(jnp.int32, (MOBA_BLOCK, LANES), 0).astype(F32)
        tail = lax.broadcasted_iota(jnp.int32, (V_ROWS - HEAD_DIM, MOBA_BLOCK), 0)
        ones_row = jnp.where(tail == 0, 1.0, 0.0).astype(BF16)

        def prep(n, carry):
            rows = pl.ds(pl.multiple_of(n * MOBA_BLOCK, MOBA_BLOCK), MOBA_BLOCK)
            kblk = k_ref[rows, :]
            in_chunk = lax.convert_element_type(n % CHUNK_BLOCKS, F32)
            aug = jnp.where(lane == n, 1.0, 0.0)
            aug = jnp.where((lane >= N_SEL_ROWS) & (lane < N_SEL_ROWS + SPLIT), in_chunk, aug)
            aug = jnp.where((lane >= N_SEL_ROWS + SPLIT) & (lane < N_SEL_ROWS + 2 * SPLIT), offs, aug)
            ka_ref[rows, :HEAD_DIM] = kblk.astype(BF16)
            ka_ref[rows, HEAD_DIM:] = aug.astype(BF16)
            vt_ref[n, :HEAD_DIM] = v_ref[rows, :].T.astype(BF16)
            vt_ref[n, HEAD_DIM:] = ones_row
            km_ref[pl.ds(n, 1), :] = jnp.mean(kblk, axis=0, keepdims=True)
            return carry

        lax.fori_loop(0, n_blocks, prep, 0)
        r = lax.broadcasted_iota(jnp.int32, pos_ref.shape, 0)
        rest = jnp.full(pos_ref.shape, slope2, F32)
        rows = jnp.zeros(pos_ref.shape, F32)
        for t in range(SPLIT):
            term = rest.astype(BF16).astype(F32)
            rest = rest - term
            rows = jnp.where(r == t, term * MOBA_BLOCK, jnp.where(r == t + SPLIT, term, rows))
        pos_ref[...] = rows.astype(BF16)

    qt = q_ref[...].astype(F32).T.astype(BF16)
    gate = _dot(km_ref[...].astype(BF16), qt)
    own = i * CHUNK_BLOCKS + lax.broadcasted_iota(jnp.int32, (1, KEY_CHUNK), 1) // MOBA_BLOCK
    sel = _top_blocks_t(gate, own.astype(F32))
    blk = lax.broadcasted_iota(jnp.int32, gate.shape, 0)
    qa_ref[:HEAD_DIM, :] = qt
    qa_ref[HEAD_DIM:HEAD_DIM + N_SEL_ROWS, :] = jnp.where(sel | (blk == own), 0.0, MASKED_BF16).astype(BF16)
    qa_ref[HEAD_DIM + N_SEL_ROWS:HEAD_DIM + N_SEL_ROWS + N_POS_ROWS, :] = jnp.concatenate(
        [pos_ref[...]] * CHUNK_BLOCKS, axis=1)
    qa_ref[HEAD_DIM + N_SEL_ROWS + N_POS_ROWS:, :] = jnp.zeros(
        (HEAD_DIM - N_SEL_ROWS - N_POS_ROWS, KEY_CHUNK), BF16)

    half = KEY_CHUNK // 2
    key = lax.broadcasted_iota(jnp.int32, (half, MOBA_BLOCK), 0)
    qry = lax.broadcasted_iota(jnp.int32, (half, MOBA_BLOCK), 1)

    def stage_logits(c, j, buf, diagonal):
        cols = slice(j * MOBA_BLOCK, (j + 1) * MOBA_BLOCK)
        mx = None
        for s in range(2):
            rows = pl.ds(pl.multiple_of(c * KEY_CHUNK + s * half, half), half)
            u = _dot(ka_ref[rows, :], qa_ref[:, cols])
            if diagonal:
                u = jnp.where(key + s * half <= qry + j * MOBA_BLOCK, u, MASKED)
            u_ref[buf, s * half:(s + 1) * half, :] = u
            hmax = jnp.max(u, axis=0, keepdims=True)
            mx = hmax if mx is None else jnp.maximum(mx, hmax)
        mx_ref[buf] = mx

    def stage_softmax(c, j, buf, diagonal):
        cols = slice(j * MOBA_BLOCK, (j + 1) * MOBA_BLOCK)
        blocks_ahead = (c - i) * CHUNK_BLOCKS - j
        off = slope2 * lax.convert_element_type(blocks_ahead * MOBA_BLOCK, F32)
        m_new = mx_ref[buf] + off
        if not diagonal:
            m_old = m_ref[:, cols]
            m_new = jnp.maximum(m_old, m_new)
            alpha_ref[buf] = jnp.exp2(m_old - m_new)
        p_ref[buf] = jnp.exp2((u_ref[buf] - (m_new - off)).astype(BF16))
        m_ref[:, cols] = m_new

    def stage_values(c, j, buf, diagonal):
        cols = slice(j * MOBA_BLOCK, (j + 1) * MOBA_BLOCK)
        parts = []
        for s in range(2):
            pv = None
            for n in range(s * CHUNK_BLOCKS // 2, (s + 1) * CHUNK_BLOCKS // 2):
                d = _dot(vt_ref[c * CHUNK_BLOCKS + n], p_ref[buf, n * MOBA_BLOCK:(n + 1) * MOBA_BLOCK, :])
                pv = d if pv is None else pv + d
            parts.append(pv)
        pv = parts[0] + parts[1]
        acc_ref[:, cols] = pv if diagonal else alpha_ref[buf] * acc_ref[:, cols] + pv

    def sweep(chunks, diagonal):
        stages = [(c, j) for c in chunks for j in range(CHUNK_BLOCKS)]
        for s, (c, j) in enumerate(stages[:STAGE_LOOKAHEAD]):
            stage_logits(c, j, s % N_STAGE_BUFS, diagonal)
        for s, (c, j) in enumerate(stages):
            if s + STAGE_LOOKAHEAD < len(stages):
                nc, nj = stages[s + STAGE_LOOKAHEAD]
                stage_logits(nc, nj, (s + STAGE_LOOKAHEAD) % N_STAGE_BUFS, diagonal)
            stage_softmax(c, j, s % N_STAGE_BUFS, diagonal)
            stage_values(c, j, s % N_STAGE_BUFS, diagonal)

    sweep([i], True)

    def past_pair(cc, carry):
        sweep([2 * cc, 2 * cc + 1], False)
        return carry

    lax.fori_loop(0, i // 2, past_pair, 0)

    @pl.when(i % 2 == 1)
    def _():
        sweep([i - 1], False)

    o_ref[...] = (acc_ref[:HEAD_DIM] / acc_ref[HEAD_DIM:HEAD_DIM + 1]).T.astype(o_ref.dtype)


def _attn_prompt(slopes, q, k, v):
    t_len = q.shape[0]
    n_blocks = t_len // MOBA_BLOCK
    assert n_blocks <= N_SEL_ROWS and t_len % KEY_CHUNK == 0
    grid_spec = pltpu.PrefetchScalarGridSpec(
        num_scalar_prefetch=1,
        grid=(N_HEADS, t_len // KEY_CHUNK),
        in_specs=[
            pl.BlockSpec((KEY_CHUNK, HEAD_DIM), lambda h, i, s: (i, h)),
            pl.BlockSpec((t_len, HEAD_DIM), lambda h, i, s: (0, h)),
            pl.BlockSpec((t_len, HEAD_DIM), lambda h, i, s: (0, h)),
        ],
        out_specs=pl.BlockSpec((KEY_CHUNK, HEAD_DIM), lambda h, i, s: (i, h)),
        scratch_shapes=[
            pltpu.VMEM((t_len, 2 * HEAD_DIM), BF16),
            pltpu.VMEM((n_blocks, V_ROWS, MOBA_BLOCK), BF16),
            pltpu.VMEM((N_SEL_ROWS, HEAD_DIM), F32),
            pltpu.VMEM((N_POS_ROWS, MOBA_BLOCK), BF16),
            pltpu.VMEM((2 * HEAD_DIM, KEY_CHUNK), BF16),
            pltpu.VMEM((1, KEY_CHUNK), F32),
            pltpu.VMEM((V_ROWS, KEY_CHUNK), F32),
            pltpu.VMEM((N_STAGE_BUFS, KEY_CHUNK, MOBA_BLOCK), F32),
            pltpu.VMEM((N_STAGE_BUFS, KEY_CHUNK, MOBA_BLOCK), BF16),
            pltpu.VMEM((N_STAGE_BUFS, 1, MOBA_BLOCK), F32),
            pltpu.VMEM((N_STAGE_BUFS, 1, MOBA_BLOCK), F32),
        ],
    )
    return pl.pallas_call(
        _attn_prompt_body,
        grid_spec=grid_spec,
        out_shape=jax.ShapeDtypeStruct((t_len, WIDTH), BF16),
        compiler_params=pltpu.CompilerParams(
            dimension_semantics=("parallel", "arbitrary"), vmem_limit_bytes=VMEM_LIMIT),
        name="attn_prompt",
    )(slopes, q, k, v)


GROUP_PAGES = 12
GROUP_BLOCKS = GROUP_PAGES // PAGES_PER_BLOCK


class _SelectStep:
    def __init__(self, t, pt_ref, q_ref, ck_ref, sel_ref, buf_ref, sem_ref, stage_ref, ksum_ref):
        self.t, self.pt_ref, self.q_ref, self.ck_ref = t, pt_ref, q_ref, ck_ref
        self.sel_ref, self.buf_ref, self.sem_ref = sel_ref, buf_ref, sem_ref
        self.stage_ref, self.ksum_ref = stage_ref, ksum_ref
        self.n_seq, self.n_pages = pt_ref.shape
        self.total_pages = self.n_seq * self.n_pages
        self.total_blocks = self.total_pages // PAGES_PER_BLOCK
        self.n_steps = -(-self.total_pages // GROUP_PAGES)
        self.active = t < self.n_steps
        self.slot = t % 2

    def _page_copy(self, flat, slot, p):
        flat = jnp.minimum(flat, self.total_pages - 1)
        page = self.pt_ref[flat // self.n_pages, flat % self.n_pages]
        return pltpu.make_async_copy(
            self.ck_ref.at[page, :, 0], self.buf_ref.at[slot, pl.ds(p * PAGE_SIZE, PAGE_SIZE)],
            self.sem_ref.at[slot])

    def _step_copies(self, t, slot):
        return [self._page_copy(t * GROUP_PAGES + p, slot, p) for p in range(GROUP_PAGES)]

    def copies(self):
        @pl.when(self.t == 0)
        def _():
            for c in self._step_copies(0, 0):
                c.start()

        @pl.when(self.t + 1 < self.n_steps)
        def _():
            for c in self._step_copies(self.t + 1, 1 - self.slot):
                c.start()

        @pl.when(self.active)
        def _():
            for c in self._step_copies(self.t, self.slot):
                c.wait()

    def block_sums(self, part):
        for n in range(part * GROUP_BLOCKS // 3, (part + 1) * GROUP_BLOCKS // 3):
            blk = self.buf_ref[self.slot, pl.ds(n * MOBA_BLOCK, MOBA_BLOCK)]
            self.stage_ref[n] = jnp.sum(blk, axis=0)

    def pick(self):
        n_full = self.n_pages // PAGES_PER_BLOCK
        for n in range(GROUP_BLOCKS):
            flat = self.t * GROUP_BLOCKS + n
            self.ksum_ref[jnp.where(flat < self.total_blocks, flat, self.total_blocks + n)] = self.stage_ref[n]

        def pick_seq(b, carry):
            ksum = self.ksum_ref[pl.ds(pl.multiple_of(b * n_full, n_full), n_full)]
            kmean = ksum / MOBA_BLOCK
            gate = jnp.sum(kmean * self.q_ref[b], axis=-1, keepdims=True)
            blk_id = lax.broadcasted_iota(jnp.int32, gate.shape, 0).astype(F32)
            lane = lax.broadcasted_iota(jnp.int32, (N_HEADS, LANES), 1)
            picks = jnp.zeros((N_HEADS, LANES), F32)
            for j in range(MOBA_TOPK):
                mx = jnp.max(gate, axis=0, keepdims=True)
                first = jnp.min(jnp.where(gate == mx, blk_id, float(n_full)), axis=0, keepdims=True)
                picks = jnp.where(lane == j, first[0], picks)
                gate = jnp.where(blk_id == first, -jnp.inf, gate)
            self.sel_ref[b] = picks.astype(jnp.int32)
            return carry

        @pl.when(self.t == self.n_steps - 1)
        def _():
            lax.fori_loop(0, self.n_seq, pick_seq, 0)


def _ffn_ln_select_body(pt_ref, x_ref, wg_ref, wu_ref, wd_ref, g_ref, b_ref, q_ref, ck_ref,
                        o_ref, sel_ref, xb_ref, buf_ref, sem_ref, stage_ref, ksum_ref):
    t = pl.program_id(0) * pl.num_programs(1) + pl.program_id(1)
    step = _SelectStep(t, pt_ref, q_ref, ck_ref, sel_ref, buf_ref, sem_ref, stage_ref, ksum_ref)
    step.copies()
    _ffn_ln_body(x_ref, wg_ref, wu_ref, wd_ref, g_ref, b_ref, o_ref, xb_ref, side_work=step.block_sums)
    step.pick()


def _ffn_ln_select(x, wg, wu, wd, g, b, page_table, q_sample, cache_k, *, tm, tf):
    rows = x.shape[0]
    n_seq, n_pages = page_table.shape
    total_blocks = n_seq * n_pages // PAGES_PER_BLOCK
    grid = (rows // tm, D_FF // tf)
    assert n_pages % PAGES_PER_BLOCK == 0 and grid[0] * grid[1] * GROUP_PAGES >= n_seq * n_pages
    once = pl.Buffered(1)
    grid_spec = pltpu.PrefetchScalarGridSpec(
        num_scalar_prefetch=1,
        grid=grid,
        in_specs=[
            pl.BlockSpec((tm, D_MODEL), lambda i, j, pt: (i, 0)),
            pl.BlockSpec((D_MODEL, tf), lambda i, j, pt: (0, j)),
            pl.BlockSpec((D_MODEL, tf), lambda i, j, pt: (0, j)),
            pl.BlockSpec((tf, D_MODEL), lambda i, j, pt: (j, 0)),
            pl.BlockSpec((1, D_MODEL), lambda i, j, pt: (0, 0)),
            pl.BlockSpec((1, D_MODEL), lambda i, j, pt: (0, 0)),
            pl.BlockSpec((n_seq, N_HEADS, HEAD_DIM), lambda i, j, pt: (0, 0, 0)),
            pl.BlockSpec(memory_space=pl.ANY),
        ],
        out_specs=[
            pl.BlockSpec((tm, D_MODEL), lambda i, j, pt: (i, 0), pipeline_mode=once),
            pl.BlockSpec((n_seq, N_HEADS, LANES), lambda i, j, pt: (0, 0, 0)),
        ],
        scratch_shapes=[
            pltpu.VMEM((tm, D_MODEL), BF16),
            pltpu.VMEM((2, GROUP_PAGES * PAGE_SIZE, N_HEADS, HEAD_DIM), F32),
            pltpu.SemaphoreType.DMA((2,)),
            pltpu.VMEM((GROUP_BLOCKS, N_HEADS, HEAD_DIM), F32),
            pltpu.VMEM((total_blocks + GROUP_BLOCKS, N_HEADS, HEAD_DIM), F32),
        ],
    )
    return pl.pallas_call(
        _ffn_ln_select_body,
        grid_spec=grid_spec,
        out_shape=[jax.ShapeDtypeStruct((rows, D_MODEL), F32),
                   jax.ShapeDtypeStruct((n_seq, N_HEADS, LANES), jnp.int32)],
        compiler_params=pltpu.CompilerParams(
            dimension_semantics=("arbitrary", "arbitrary"), vmem_limit_bytes=VMEM_LIMIT),
        name="ffn1_prompt_select",
    )(page_table, x, wg, wu, wd, g, b, q_sample.reshape(n_seq, N_HEADS, HEAD_DIM), cache_k)


def _attn_sample_body(pt_ref, sel_ref, slope_ref, q_ref, kn_ref, vn_ref, ck_ref, cv_ref, o_ref,
                      kbuf_ref, vbuf_ref, sem_ref, *, past_len):
    b = pl.program_id(0)
    n_sel = MOBA_TOPK
    slot = b % 2

    def gathers(b, slot):
        out = []
        for h in range(N_HEADS):
            for j in range(n_sel):
                for pp in range(PAGES_PER_BLOCK):
                    page = pt_ref[b, sel_ref[b, h * MOBA_TOPK + j] * PAGES_PER_BLOCK + pp]
                    rows = pl.ds((j * PAGES_PER_BLOCK + pp) * PAGE_SIZE, PAGE_SIZE)
                    out.append(pltpu.make_async_copy(
                        ck_ref.at[page, :, 0, h, :], kbuf_ref.at[slot, h, rows, :], sem_ref.at[slot, 0]))
                    out.append(pltpu.make_async_copy(
                        cv_ref.at[page, :, 0, h, :], vbuf_ref.at[slot, h, rows, :], sem_ref.at[slot, 1]))
        return out

    @pl.when(b == 0)
    def _():
        for c in gathers(0, 0):
            c.start()

    @pl.when(b + 1 < pl.num_programs(0))
    def _():
        for c in gathers(b + 1, 1 - slot):
            c.start()

    for c in gathers(b, slot):
        c.wait()

    n_keys = n_sel * MOBA_BLOCK
    lane = lax.broadcasted_iota(jnp.int32, (1, n_keys), 1)
    which = lane // MOBA_BLOCK
    offs = lane % MOBA_BLOCK
    outs = []
    for h in range(N_HEADS):
        sl = slice(h * HEAD_DIM, (h + 1) * HEAD_DIM)
        slope = slope_ref[h]
        qh = q_ref[:, sl]
        kn = kn_ref[:, sl]
        vn = vn_ref[:, sl]
        q8 = jnp.broadcast_to(qh, (8, HEAD_DIM)).astype(BF16)
        s = _dot_nt(q8, kbuf_ref[slot, h].astype(BF16))[:1] * ATTN_SCALE
        blk = jnp.zeros((1, n_keys), jnp.int32)
        for j in range(n_sel):
            blk = jnp.where(which == j, sel_ref[b, h * MOBA_TOPK + j], blk)
        dist = (past_len - (blk * MOBA_BLOCK + offs)).astype(F32)
        s = s - slope * dist
        s_own = jnp.sum(qh * kn, axis=-1, keepdims=True) * ATTN_SCALE
        mx = jnp.maximum(jnp.max(s, axis=-1, keepdims=True), s_own)
        p = jnp.exp(s - mx)
        p_own = jnp.exp(s_own - mx)
        denom = jnp.sum(p, axis=-1, keepdims=True) + p_own
        p8 = jnp.broadcast_to(p, (8, n_keys)).astype(BF16)
        pv = _dot(p8, vbuf_ref[slot, h].astype(BF16))[:1]
        outs.append((pv + p_own * vn) / denom)
    o_ref[...] = jnp.concatenate(outs, axis=-1)


def _attn_sample(page_table, sel, slopes, q, k_new, v_new, cache_k, cache_v):
    n_seq, n_pages = page_table.shape
    per_seq = pl.BlockSpec((None, 1, WIDTH), lambda b, *_: (b, 0, 0))
    grid_spec = pltpu.PrefetchScalarGridSpec(
        num_scalar_prefetch=3,
        grid=(n_seq,),
        in_specs=[
            per_seq, per_seq, per_seq,
            pl.BlockSpec(memory_space=pl.ANY),
            pl.BlockSpec(memory_space=pl.ANY),
        ],
        out_specs=per_seq,
        scratch_shapes=[
            pltpu.VMEM((2, N_HEADS, MOBA_TOPK * MOBA_BLOCK, HEAD_DIM), F32),
            pltpu.VMEM((2, N_HEADS, MOBA_TOPK * MOBA_BLOCK, HEAD_DIM), F32),
            pltpu.SemaphoreType.DMA((2, 2)),
        ],
    )
    return pl.pallas_call(
        functools.partial(_attn_sample_body, past_len=n_pages * PAGE_SIZE),
        grid_spec=grid_spec,
        out_shape=jax.ShapeDtypeStruct((n_seq, 1, WIDTH), F32),
        compiler_params=pltpu.CompilerParams(
            dimension_semantics=("arbitrary",), vmem_limit_bytes=VMEM_LIMIT),
        name="attn_sample",
    )(page_table, sel, slopes, q.reshape(n_seq, 1, WIDTH), k_new.reshape(n_seq, 1, WIDTH),
      v_new.reshape(n_seq, 1, WIDTH), cache_k, cache_v).reshape(n_seq, WIDTH)


def kernel(x_prompt, x_sample, cache_k, cache_v, page_table, ffn1_w_gate, ffn1_w_up, ffn1_w_down, ln1_g, ln1_b, w_in, a_norm_g, a_norm_b, a_spatial_w, a_spatial_b, w_out, ln2_g, ln2_b, ffn2_w_gate, ffn2_w_up, ffn2_w_down, ln3_g, ln3_b):
    assert x_prompt.shape[0] == 1 and x_sample.shape[1] == 1 and w_in.shape[0] == DEPTH == 1
    t_len = x_prompt.shape[1]
    n_seq = x_sample.shape[0]
    xp = x_prompt.reshape(t_len, D_MODEL)
    xs = x_sample.reshape(n_seq, D_MODEL)
    slopes = jnp.exp2(-8.0 * jnp.arange(1, N_HEADS + 1, dtype=F32) / N_HEADS)

    row = lambda v: v.reshape(1, -1)
    ln1, ln2, ln3 = ((row(g[0]), row(b[0])) for g, b in ((ln1_g, ln1_b), (ln2_g, ln2_b), (ln3_g, ln3_b)))
    ag, ab = row(a_norm_g[0]), row(a_norm_b[0])
    ws, bs = a_spatial_w[0], a_spatial_b[0]
    bs_exp = jnp.repeat(bs.T, HEAD_DIM, axis=1)
    w00 = row(jnp.repeat(ws[:, 0, 0], HEAD_DIM))
    b0 = row(jnp.repeat(bs[:, 0], HEAD_DIM))

    hs, w1g, w1u, w1d = _ffn_ln_sample(xs, ffn1_w_gate[0], ffn1_w_up[0], ffn1_w_down[0], *ln1, tf=512)
    qs, ks, vs, gm_s, van_s, w_in_b = _proj_sample(hs, w_in[0], ag, ab, w00, b0)
    hp, sel = _ffn_ln_select(xp, w1g, w1u, w1d, *ln1, page_table, qs, cache_k, tm=1024, tf=256)
    qp, kp, vp, gm_p = _proj_prompt(hp, w_in_b, ag, ab, ws, bs_exp, tm=512)

    sel = sel[:, :, :MOBA_TOPK].reshape(n_seq, N_HEADS * MOBA_TOPK)
    attn_s = _attn_sample(page_table, sel, slopes, qs, ks, vs, cache_k, cache_v)
    attn_p = _attn_prompt(slopes, qp, kp, vp)

    hs, w_out_b = _out_ln_sample(jnp.concatenate([attn_s, gm_s], axis=-1), hs, w_out[0], *ln2, tk=512)
    hp = _out_ln(attn_p, gm_p, hp, w_out_b, *ln2, tm=512)

    yp, ys = _ffn_ln_pair(hp, hs, ffn2_w_gate[0], ffn2_w_up[0], ffn2_w_down[0], *ln3, tm=1024, tf=256)

    heads = (1, N_HEADS, HEAD_DIM)
    return (yp.reshape(1, t_len, D_MODEL), ys.reshape(n_seq, 1, D_MODEL),
            kp.reshape((1, t_len) + heads), vp.reshape((1, t_len) + heads),
            ks.reshape((n_seq, 1) + heads), vs.reshape((n_seq, 1) + heads),
            van_s.reshape((n_seq, 1) + heads))
```
